```python
import jax, jax.numpy as jnp
from jax import lax
import numpy as np


D_MODEL = 1024
BATCH = 8
SEQ = 2048
DEPTH = 2
DEC_BATCH = 128
DEC_SEQ = 4
PAST_LEN = 2048
PAGE_SIZE = 128

HEAD_DIM = 64
FOX_H = 8
RWKV_H = 4
RET_H = 4
FOX_W = FOX_H * HEAD_DIM
RWKV_W = RWKV_H * HEAD_DIM
RET_W = RET_H * HEAD_DIM
MIX_W = FOX_W + RWKV_W + RET_W
RWKV_LORA_W = 32
RWKV_LORA_A = 32
RWKV_LORA_G = 64
FOX_PROJ = 3 * FOX_W + FOX_H
RWKV_PROJ = 3 * RWKV_W + RWKV_LORA_W + RWKV_LORA_A + RWKV_LORA_G
RET_PROJ = 4 * RET_W
PROJ_W = FOX_PROJ + RWKV_PROJ + RET_PROJ
D_FF = -(-8 * D_MODEL // (3 * 256)) * 256
Q_BLOCK = 128
RET_CHUNK = 128
ROPE_BASE = 10000.0
NORM_EPS = 1e-6
RWKV_GN_EPS = 64e-5
RET_GN_EPS = 1e-5
FORGET_BIAS_LO = 1.0
FORGET_BIAS_HI = 4.0
RWKV_W0_CENTER = -2.0

kernel_name = 'fox_rwkv7_retention_hybrid_step'

F32 = jnp.float32


def _split(x, sizes):
    offs = np.cumsum(np.array(sizes))[:-1].tolist()
    return jnp.split(x, offs, axis=-1)


def rms_norm(x, g, eps=NORM_EPS):
    xf = x.astype(F32)
    y = xf * lax.rsqrt(jnp.mean(xf * xf, -1, keepdims=True) + eps) * g.astype(F32)
    return y.astype(x.dtype)


def head_norm(x, w, b, eps):
    xf = x.astype(F32)
    xc = xf - jnp.mean(xf, -1, keepdims=True)
    y = xc * lax.rsqrt(jnp.mean(xc * xc, -1, keepdims=True) + eps) * w.astype(F32)
    return y if b is None else y + b.astype(F32)


def rotary(x, pos):
    half = HEAD_DIM // 2
    inv = ROPE_BASE ** (-jnp.arange(half, dtype=F32) / half)
    ang = pos.astype(F32)[:, None] * inv[None, :]
    cos = jnp.cos(ang)[None, :, None, :]
    sin = jnp.sin(ang)[None, :, None, :]
    x1 = x[..., :half].astype(F32)
    x2 = x[..., half:].astype(F32)
    return jnp.concatenate([x1 * cos - x2 * sin, x2 * cos + x1 * sin], -1).astype(x.dtype)


def fox_prompt(q, k, v, logf):
    B, T, H, D = q.shape
    c = jnp.cumsum(logf, axis=1)
    c_keys = c.transpose(0, 2, 1)
    nb = T // Q_BLOCK
    qb = q.reshape(B, nb, Q_BLOCK, H, D).transpose(1, 0, 2, 3, 4)
    cb = c.reshape(B, nb, Q_BLOCK, H).transpose(1, 0, 3, 2)
    key_pos = jnp.arange(T)
    scale = D ** -0.5

    def block(args):
        i, q_i, c_i = args
        s = jnp.einsum('bqhd,bkhd->bhqk', q_i, k, preferred_element_type=F32) * scale
        s = s + (c_i[..., :, None] - c_keys[..., None, :])
        qpos = i * Q_BLOCK + jnp.arange(Q_BLOCK)
        s = jnp.where(key_pos[None, :] <= qpos[:, None], s, -jnp.inf)
        p = jax.nn.softmax(s, axis=-1)
        return jnp.einsum('bhqk,bkhd->bqhd', p.astype(v.dtype), v)

    o = lax.map(block, (jnp.arange(nb), qb, cb))
    return o.transpose(1, 0, 2, 3, 4).reshape(B, T, H, D)


def fox_sample(q, k, v, logf, k_past, v_past, logf_past):
    B, T, H, D = q.shape
    P = k_past.shape[1]
    scale = D ** -0.5
    c_past = jnp.cumsum(logf_past.astype(F32), axis=1)
    c_new = c_past[:, -1:, :] + jnp.cumsum(logf.astype(F32), axis=1)
    cp = c_past.transpose(0, 2, 1)
    cn = c_new.transpose(0, 2, 1)
    s_past = jnp.einsum('bthd,bshd->bhts', q, k_past, preferred_element_type=F32) * scale
    s_past = s_past + (cn[..., :, None] - cp[..., None, :])
    s_new = jnp.einsum('bthd,bshd->bhts', q, k, preferred_element_type=F32) * scale
    s_new = s_new + (cn[..., :, None] - cn[..., None, :])
    s_new = jnp.where(jnp.tril(jnp.ones((T, T), bool)), s_new, -jnp.inf)
    p = jax.nn.softmax(jnp.concatenate([s_past, s_new], -1), axis=-1).astype(v.dtype)
    return (jnp.einsum('bhts,bshd->bthd', p[..., :P], v_past)
            + jnp.einsum('bhts,bshd->bthd', p[..., P:], v))


def rwkv7(p, shift_prev, S0, lp):
    dt = p.dtype
    B, T, _ = p.shape
    p_prev = jnp.concatenate([shift_prev[:, None, :].astype(dt), p[:, :-1]], axis=1)
    xs = p + (p_prev - p) * lp['rwkv_mu']
    r, k, v, w_lo, a_lo, g_lo = _split(xs, [RWKV_W, RWKV_W, RWKV_W, RWKV_LORA_W, RWKV_LORA_A, RWKV_LORA_G])
    w = -jax.nn.softplus(-(lp['rwkv_w0'] + jnp.tanh(w_lo) @ lp['rwkv_w2'])) - 0.5
    decay = jnp.exp(-jnp.exp(w.astype(F32)))
    a = jax.nn.sigmoid(lp['rwkv_a0'] + a_lo @ lp['rwkv_a2'])
    g = jax.nn.sigmoid(g_lo) @ lp['rwkv_g2']
    heads = lambda t: t.reshape(B, T, RWKV_H, HEAD_DIM).astype(F32)
    kk = heads(k * lp['rwkv_k_k'])
    kk = kk / jnp.maximum(jnp.sqrt(jnp.sum(kk * kk, -1, keepdims=True)), 1e-12)
    k = k * (1 + (a - 1) * lp['rwkv_k_a'])
    r_h, k_h, v_h, w_h, a_h = heads(r), heads(k), heads(v), heads(decay), heads(a)
    a_vec = -kk
    b_vec = kk * a_h

    def step(S, inp):
        r_t, w_t, k_t, v_t, a_t, b_t = inp
        sa = jnp.einsum('bhvk,bhk->bhv', S, a_t)
        S = S * w_t[:, :, None, :] + sa[..., None] * b_t[:, :, None, :] + v_t[..., None] * k_t[:, :, None, :]
        return S, jnp.einsum('bhvk,bhk->bhv', S, r_t)

    tm = lambda t: jnp.moveaxis(t, 1, 0)
    S_T, y = lax.scan(step, S0.astype(F32), (tm(r_h), tm(w_h), tm(k_h), tm(v_h), tm(a_vec), tm(b_vec)))
    y = jnp.moveaxis(y, 0, 1)
    y = head_norm(y, lp['rwkv_lnx_w'].reshape(RWKV_H, HEAD_DIM), lp['rwkv_lnx_b'].reshape(RWKV_H, HEAD_DIM), RWKV_GN_EPS)
    bonus = jnp.sum(r_h * k_h * lp['rwkv_r_k'].astype(F32), -1, keepdims=True) * v_h
    out = ((y + bonus).reshape(B, T, RWKV_W) * g.astype(F32)).astype(dt)
    return out, S_T.astype(dt), p[:, -1]


def retention(q, k, v, R0):
    dt = q.dtype
    B, T, H, D = q.shape
    L = RET_CHUNK if T % RET_CHUNK == 0 else T
    nc = T // L
    log_gamma = jnp.log1p(-jnp.exp2(-5.0 - jnp.arange(H, dtype=F32)))
    idx = jnp.arange(L, dtype=F32)
    diff = idx[:, None] - idx[None, :]
    dmask = jnp.where(diff >= 0, jnp.exp(log_gamma[:, None, None] * jnp.maximum(diff, 0.0)), 0.0)
    cross = jnp.exp(log_gamma[:, None] * (idx + 1.0)).T[None, :, :, None]
    inner_w = jnp.exp(log_gamma[:, None] * (L - 1.0 - idx))
    chunk_decay = jnp.exp(log_gamma * L)[None, :, None, None]
    ch = lambda t: t.astype(F32).reshape(B, nc, L, H, D).transpose(1, 0, 2, 3, 4)

    def step(R, inp):
        qi, ki, vi = inp
        s = jnp.einsum('blhd,bmhd->bhlm', qi, ki) * dmask
        o = jnp.einsum('bhlm,bmhd->blhd', s, vi) + jnp.einsum('blhd,bhde->blhe', qi, R) * cross
        R = R * chunk_decay + jnp.einsum('blhd,blhe,hl->bhde', ki, vi, inner_w)
        return R, o

    R_T, o = lax.scan(step, R0.astype(F32), (ch(q), ch(k), ch(v)))
    return o.transpose(1, 0, 2, 3, 4).reshape(B, T, H, D), R_T.astype(dt)


def mixer(h, pos, lp, fox_past, rwkv_S, rwkv_shift, ret_R):
    dt = h.dtype
    B, T, _ = h.shape
    proj = jnp.einsum('btd,dp->btp', h, lp['w_in'])
    fq, fk, fv, f_lin, rwkv_p, eq, ek, ev, eg = _split(
        proj, [FOX_W, FOX_W, FOX_W, FOX_H, RWKV_PROJ, RET_W, RET_W, RET_W, RET_W])
    heads = lambda t, n: t.reshape(B, T, n, HEAD_DIM)
    fq = rms_norm(heads(fq, FOX_H), lp['fox_qn_g'])
    fk = rms_norm(heads(fk, FOX_H), lp['fox_kn_g'])
    fv = heads(fv, FOX_H)
    logf = jax.nn.log_sigmoid(f_lin.astype(F32) + lp['fox_f_b'].astype(F32))
    if fox_past is None:
        fo = fox_prompt(fq, fk, fv, logf)
    else:
        fo = fox_sample(fq, fk, fv, logf, *fox_past)
    ro, rwkv_S, rwkv_shift = rwkv7(rwkv_p, rwkv_shift, rwkv_S, lp)
    eq = rotary(heads(eq, RET_H), pos)
    ek = rotary(heads(ek, RET_H), pos) * (HEAD_DIM ** -0.5)
    eo, ret_R = retention(eq, ek, heads(ev, RET_H), ret_R)
    eo = head_norm(eo, lp['ret_gn_w'].reshape(RET_H, HEAD_DIM), None, RET_GN_EPS) * jax.nn.silu(heads(eg, RET_H).astype(F32))
    mixed = jnp.concatenate([fo.reshape(B, T, FOX_W).astype(dt), ro, eo.reshape(B, T, RET_W).astype(dt)], -1)
    out = jnp.einsum('btm,md->btd', mixed, lp['w_out'])
    return out, (fk, fv, logf.astype(dt)), rwkv_S, rwkv_shift, ret_R


def layer(x, pos, lp, fox_past, rwkv_S, rwkv_shift, ret_R):
    m, fox_rows, rwkv_S, rwkv_shift, ret_R = mixer(rms_norm(x, lp['ln_mix_g']), pos, lp, fox_past, rwkv_S, rwkv_shift, ret_R)
    x = x + m
    h = rms_norm(x, lp['ln_ffn_g'])
    x = x + (jax.nn.silu(h @ lp['w_gate']) * (h @ lp['w_up'])) @ lp['w_down']
    return x, fox_rows, rwkv_S, rwkv_shift, ret_R


def setup_inputs(seed: int = 0) -> dict:
    key = jax.random.key(seed)
    ks = iter(jax.random.split(key, 48))
    nrm = lambda shape, scale: jax.random.normal(next(ks), shape, F32) * scale
    wgt = lambda shape, fan_in: nrm(shape, fan_in ** -0.5)
    gain = lambda shape: 1.0 + nrm(shape, 0.05)
    n_pages = PAST_LEN // PAGE_SIZE
    n_pool = (DEC_BATCH * n_pages * 5) // 4
    f_sched = jnp.linspace(FORGET_BIAS_LO, FORGET_BIAS_HI, FOX_H, dtype=F32)
    x_prompt = nrm((BATCH, SEQ, D_MODEL), 1.0)
    x_sample = nrm((DEC_BATCH, DEC_SEQ, D_MODEL), 1.0)
    cache_fox_k = nrm((DEPTH, n_pool, PAGE_SIZE, FOX_H, HEAD_DIM), 1.0)
    cache_fox_v = nrm((DEPTH, n_pool, PAGE_SIZE, FOX_H, HEAD_DIM), 1.0)
    cache_fox_logf = jax.nn.log_sigmoid(f_sched + nrm((DEPTH, n_pool, PAGE_SIZE, FOX_H), 1.0))
    state_rwkv = nrm((DEPTH, DEC_BATCH, RWKV_H, HEAD_DIM, HEAD_DIM), 0.5)
    state_rwkv_shift = nrm((DEPTH, DEC_BATCH, RWKV_PROJ), 1.0)
    state_ret = nrm((DEPTH, DEC_BATCH, RET_H, HEAD_DIM, HEAD_DIM), 0.5)
    perm = jax.random.permutation(next(ks), n_pool)[:DEC_BATCH * n_pages]
    page_table = perm.reshape(DEC_BATCH, n_pages).astype(jnp.int32)
    return {
        'x_prompt': x_prompt, 'x_sample': x_sample,
        'cache_fox_k': cache_fox_k, 'cache_fox_v': cache_fox_v, 'cache_fox_logf': cache_fox_logf,
        'state_rwkv': state_rwkv, 'state_rwkv_shift': state_rwkv_shift, 'state_ret': state_ret,
        'page_table': page_table,
        'ln_mix_g': gain((DEPTH, D_MODEL)),
        'w_in': wgt((DEPTH, D_MODEL, PROJ_W), D_MODEL),
        'fox_qn_g': gain((DEPTH, HEAD_DIM)),
        'fox_kn_g': gain((DEPTH, HEAD_DIM)),
        'fox_f_b': f_sched[None, :] + nrm((DEPTH, FOX_H), 0.1),
        'rwkv_mu': jax.random.uniform(next(ks), (DEPTH, RWKV_PROJ), F32),
        'rwkv_w0': RWKV_W0_CENTER + nrm((DEPTH, RWKV_W), 0.5),
        'rwkv_w2': nrm((DEPTH, RWKV_LORA_W, RWKV_W), 0.1 * RWKV_LORA_W ** -0.5),
        'rwkv_a0': nrm((DEPTH, RWKV_W), 0.1),
        'rwkv_a2': wgt((DEPTH, RWKV_LORA_A, RWKV_W), RWKV_LORA_A),
        'rwkv_g2': wgt((DEPTH, RWKV_LORA_G, RWKV_W), RWKV_LORA_G),
        'rwkv_k_k': gain((DEPTH, RWKV_W)),
        'rwkv_k_a': gain((DEPTH, RWKV_W)),
        'rwkv_r_k': nrm((DEPTH, RWKV_H, HEAD_DIM), 0.1),
        'rwkv_lnx_w': gain((DEPTH, RWKV_W)),
        'rwkv_lnx_b': nrm((DEPTH, RWKV_W), 0.02),
        'ret_gn_w': gain((DEPTH, RET_W)),
        'w_out': wgt((DEPTH, MIX_W, D_MODEL), MIX_W),
        'ln_ffn_g': gain((DEPTH, D_MODEL)),
        'w_gate': wgt((DEPTH, D_MODEL, D_FF), D_MODEL),
        'w_up': wgt((DEPTH, D_MODEL, D_FF), D_MODEL),
        'w_down': wgt((DEPTH, D_FF, D_MODEL), D_FF),
    }


def reference(x_prompt, x_sample, cache_fox_k, cache_fox_v, cache_fox_logf, state_rwkv, state_rwkv_shift,
              state_ret, page_table, ln_mix_g, w_in, fox_qn_g, fox_kn_g, fox_f_b, rwkv_mu, rwkv_w0, rwkv_w2,
              rwkv_a0, rwkv_a2, rwkv_g2, rwkv_k_k, rwkv_k_a, rwkv_r_k, rwkv_lnx_w, rwkv_lnx_b, ret_gn_w,
              w_out, ln_ffn_g, w_gate, w_up, w_down):
    B, T, _ = x_prompt.shape
    DB, TS, _ = x_sample.shape
    n_pages = page_table.shape[1]
    past_len = n_pages * PAGE_SIZE
    pos_p = jnp.arange(T)
    pos_s = past_len + jnp.arange(TS)
    dt = x_prompt.dtype
    yp, ys = x_prompt, x_sample
    kp, vp, lfp, Sp, shp, Rp = [], [], [], [], [], []
    ks_, vs_, lfs, Ss, shs, Rs = [], [], [], [], [], []
    for l in range(DEPTH):
        lp = dict(ln_mix_g=ln_mix_g[l], w_in=w_in[l], fox_qn_g=fox_qn_g[l], fox_kn_g=fox_kn_g[l],
                  fox_f_b=fox_f_b[l], rwkv_mu=rwkv_mu[l], rwkv_w0=rwkv_w0[l], rwkv_w2=rwkv_w2[l],
                  rwkv_a0=rwkv_a0[l], rwkv_a2=rwkv_a2[l], rwkv_g2=rwkv_g2[l], rwkv_k_k=rwkv_k_k[l],
                  rwkv_k_a=rwkv_k_a[l], rwkv_r_k=rwkv_r_k[l], rwkv_lnx_w=rwkv_lnx_w[l],
                  rwkv_lnx_b=rwkv_lnx_b[l], ret_gn_w=ret_gn_w[l], w_out=w_out[l], ln_ffn_g=ln_ffn_g[l],
                  w_gate=w_gate[l], w_up=w_up[l], w_down=w_down[l])
        yp, (fk, fv, flf), S_new, sh_new, R_new = layer(
            yp, pos_p, lp, None,
            jnp.zeros((B, RWKV_H, HEAD_DIM, HEAD_DIM), F32),
            jnp.zeros((B, RWKV_PROJ), dt),
            jnp.zeros((B, RET_H, HEAD_DIM, HEAD_DIM), F32))
        kp.append(fk); vp.append(fv); lfp.append(flf); Sp.append(S_new); shp.append(sh_new); Rp.append(R_new)
        k_past = cache_fox_k[l, page_table].reshape(DB, past_len, FOX_H, HEAD_DIM)
        v_past = cache_fox_v[l, page_table].reshape(DB, past_len, FOX_H, HEAD_DIM)
        lf_past = cache_fox_logf[l, page_table].reshape(DB, past_len, FOX_H)
        ys, (fk, fv, flf), S_new, sh_new, R_new = layer(
            ys, pos_s, lp, (k_past, v_past, lf_past), state_rwkv[l], state_rwkv_shift[l], state_ret[l])
        ks_.append(fk); vs_.append(fv); lfs.append(flf); Ss.append(S_new); shs.append(sh_new); Rs.append(R_new)
    n_prompt_pages = B * T // PAGE_SIZE
    fox_k_prompt = jnp.stack(kp).reshape(DEPTH, n_prompt_pages, PAGE_SIZE, FOX_H, HEAD_DIM)
    fox_v_prompt = jnp.stack(vp).reshape(DEPTH, n_prompt_pages, PAGE_SIZE, FOX_H, HEAD_DIM)
    fox_logf_prompt = jnp.stack(lfp).reshape(DEPTH, n_prompt_pages, PAGE_SIZE, FOX_H)
    rwkv_state_prompt = jnp.stack(Sp)
    rwkv_shift_prompt = jnp.stack(shp)
    ret_state_prompt = jnp.stack(Rp)
    fox_k_sample = jnp.stack(ks_)
    fox_v_sample = jnp.stack(vs_)
    fox_logf_sample = jnp.stack(lfs)
    rwkv_state_sample = jnp.stack(Ss)
    rwkv_shift_sample = jnp.stack(shs)
    ret_state_sample = jnp.stack(Rs)
    return (yp, ys, fox_k_prompt, fox_v_prompt, fox_logf_prompt, rwkv_state_prompt, rwkv_shift_prompt,
            ret_state_prompt, fox_k_sample, fox_v_sample, fox_logf_sample, rwkv_state_sample,
            rwkv_shift_sample, ret_state_sample)
```

```python
import functools

import jax
import jax.numpy as jnp
import numpy as np
from jax import lax
from jax.experimental import pallas as pl
from jax.experimental.pallas import tpu as pltpu

F32 = jnp.float32
BF16 = jnp.bfloat16

LANES = 128
SUBLANES = 8
VMEM_LIMIT = 56 * 1024 * 1024

D_MODEL = 1024
HEAD_DIM = 64
FOX_H = 8
RWKV_H = 4
RET_H = 4
FOX_W = FOX_H * HEAD_DIM
RWKV_W = RWKV_H * HEAD_DIM
RET_W = RET_H * HEAD_DIM
RWKV_LORA_W = 32
RWKV_LORA_A = 32
RWKV_LORA_G = 64
RWKV_PROJ = 3 * RWKV_W + RWKV_LORA_W + RWKV_LORA_A + RWKV_LORA_G
RET_PROJ = 4 * RET_W
D_FF = 2816
PAGE_SIZE = 128
RET_CHUNK = 128
ROPE_BASE = 10000.0
NORM_EPS = 1e-6
RWKV_GN_EPS = 64e-5
RET_GN_EPS = 1e-5
NEG_BIG = -1e30
HEADS_PER_TILE = LANES // HEAD_DIM


def _cparams(sem):
    return pltpu.CompilerParams(dimension_semantics=sem, vmem_limit_bytes=VMEM_LIMIT)


def _resident(shape):
    nd = len(shape)
    return pl.BlockSpec(shape, lambda *_: (0,) * nd, pipeline_mode=pl.Buffered(1))


def _bdot(a, b):
    return jnp.dot(a.astype(BF16), b.astype(BF16), preferred_element_type=F32)


def _bdot_nt(a, b):
    return lax.dot_general(a.astype(BF16), b.astype(BF16), (((1,), (1,)), ((), ())),
                           preferred_element_type=F32)


def _split2(x):
    hi = x.astype(BF16)
    lo = (x - hi.astype(F32)).astype(BF16)
    return hi, lo


def _split3(x):
    hi = x.astype(BF16)
    r = x - hi.astype(F32)
    mid = r.astype(BF16)
    lo = (r - mid.astype(F32)).astype(BF16)
    return hi, mid, lo


def _iota(shape, axis):
    return lax.broadcasted_iota(jnp.int32, shape, axis)


def _seg_ones():
    return (_iota((LANES, LANES), 0) // HEAD_DIM == _iota((LANES, LANES), 1) // HEAD_DIM).astype(BF16)


def _seg_sum(x, e):
    outs = []
    for c in range(x.shape[-1] // LANES):
        hi, lo = _split2(x[:, c * LANES:(c + 1) * LANES])
        outs.append(jnp.dot(hi, e, preferred_element_type=F32) + jnp.dot(lo, e, preferred_element_type=F32))
    return outs[0] if len(outs) == 1 else jnp.concatenate(outs, axis=-1)


def _exact_dot(x, m01, left):
    parts = _split3(x)
    if left:
        return sum(jnp.dot(m01, p, preferred_element_type=F32) for p in parts)
    return sum(jnp.dot(p, m01, preferred_element_type=F32) for p in parts)


def _sigmoid(x):
    return 1.0 / (1.0 + jnp.exp(-x))


def _softplus(x):
    return jnp.maximum(x, 0.0) + jnp.log1p(jnp.exp(-jnp.abs(x)))


def _inproj_kernel(x_ref, g_ref, wq_ref, wk_ref, wv_ref, wl_ref, wr_ref, we_ref,
                   oq_ref, ok_ref, ov_ref, ol_ref, or_ref, oe_ref):
    x = x_ref[...]
    h = x * lax.rsqrt(jnp.mean(x * x, -1, keepdims=True) + NORM_EPS) * g_ref[...]
    hb = h.astype(BF16)
    for w_ref, o_ref in ((wq_ref, oq_ref), (wk_ref, ok_ref), (wv_ref, ov_ref), (wl_ref, ol_ref),
                         (wr_ref, or_ref), (we_ref, oe_ref)):
        o_ref[...] = jnp.dot(hb, w_ref[...], preferred_element_type=F32)


def _inproj(x, g, ws, tm):
    n = x.shape[0]
    widths = [w.shape[1] for w in ws]
    row = lambda wd: pl.BlockSpec((tm, wd), lambda i: (i, 0))
    return pl.pallas_call(
        _inproj_kernel,
        grid=(n // tm,),
        in_specs=[row(D_MODEL), _resident((1, D_MODEL))] + [_resident(w.shape) for w in ws],
        out_specs=[row(wd) for wd in widths],
        out_shape=[jax.ShapeDtypeStruct((n, wd), F32) for wd in widths],
        compiler_params=_cparams(("parallel",)),
        name="inproj",
    )(x, g, *ws)


def _foxprep_kernel(q_ref, k_ref, fl_ref, qg_ref, kg_ref, fb_ref, qn_ref, kn_ref, lf_ref, *cum,
                    tm, with_cumsum):
    e = _seg_ones()

    def hnorm(x, g):
        ms = _seg_sum(x * x, e) * (1.0 / HEAD_DIM)
        return x * lax.rsqrt(ms + NORM_EPS) * g

    qn_ref[...] = hnorm(q_ref[...], qg_ref[...]) * (HEAD_DIM ** -0.5)
    kn_ref[...] = hnorm(k_ref[...], kg_ref[...])
    z = fl_ref[...] + fb_ref[...]
    lf = jnp.minimum(z, 0.0) - jnp.log1p(jnp.exp(-jnp.abs(z)))
    lf_ref[...] = lf
    if with_cumsum:
        c_ref, ct_ref, carry_ref = cum

        @pl.when(pl.program_id(1) == 0)
        def _():
            carry_ref[...] = jnp.zeros_like(carry_ref)

        tri = (_iota((tm, tm), 0) >= _iota((tm, tm), 1)).astype(BF16)
        c = _exact_dot(lf, tri, left=True) + carry_ref[...]
        c_ref[...] = c
        carry_ref[...] = c[tm - 1:tm, :]
        ct_ref[...] = c.T[:SUBLANES, :]


def _foxprep(fq, fk, fl, qg, kg, fb, nseq, tm, with_cumsum):
    n = fq.shape[0]
    t = n // nseq
    nt = t // tm
    row = lambda wd: pl.BlockSpec((tm, wd), lambda b, j: (b * nt + j, 0))
    out_specs = [row(FOX_W), row(FOX_W), row(LANES)]
    out_shape = [jax.ShapeDtypeStruct((n, FOX_W), F32), jax.ShapeDtypeStruct((n, FOX_W), F32),
                 jax.ShapeDtypeStruct((n, LANES), F32)]
    scratch = []
    if with_cumsum:
        out_specs += [row(LANES), pl.BlockSpec((None, SUBLANES, tm), lambda b, j: (b, 0, j))]
        out_shape += [jax.ShapeDtypeStruct((n, LANES), F32), jax.ShapeDtypeStruct((nseq, SUBLANES, t), F32)]
        scratch = [pltpu.VMEM((1, LANES), F32)]
    return pl.pallas_call(
        functools.partial(_foxprep_kernel, tm=tm, with_cumsum=with_cumsum),
        grid=(nseq, nt),
        in_specs=[row(FOX_W), row(FOX_W), row(LANES), _resident((1, FOX_W)), _resident((1, FOX_W)),
                  _resident((1, LANES))],
        out_specs=out_specs,
        out_shape=out_shape,
        scratch_shapes=scratch,
        compiler_params=_cparams(("parallel", "arbitrary")),
        name="foxprep",
    )(fq, fk, fl, qg, kg, fb)


def _foxattn_kernel(q_ref, k_ref, v_ref, c_ref, ct_ref, o_ref, *, tq, tk):
    hp = pl.program_id(1)
    qi = pl.program_id(2)
    lane = _iota((1, LANES), 1)
    q = q_ref[...]
    c = c_ref[...]
    qpos = qi * tq + _iota((tq, 1), 0)
    outs = []
    for hh in range(HEADS_PER_TILE):
        head = HEADS_PER_TILE * hp + hh
        qh = jnp.where(lane // HEAD_DIM == hh, q, 0.0).astype(BF16)
        cq = jnp.sum(jnp.where(lane == head, c, 0.0), axis=-1, keepdims=True)

        def body(j, carry):
            m, l, acc = carry
            off = pl.multiple_of(j * tk, tk)
            kb = k_ref[pl.ds(off, tk), :].astype(BF16)
            vb = v_ref[pl.ds(off, tk), :].astype(BF16)
            ck = ct_ref[pl.ds(head, 1), pl.ds(off, tk)]
            s = lax.dot_general(qh, kb, (((1,), (1,)), ((), ())), preferred_element_type=F32) + (cq - ck)
            kpos = off + _iota((1, tk), 1)
            s = jnp.where(kpos <= qpos, s, NEG_BIG)
            m_new = jnp.maximum(m, jnp.max(s, axis=-1, keepdims=True))
            alpha = jnp.exp(m - m_new)
            p = jnp.exp(s - m_new)
            l = alpha * l + jnp.sum(p, axis=-1, keepdims=True)
            acc = alpha * acc + jnp.dot(p.astype(BF16), vb, preferred_element_type=F32)
            return m_new, l, acc

        init = (jnp.full((tq, 1), NEG_BIG, F32), jnp.zeros((tq, 1), F32), jnp.zeros((tq, LANES), F32))
        m, l, acc = lax.fori_loop(0, qi + 1, body, init)
        outs.append(acc / l)
    o_ref[...] = jnp.where(lane // HEAD_DIM == 0, outs[0], outs[1])


def _foxattn(qn, kn, fv, c, ct, nseq, tq):
    n = qn.shape[0]
    t = n // nseq
    nq = t // tq
    npair = FOX_H // HEADS_PER_TILE
    return pl.pallas_call(
        functools.partial(_foxattn_kernel, tq=tq, tk=tq),
        grid=(nseq, npair, nq),
        in_specs=[
            pl.BlockSpec((tq, LANES), lambda b, h, i: (b * nq + i, h)),
            pl.BlockSpec((t, LANES), lambda b, h, i: (b, h)),
            pl.BlockSpec((t, LANES), lambda b, h, i: (b, h)),
            pl.BlockSpec((tq, LANES), lambda b, h, i: (b * nq + i, 0)),
            pl.BlockSpec((None, SUBLANES, t), lambda b, h, i: (b, 0, 0)),
        ],
        out_specs=pl.BlockSpec((tq, LANES), lambda b, h, i: (b * nq + i, h)),
        out_shape=jax.ShapeDtypeStruct((n, FOX_W), F32),
        compiler_params=_cparams(("parallel", "parallel", "arbitrary")),
        name="foxattn",
    )(qn, kn, fv, c, ct)


def _foxsample_kernel(pt_ref, q_ref, kn_ref, vn_ref, lfn_ref, *rest, n_pages, t_new):
    del pt_ref
    kp = rest[:n_pages]
    vp = rest[n_pages:2 * n_pages]
    lp = rest[2 * n_pages:3 * n_pages]
    o_ref = rest[3 * n_pages]
    nrow = t_new * FOX_H

    rep = lambda x: jnp.concatenate([x] * t_new, axis=0)
    q = q_ref[...]
    hmask = _iota((FOX_H, FOX_W), 1) // HEAD_DIM == _iota((FOX_H, FOX_W), 0)
    qbd = jnp.concatenate(
        [jnp.where(hmask, jnp.broadcast_to(q[t:t + 1, :], (FOX_H, FOX_W)), 0.0) for t in range(t_new)],
        axis=0).astype(BF16)

    upper = (_iota((PAGE_SIZE, PAGE_SIZE), 0) <= _iota((PAGE_SIZE, PAGE_SIZE), 1)).astype(BF16)
    carry = jnp.zeros((FOX_H, 1), F32)
    cps = []
    for i in range(n_pages):
        ci = _exact_dot(lp[i][...], upper, left=False) + carry
        carry = ci[:, PAGE_SIZE - 1:PAGE_SIZE]
        cps.append(ci)

    lfn = lfn_ref[...]
    diag8 = _iota((FOX_H, LANES), 1) == _iota((FOX_H, LANES), 0)
    cn_cols = []
    run = jnp.zeros((1, LANES), F32)
    for t in range(t_new):
        run = run + lfn[t:t + 1, :]
        cn_cols.append(jnp.sum(jnp.where(diag8, jnp.broadcast_to(run, (FOX_H, LANES)), 0.0), axis=-1, keepdims=True))
    cn = jnp.concatenate(cn_cols, axis=0)
    cq_abs = rep(carry) + cn

    s_past = [_bdot_nt(qbd, kp[i][...]) + (cq_abs - rep(cps[i])) for i in range(n_pages)]

    zeros_tail = jnp.zeros((PAGE_SIZE - SUBLANES, FOX_W), F32)
    s_new = _bdot_nt(qbd, jnp.concatenate([kn_ref[...], zeros_tail], axis=0))
    key = _iota((nrow, PAGE_SIZE), 1)
    trow = _iota((nrow, PAGE_SIZE), 0) // FOX_H
    ckey = jnp.zeros((nrow, PAGE_SIZE), F32)
    for j in range(t_new):
        ckey = jnp.where(key == j, rep(cn_cols[j]), ckey)
    s_new = jnp.where(key <= trow, s_new + (cn - ckey), NEG_BIG)

    m = jnp.max(s_new, axis=-1, keepdims=True)
    for s in s_past:
        m = jnp.maximum(m, jnp.max(s, axis=-1, keepdims=True))
    p_new = jnp.exp(s_new - m)
    l = jnp.sum(p_new, axis=-1, keepdims=True)
    o = _bdot(p_new, jnp.concatenate([vn_ref[...], zeros_tail], axis=0))
    for i in range(n_pages):
        p = jnp.exp(s_past[i] - m)
        l = l + jnp.sum(p, axis=-1, keepdims=True)
        o = o + _bdot(p, vp[i][...])
    o = o / l
    omask = _iota((nrow, FOX_W), 1) // HEAD_DIM == _iota((nrow, FOX_W), 0) % FOX_H
    o = jnp.where(omask, o, 0.0).reshape(t_new, FOX_H, FOX_W)
    o_ref[...] = jnp.sum(o, axis=1)


def _foxsample(page_table, layer, qn, kn, vn, lfn, cache_k, cache_v, cache_lft, t_new):
    db, n_pages = page_table.shape
    pt = page_table.reshape(-1)
    new = lambda wd: pl.BlockSpec((None, SUBLANES, wd), lambda b, pt: (b, 0, 0))

    def page(shape, i):
        return pl.BlockSpec((None, None) + shape, lambda b, pt: (layer, pt[b * n_pages + i], 0, 0))

    in_specs = [new(FOX_W), new(FOX_W), new(FOX_W), new(LANES)]
    in_specs += [page((PAGE_SIZE, FOX_W), i) for i in range(n_pages)]
    in_specs += [page((PAGE_SIZE, FOX_W), i) for i in range(n_pages)]
    in_specs += [page((FOX_H, PAGE_SIZE), i) for i in range(n_pages)]
    return pl.pallas_call(
        functools.partial(_foxsample_kernel, n_pages=n_pages, t_new=t_new),
        grid_spec=pltpu.PrefetchScalarGridSpec(
            num_scalar_prefetch=1,
            grid=(db,),
            in_specs=in_specs,
            out_specs=pl.BlockSpec((None, t_new, FOX_W), lambda b, pt: (b, 0, 0)),
        ),
        out_shape=jax.ShapeDtypeStruct((db, t_new, FOX_W), F32),
        compiler_params=_cparams(("arbitrary",)),
        name="foxsample",
    )(pt, qn, kn, vn, lfn, *([cache_k] * n_pages), *([cache_v] * n_pages), *([cache_lft] * n_pages))


def _rwkvprep_kernel(p_ref, pp_ref, mu_ref, w0_ref, w2_ref, a0_ref, a2_ref, g2_ref, kk_ref, ka_ref,
                     r_out, w_out, k_out, v_out, a_out, b_out, g_out):
    e = _seg_ones()
    p = p_ref[...]
    xs = p + (pp_ref[...] - p) * mu_ref[...]
    r = xs[:, 0:RWKV_W]
    k = xs[:, RWKV_W:2 * RWKV_W]
    v = xs[:, 2 * RWKV_W:3 * RWKV_W]
    lo = xs[:, 3 * RWKV_W:]
    w = -_softplus(-(w0_ref[...] + _bdot(jnp.tanh(lo), w2_ref[...]))) - 0.5
    a = _sigmoid(a0_ref[...] + _bdot(lo, a2_ref[...]))
    kk = k * kk_ref[...]
    nrm = jnp.sqrt(_seg_sum(kk * kk, e))
    kk = kk / jnp.maximum(nrm, 1e-12)
    r_out[...] = r
    w_out[...] = jnp.exp(-jnp.exp(w))
    k_out[...] = k * (1.0 + (a - 1.0) * ka_ref[...])
    v_out[...] = v
    a_out[...] = -kk
    b_out[...] = kk * a
    g_out[...] = _bdot(_sigmoid(lo), g2_ref[...])


def _rwkvprep(p, pp, mu, w0, w2p, a0, a2p, g2p, k_k, k_a, tm):
    n = p.shape[0]
    row = lambda wd: pl.BlockSpec((tm, wd), lambda i: (i, 0))
    vec = _resident((1, RWKV_W))
    lora = _resident((LANES, RWKV_W))
    return pl.pallas_call(
        _rwkvprep_kernel,
        grid=(n // tm,),
        in_specs=[row(RWKV_PROJ), row(RWKV_PROJ), _resident((1, RWKV_PROJ)), vec, lora, vec, lora, lora, vec, vec],
        out_specs=[row(RWKV_W)] * 7,
        out_shape=[jax.ShapeDtypeStruct((n, RWKV_W), F32)] * 7,
        compiler_params=_cparams(("parallel",)),
        name="rwkvprep",
    )(p, pp, mu, w0, w2p, a0, a2p, g2p, k_k, k_a)


def _rwkvscan_kernel(r_ref, w_ref, k_ref, v_ref, a_ref, b_ref, s0_ref, y_ref, st_ref, s_scr, *, bblk, tblk):
    tb = pl.program_id(1)
    ng = HEADS_PER_TILE * bblk

    @pl.when(tb == 0)
    def _():
        s_scr[...] = s0_ref[...]

    e = _seg_ones()
    diag = (_iota((HEAD_DIM, LANES), 1) % HEAD_DIM == _iota((HEAD_DIM, LANES), 0))[None]

    sub = min(SUBLANES, tblk)

    def seg(x):
        return _seg_sum(x.reshape(ng * HEAD_DIM, LANES), e).reshape(ng, HEAD_DIM, LANES)

    def chunk(c, carry):
        off = pl.multiple_of(c * sub, sub)
        tiles = [[ref[b, pl.ds(off, sub), :] for b in range(bblk)] for ref in (r_ref, w_ref, k_ref, v_ref, a_ref, b_ref)]

        def rows(kind, i):
            return jnp.concatenate(
                [tiles[kind][b][i:i + 1, hp * LANES:(hp + 1) * LANES][None] for b in range(bblk) for hp in range(2)],
                axis=0)

        s = s_scr[...]
        yrows = []
        for i in range(sub):
            sa = seg(s * rows(4, i))
            vcol = seg(jnp.where(diag, rows(3, i), 0.0))
            s = s * rows(1, i) + sa * rows(5, i) + vcol * rows(2, i)
            ycol = seg(s * rows(0, i))
            yrows.append(jnp.sum(jnp.where(diag, ycol, 0.0), axis=1))
        s_scr[...] = s
        for b in range(bblk):
            y_ref[b, pl.ds(off, sub), :] = jnp.concatenate(
                [jnp.concatenate([yr[2 * b + hp:2 * b + hp + 1, :] for yr in yrows], axis=0) for hp in range(2)],
                axis=1)
        return carry

    lax.fori_loop(0, tblk // sub, chunk, 0)

    @pl.when(tb == pl.num_programs(1) - 1)
    def _():
        st_ref[...] = s_scr[...]


def _rwkvscan(r, w, k, v, a, b, s0, bblk, tblk):
    nb, t, _ = r.shape
    ng = HEADS_PER_TILE * bblk
    seq = pl.BlockSpec((bblk, tblk, RWKV_W), lambda i, j: (i, j, 0))
    st = pl.BlockSpec((ng, HEAD_DIM, LANES), lambda i, j: (i, 0, 0))
    return pl.pallas_call(
        functools.partial(_rwkvscan_kernel, bblk=bblk, tblk=tblk),
        grid=(nb // bblk, t // tblk),
        in_specs=[seq] * 6 + [st],
        out_specs=[seq, st],
        out_shape=[jax.ShapeDtypeStruct((nb, t, RWKV_W), F32), jax.ShapeDtypeStruct(s0.shape, F32)],
        scratch_shapes=[pltpu.VMEM((ng, HEAD_DIM, LANES), F32)],
        compiler_params=_cparams(("parallel", "arbitrary")),
        name="rwkvscan",
    )(r, w, k, v, a, b, s0)


def _rope_kernel(inv_ref, cos_ref, sin_ref, *, pos0):
    t = cos_ref.shape[0]
    pos = (pos0 + _iota((t, LANES), 0)).astype(F32)
    ang = pos * inv_ref[...]
    first_half = _iota((t, LANES), 1) % HEAD_DIM < HEAD_DIM // 2
    cos_ref[...] = jnp.cos(ang)
    sin_ref[...] = jnp.where(first_half, -jnp.sin(ang), jnp.sin(ang))


def _rope_tables(inv_lanes, t, pos0):
    return pl.pallas_call(
        functools.partial(_rope_kernel, pos0=pos0),
        out_shape=[jax.ShapeDtypeStruct((t, LANES), F32)] * 2,
        name="rope",
    )(inv_lanes)


def _ret_kernel(q_ref, k_ref, v_ref, g_ref, cos_ref, sin_ref, lg_ref, lgh_ref, gn_ref, r0_ref,
                o_ref, rt_ref, r_scr, *, l_in, l_true):
    lp = RET_CHUNK
    c = pl.program_id(2)

    @pl.when(c == 0)
    def _():
        r_scr[...] = r0_ref[...]

    def pad(x):
        if l_in == lp:
            return x
        return jnp.concatenate([x, jnp.zeros((lp - l_in, x.shape[1]), F32)], axis=0)

    lane = _iota((lp, LANES), 1)
    first_half = lane % HEAD_DIM < HEAD_DIM // 2
    cos = pad(cos_ref[...])
    sin = pad(sin_ref[...])

    def rope(x):
        swapped = jnp.where(first_half, pltpu.roll(x, LANES - HEAD_DIM // 2, 1), pltpu.roll(x, HEAD_DIM // 2, 1))
        return x * cos + swapped * sin

    q = rope(pad(q_ref[...]))
    k = rope(pad(k_ref[...])) * (HEAD_DIM ** -0.5)
    v = pad(v_ref[...])
    lg = lg_ref[...]
    idx = _iota((lp, 1), 0).astype(F32)
    diff = (_iota((lp, lp), 0) - _iota((lp, lp), 1)).astype(F32)
    vb = v.astype(BF16)
    kb = k.astype(BF16)
    outs = []
    for hh in range(HEADS_PER_TILE):
        dmask = jnp.where(diff >= 0.0, jnp.exp(lgh_ref[hh:hh + 1, :] * jnp.maximum(diff, 0.0)), 0.0)
        qh = jnp.where(lane // HEAD_DIM == hh, q, 0.0).astype(BF16)
        s = lax.dot_general(qh, kb, (((1,), (1,)), ((), ())), preferred_element_type=F32) * dmask
        outs.append(jnp.dot(s.astype(BF16), vb, preferred_element_type=F32))
    o = jnp.where(lane // HEAD_DIM == 0, outs[0], outs[1])
    r = r_scr[...]
    o = o + _bdot(q, r) * jnp.exp(lg * (idx + 1.0))
    kw = k * jnp.exp(lg * jnp.maximum(l_true - 1.0 - idx, 0.0))
    same_head = _iota((LANES, LANES), 0) // HEAD_DIM == _iota((LANES, LANES), 1) // HEAD_DIM
    upd = jnp.dot(kw.T.astype(BF16), vb, preferred_element_type=F32)
    r_new = r * jnp.exp(lg * float(l_true)) + jnp.where(same_head, upd, 0.0)
    r_scr[...] = r_new

    @pl.when(c == pl.num_programs(2) - 1)
    def _():
        rt_ref[...] = r_new

    e = _seg_ones()
    mean = _seg_sum(o, e) * (1.0 / HEAD_DIM)
    xc = o - mean
    var = _seg_sum(xc * xc, e) * (1.0 / HEAD_DIM)
    g = pad(g_ref[...])
    y = xc * lax.rsqrt(var + RET_GN_EPS) * gn_ref[...] * (g * _sigmoid(g))
    o_ref[...] = y[:l_in, :]


def _retention(proj, cos, sin, lg, lgh, gn, r0, l_in, l_true):
    nb, t, _ = proj.shape
    nc = t // l_in
    npair = RET_H // HEADS_PER_TILE
    col = lambda base: pl.BlockSpec((None, l_in, LANES), lambda b, h, c: (b, c, base + h))
    tab = pl.BlockSpec((l_in, LANES), lambda b, h, c: (c, 0))
    st = pl.BlockSpec((None, LANES, LANES), lambda b, h, c: (b * npair + h, 0, 0))
    return pl.pallas_call(
        functools.partial(_ret_kernel, l_in=l_in, l_true=l_true),
        grid=(nb, npair, nc),
        in_specs=[col(0), col(npair), col(2 * npair), col(3 * npair), tab, tab,
                  pl.BlockSpec((None, 1, LANES), lambda b, h, c: (h, 0, 0)),
                  pl.BlockSpec((None, HEADS_PER_TILE, LANES), lambda b, h, c: (h, 0, 0)),
                  pl.BlockSpec((1, LANES), lambda b, h, c: (0, h)),
                  st],
        out_specs=[pl.BlockSpec((None, l_in, LANES), lambda b, h, c: (b, c, h)), st],
        out_shape=[jax.ShapeDtypeStruct((nb, t, RET_W), F32), jax.ShapeDtypeStruct(r0.shape, F32)],
        scratch_shapes=[pltpu.VMEM((LANES, LANES), F32)],
        compiler_params=_cparams(("parallel", "parallel", "arbitrary")),
        name="retention",
    )(proj, proj, proj, proj, cos, sin, lg, lgh, gn, r0)


def _outproj_kernel(x_ref, fo_ref, y_ref, r_ref, k_ref, v_ref, g_ref, eo_ref, lw_ref, lb_ref, rk_ref,
                    wf_ref, wr_ref, we_ref, o_ref):
    e = _seg_ones()
    y = y_ref[...]
    mean = _seg_sum(y, e) * (1.0 / HEAD_DIM)
    yc = y - mean
    var = _seg_sum(yc * yc, e) * (1.0 / HEAD_DIM)
    yn = yc * lax.rsqrt(var + RWKV_GN_EPS) * lw_ref[...] + lb_ref[...]
    v = v_ref[...]
    bonus = _seg_sum(r_ref[...] * k_ref[...] * rk_ref[...], e) * v
    ro = (yn + bonus) * g_ref[...]
    o_ref[...] = (x_ref[...] + _bdot(fo_ref[...], wf_ref[...]) + _bdot(ro, wr_ref[...])
                  + _bdot(eo_ref[...], we_ref[...]))


def _outproj(x, fo, y, r, k, v, g, eo, lw, lb, rk, wf, wr, we, tm):
    n = x.shape[0]
    row = lambda wd: pl.BlockSpec((tm, wd), lambda i: (i, 0))
    vec = _resident((1, RWKV_W))
    return pl.pallas_call(
        _outproj_kernel,
        grid=(n // tm,),
        in_specs=[row(D_MODEL), row(FOX_W)] + [row(RWKV_W)] * 6 + [vec, vec, vec,
                  _resident(wf.shape), _resident(wr.shape), _resident(we.shape)],
        out_specs=row(D_MODEL),
        out_shape=jax.ShapeDtypeStruct((n, D_MODEL), F32),
        compiler_params=_cparams(("parallel",)),
        name="outproj",
    )(x, fo, y, r, k, v, g, eo, lw, lb, rk, wf, wr, we)


def _ffn_kernel(x_ref, g_ref, wg_ref, wu_ref, wd_ref, o_ref):
    x = x_ref[...]
    h = (x * lax.rsqrt(jnp.mean(x * x, -1, keepdims=True) + NORM_EPS) * g_ref[...]).astype(BF16)
    gate = jnp.dot(h, wg_ref[...], preferred_element_type=F32)
    up = jnp.dot(h, wu_ref[...], preferred_element_type=F32)
    act = gate * _sigmoid(gate) * up
    o_ref[...] = x + _bdot(act, wd_ref[...])


def _ffn(x, g, wg, wu, wd, tm):
    n = x.shape[0]
    row = pl.BlockSpec((tm, D_MODEL), lambda i: (i, 0))
    return pl.pallas_call(
        _ffn_kernel,
        grid=(n // tm,),
        in_specs=[row, _resident((1, D_MODEL)), _resident(wg.shape), _resident(wu.shape), _resident(wd.shape)],
        out_specs=row,
        out_shape=jax.ShapeDtypeStruct((n, D_MODEL), F32),
        compiler_params=_cparams(("parallel",)),
        name="ffn",
    )(x, g, wg, wu, wd)


def _rwkv_state_in(s):
    nb = s.shape[0]
    s = s.reshape(nb, RWKV_H // 2, 2, HEAD_DIM, HEAD_DIM).transpose(0, 1, 3, 2, 4)
    return s.reshape(nb * (RWKV_H // 2), HEAD_DIM, LANES)


def _rwkv_state_out(s, nb):
    s = s.reshape(nb, RWKV_H // 2, HEAD_DIM, 2, HEAD_DIM).transpose(0, 1, 3, 2, 4)
    return s.reshape(nb, RWKV_H, HEAD_DIM, HEAD_DIM)


def _ret_state_in(r):
    nb = r.shape[0]
    r = r.reshape(nb, RET_H // 2, 2, HEAD_DIM, HEAD_DIM)
    z = jnp.zeros_like(r[:, :, 0])
    top = jnp.concatenate([r[:, :, 0], z], axis=-1)
    bot = jnp.concatenate([z, r[:, :, 1]], axis=-1)
    return jnp.concatenate([top, bot], axis=-2).reshape(nb * (RET_H // 2), LANES, LANES)


def _ret_state_out(r, nb):
    r = r.reshape(nb, RET_H // 2, LANES, LANES)
    return jnp.stack([r[:, :, :HEAD_DIM, :HEAD_DIM], r[:, :, HEAD_DIM:, HEAD_DIM:]], axis=2).reshape(
        nb, RET_H, HEAD_DIM, HEAD_DIM)


def _layer_weights(l, ln_mix_g, w_in, fox_qn_g, fox_kn_g, fox_f_b, rwkv_mu, rwkv_w0, rwkv_w2, rwkv_a0, rwkv_a2,
                   rwkv_g2, rwkv_k_k, rwkv_k_a, rwkv_r_k, rwkv_lnx_w, rwkv_lnx_b, ret_gn_w, w_out, ln_ffn_g,
                   w_gate, w_up, w_down):
    wi = w_in[l].astype(BF16)
    o_f = 3 * FOX_W
    o_r = o_f + FOX_H
    o_e = o_r + RWKV_PROJ
    pad_rows = lambda w, off: jnp.zeros((LANES, RWKV_W), BF16).at[off:off + w.shape[0]].set(w.astype(BF16))
    wo = w_out[l].astype(BF16)
    return dict(
        ln_mix_g=ln_mix_g[l][None],
        w_in=[wi[:, 0:FOX_W], wi[:, FOX_W:2 * FOX_W], wi[:, 2 * FOX_W:o_f],
              jnp.pad(wi[:, o_f:o_r], ((0, 0), (0, LANES - FOX_H))), wi[:, o_r:o_e], wi[:, o_e:]],
        qg=jnp.tile(fox_qn_g[l], FOX_H)[None], kg=jnp.tile(fox_kn_g[l], FOX_H)[None],
        fb=jnp.pad(fox_f_b[l], (0, LANES - FOX_H))[None],
        mu=rwkv_mu[l][None], w0=rwkv_w0[l][None], a0=rwkv_a0[l][None],
        w2=pad_rows(rwkv_w2[l], 0), a2=pad_rows(rwkv_a2[l], RWKV_LORA_W),
        g2=pad_rows(rwkv_g2[l], RWKV_LORA_W + RWKV_LORA_A),
        k_k=rwkv_k_k[l][None], k_a=rwkv_k_a[l][None], r_k=rwkv_r_k[l].reshape(1, RWKV_W),
        lnx_w=rwkv_lnx_w[l][None], lnx_b=rwkv_lnx_b[l][None], gn=ret_gn_w[l][None],
        wo_f=wo[:FOX_W], wo_r=wo[FOX_W:FOX_W + RWKV_W], wo_e=wo[FOX_W + RWKV_W:],
        ln_ffn_g=ln_ffn_g[l][None],
        w_gate=w_gate[l].astype(BF16), w_up=w_up[l].astype(BF16), w_down=w_down[l].astype(BF16),
    )


def _mix_and_ffn(x2, fo, p_rwkv, p_prev, s0, proj_ret, ret_tabs, r0, lw, nb, t, tm, bblk, tblk, l_in, l_true):
    r, w, k, v, a, b, g = _rwkvprep(p_rwkv, p_prev, lw['mu'], lw['w0'], lw['w2'], lw['a0'], lw['a2'], lw['g2'],
                                    lw['k_k'], lw['k_a'], tm)
    seq = lambda z: z.reshape(nb, t, RWKV_W)
    y, s_t = _rwkvscan(seq(r), seq(w), seq(k), seq(v), seq(a), seq(b), s0, bblk, tblk)
    cos, sin, lg, lgh = ret_tabs
    eo, r_t = _retention(proj_ret, cos, sin, lg, lgh, lw['gn'], r0, l_in, l_true)
    eo = eo[:, :t].reshape(nb * t, RET_W)
    x2 = _outproj(x2, fo, y.reshape(nb * t, RWKV_W), r, k, v, g, eo, lw['lnx_w'], lw['lnx_b'], lw['r_k'],
                  lw['wo_f'], lw['wo_r'], lw['wo_e'], tm)
    x2 = _ffn(x2, lw['ln_ffn_g'], lw['w_gate'], lw['w_up'], lw['w_down'], tm)
    return x2, s_t, r_t


def kernel(x_prompt, x_sample, cache_fox_k, cache_fox_v, cache_fox_logf, state_rwkv, state_rwkv_shift, state_ret,
           page_table, ln_mix_g, w_in, fox_qn_g, fox_kn_g, fox_f_b, rwkv_mu, rwkv_w0, rwkv_w2, rwkv_a0, rwkv_a2,
           rwkv_g2, rwkv_k_k, rwkv_k_a, rwkv_r_k, rwkv_lnx_w, rwkv_lnx_b, ret_gn_w, w_out, ln_ffn_g, w_gate,
           w_up, w_down):
    nb, t, _ = x_prompt.shape
    db, ts, _ = x_sample.shape
    depth = w_in.shape[0]
    n_pages = page_table.shape[1]
    past_len = n_pages * PAGE_SIZE
    n_pool = cache_fox_k.shape[1]

    half = HEAD_DIM // 2
    inv = ROPE_BASE ** (-jnp.arange(half, dtype=F32) / half)
    inv_lanes = jnp.tile(inv, LANES // half)[None]
    log_gamma = jnp.log1p(-jnp.exp2(-5.0 - jnp.arange(RET_H, dtype=F32)))
    lg = jnp.repeat(log_gamma, HEAD_DIM).reshape(RET_H // 2, 1, LANES)
    lgh = jnp.broadcast_to(log_gamma.reshape(RET_H // 2, 2, 1), (RET_H // 2, 2, LANES))
    cos_p, sin_p = _rope_tables(inv_lanes, t, 0)
    cos_s, sin_s = _rope_tables(inv_lanes, SUBLANES, past_len)

    ck = cache_fox_k.reshape(depth, n_pool, PAGE_SIZE, FOX_W)
    cv = cache_fox_v.reshape(depth, n_pool, PAGE_SIZE, FOX_W)
    clt = jnp.swapaxes(cache_fox_logf, 2, 3)

    pad8 = lambda z: jnp.pad(z.reshape(db, ts, -1), ((0, 0), (0, SUBLANES - ts), (0, 0)))

    yp = x_prompt.reshape(nb * t, D_MODEL)
    ys = x_sample.reshape(db * ts, D_MODEL)
    outs = [[] for _ in range(12)]
    for l in range(depth):
        lw = _layer_weights(l, ln_mix_g, w_in, fox_qn_g, fox_kn_g, fox_f_b, rwkv_mu, rwkv_w0, rwkv_w2, rwkv_a0,
                            rwkv_a2, rwkv_g2, rwkv_k_k, rwkv_k_a, rwkv_r_k, rwkv_lnx_w, rwkv_lnx_b, ret_gn_w,
                            w_out, ln_ffn_g, w_gate, w_up, w_down)
        fq, fk, fv, fl, p_rwkv, p_ret = _inproj(yp, lw['ln_mix_g'], lw['w_in'], 256)
        qn, kn, lf, c, ct = _foxprep(fq, fk, fl, lw['qg'], lw['kg'], lw['fb'], nb, 256, True)
        fo = _foxattn(qn, kn, fv, c, ct, nb, 256)
        p3 = p_rwkv.reshape(nb, t, RWKV_PROJ)
        p_prev = jnp.concatenate([jnp.zeros((nb, 1, RWKV_PROJ), F32), p3[:, :-1]], axis=1).reshape(nb * t, RWKV_PROJ)
        yp, s_t, r_t = _mix_and_ffn(
            yp, fo, p_rwkv, p_prev, jnp.zeros((2 * nb, HEAD_DIM, LANES), F32), p_ret.reshape(nb, t, RET_PROJ),
            (cos_p, sin_p, lg, lgh), jnp.zeros((2 * nb, LANES, LANES), F32), lw, nb, t, 256, nb, 128,
            RET_CHUNK, RET_CHUNK)
        outs[0].append(kn); outs[1].append(fv); outs[2].append(lf[:, :FOX_H])
        outs[3].append(_rwkv_state_out(s_t, nb)); outs[4].append(p3[:, -1]); outs[5].append(_ret_state_out(r_t, nb))
        fq, fk, fv, fl, p_rwkv, p_ret = _inproj(ys, lw['ln_mix_g'], lw['w_in'], 256)
        qn, kn, lf = _foxprep(fq, fk, fl, lw['qg'], lw['kg'], lw['fb'], 1, 256, False)
        fo = _foxsample(page_table, l, pad8(qn), pad8(kn), pad8(fv), pad8(lf), ck, cv, clt, ts)
        p3 = p_rwkv.reshape(db, ts, RWKV_PROJ)
        p_prev = jnp.concatenate([state_rwkv_shift[l][:, None, :], p3[:, :-1]], axis=1).reshape(db * ts, RWKV_PROJ)
        ys, s_t, r_t = _mix_and_ffn(
            ys, fo.reshape(db * ts, FOX_W), p_rwkv, p_prev, _rwkv_state_in(state_rwkv[l]), pad8(p_ret),
            (cos_s, sin_s, lg, lgh), _ret_state_in(state_ret[l]), lw, db, ts, 256, 8, ts, SUBLANES, ts)
        outs[6].append(kn); outs[7].append(fv); outs[8].append(lf[:, :FOX_H])
        outs[9].append(_rwkv_state_out(s_t, db)); outs[10].append(p3[:, -1]); outs[11].append(_ret_state_out(r_t, db))

    n_pp = nb * t // PAGE_SIZE
    st = lambda i: jnp.stack(outs[i])
    return (yp.reshape(nb, t, D_MODEL), ys.reshape(db, ts, D_MODEL),
            st(0).reshape(depth, n_pp, PAGE_SIZE, FOX_H, HEAD_DIM),
            st(1).reshape(depth, n_pp, PAGE_SIZE, FOX_H, HEAD_DIM),
            st(2).reshape(depth, n_pp, PAGE_SIZE, FOX_H),
            st(3), st(4), st(5),
            st(6).reshape(depth, db, ts, FOX_H, HEAD_DIM),
            st(7).reshape(depth, db, ts, FOX_H, HEAD_DIM),
            st(8).reshape(depth, db, ts, FOX_H),
            st(9), st(10), st(11))
```

```python
import functools

import jax
import jax.numpy as jnp
import numpy as np
from jax import lax
from jax.experimental import pallas as pl
from jax.experimental.pallas import tpu as pltpu

F32 = jnp.float32
BF16 = jnp.bfloat16

LANES = 128
SUBLANES = 8
VMEM_LIMIT = 56 * 1024 * 1024

D_MODEL = 1024
HEAD_DIM = 64
FOX_H = 8
RWKV_H = 4
RET_H = 4
FOX_W = FOX_H * HEAD_DIM
RWKV_W = RWKV_H * HEAD_DIM
RET_W = RET_H * HEAD_DIM
RWKV_LORA_W = 32
RWKV_LORA_A = 32
RWKV_LORA_G = 64
RWKV_PROJ = 3 * RWKV_W + RWKV_LORA_W + RWKV_LORA_A + RWKV_LORA_G
RET_PROJ = 4 * RET_W
D_FF = 2816
PAGE_SIZE = 128
RET_CHUNK = 128
ROPE_BASE = 10000.0
NORM_EPS = 1e-6
RWKV_GN_EPS = 64e-5
RET_GN_EPS = 1e-5
NEG_BIG = -1e30
HEADS_PER_TILE = LANES // HEAD_DIM


def _cparams(sem):
    return pltpu.CompilerParams(dimension_semantics=sem, vmem_limit_bytes=VMEM_LIMIT)


def _resident(shape):
    nd = len(shape)
    return pl.BlockSpec(shape, lambda *_: (0,) * nd, pipeline_mode=pl.Buffered(1))


def _bdot(a, b):
    return jnp.dot(a.astype(BF16), b.astype(BF16), preferred_element_type=F32)


def _bdot_nt(a, b):
    return lax.dot_general(a.astype(BF16), b.astype(BF16), (((1,), (1,)), ((), ())),
                           preferred_element_type=F32)


def _split2(x):
    hi = x.astype(BF16)
    lo = (x - hi.astype(F32)).astype(BF16)
    return hi, lo


def _split3(x):
    hi = x.astype(BF16)
    r = x - hi.astype(F32)
    mid = r.astype(BF16)
    lo = (r - mid.astype(F32)).astype(BF16)
    return hi, mid, lo


def _iota(shape, axis):
    return lax.broadcasted_iota(jnp.int32, shape, axis)


def _seg_ones():
    return (_iota((LANES, LANES), 0) // HEAD_DIM == _iota((LANES, LANES), 1) // HEAD_DIM).astype(BF16)


def _seg_sum(x, e):
    outs = []
    for c in range(x.shape[-1] // LANES):
        hi, lo = _split2(x[:, c * LANES:(c + 1) * LANES])
        outs.append(jnp.dot(hi, e, preferred_element_type=F32) + jnp.dot(lo, e, preferred_element_type=F32))
    return outs[0] if len(outs) == 1 else jnp.concatenate(outs, axis=-1)


def _exact_dot(x, m01, left):
    parts = _split3(x)
    if left:
        return sum(jnp.dot(m01, p, preferred_element_type=F32) for p in parts)
    return sum(jnp.dot(p, m01, preferred_element_type=F32) for p in parts)


def _sigmoid(x):
    return 1.0 / (1.0 + jnp.exp(-x))


def _softplus(x):
    return jnp.maximum(x, 0.0) + jnp.log1p(jnp.exp(-jnp.abs(x)))


def _inproj_kernel(x_ref, g_ref, wq_ref, wk_ref, wv_ref, wl_ref, wr_ref, we_ref,
                   oq_ref, ok_ref, ov_ref, ol_ref, or_ref, oe_ref):
    x = x_ref[...]
    h = x * lax.rsqrt(jnp.mean(x * x, -1, keepdims=True) + NORM_EPS) * g_ref[...]
    hb = h.astype(BF16)
    for w_ref, o_ref in ((wq_ref, oq_ref), (wk_ref, ok_ref), (wv_ref, ov_ref), (wl_ref, ol_ref),
                         (wr_ref, or_ref), (we_ref, oe_ref)):
        o_ref[...] = jnp.dot(hb, w_ref[...], preferred_element_type=F32)


def _inproj(x, g, ws, tm):
    n = x.shape[0]
    widths = [w.shape[1] for w in ws]
    row = lambda wd: pl.BlockSpec((tm, wd), lambda i: (i, 0))
    return pl.pallas_call(
        _inproj_kernel,
        grid=(n // tm,),
        in_specs=[row(D_MODEL), _resident((1, D_MODEL))] + [_resident(w.shape) for w in ws],
        out_specs=[row(wd) for wd in widths],
        out_shape=[jax.ShapeDtypeStruct((n, wd), F32) for wd in widths],
        compiler_params=_cparams(("parallel",)),
        name="inproj",
    )(x, g, *ws)


def _fox_norms(q_ref, k_ref, fl_ref, qg_ref, kg_ref, fb_ref):
    e = _seg_ones()

    def hnorm(x, g):
        ms = _seg_sum(x * x, e) * (1.0 / HEAD_DIM)
        return x * lax.rsqrt(ms + NORM_EPS) * g

    qn = hnorm(q_ref[...], qg_ref[...]) * (HEAD_DIM ** -0.5)
    kn = hnorm(k_ref[...], kg_ref[...])
    z = fl_ref[...] + fb_ref[...]
    lf = jnp.minimum(z, 0.0) - jnp.log1p(jnp.exp(-jnp.abs(z)))
    return qn, kn, lf


def _foxprep_sample_kernel(q_ref, k_ref, fl_ref, qg_ref, kg_ref, fb_ref, qn_ref, kn_ref, lf_ref):
    qn_ref[...], kn_ref[...], lf_ref[...] = _fox_norms(q_ref, k_ref, fl_ref, qg_ref, kg_ref, fb_ref)


def _foxprep_prompt_kernel(q_ref, k_ref, fl_ref, qg_ref, kg_ref, fb_ref, v_ref,
                           kn_ref, lf_ref, qx_ref, kx_ref, vb_ref, carry_ref, *, tm):
    qn, kn, lf = _fox_norms(q_ref, k_ref, fl_ref, qg_ref, kg_ref, fb_ref)
    kn_ref[...] = kn
    lf_ref[...] = lf
    vb_ref[...] = v_ref[...].astype(BF16)

    @pl.when(pl.program_id(1) == 0)
    def _():
        carry_ref[...] = jnp.zeros_like(carry_ref)

    tri = (_iota((tm, tm), 0) >= _iota((tm, tm), 1)).astype(BF16)
    c = _exact_dot(lf, tri, left=True) + carry_ref[...]
    carry_ref[...] = c[tm - 1:tm, :]

    lane = _iota((tm, LANES), 1)
    j = lane % HEAD_DIM
    for h in range(FOX_H):
        hp, hh = divmod(h, HEADS_PER_TILE)
        c_hi, c_mid, c_lo = (p.astype(F32) for p in _split3(jnp.broadcast_to(c[:, h:h + 1], (tm, LANES))))
        ext_q = jnp.where(j == 0, c_hi, jnp.where(j == 1, c_mid, jnp.where(j == 2, c_lo, jnp.where(j < 6, 1.0, 0.0))))
        ext_k = jnp.where(j < 3, 1.0, jnp.where(j == 3, -c_hi, jnp.where(j == 4, -c_mid, jnp.where(j == 5, -c_lo, 0.0))))
        own = lane // HEAD_DIM == hh
        pair = slice(hp * LANES, (hp + 1) * LANES)
        tile = slice(h * LANES, (h + 1) * LANES)
        qx_ref[:, tile] = jnp.where(own, qn[:, pair], ext_q).astype(BF16)
        kx_ref[:, tile] = jnp.where(own, kn[:, pair], ext_k).astype(BF16)


def _foxprep_sample(fq, fk, fl, qg, kg, fb, tm):
    n = fq.shape[0]
    row = lambda wd: pl.BlockSpec((tm, wd), lambda i: (i, 0))
    return pl.pallas_call(
        _foxprep_sample_kernel,
        grid=(n // tm,),
        in_specs=[row(FOX_W), row(FOX_W), row(LANES), _resident((1, FOX_W)), _resident((1, FOX_W)),
                  _resident((1, LANES))],
        out_specs=[row(FOX_W), row(FOX_W), row(LANES)],
        out_shape=[jax.ShapeDtypeStruct((n, FOX_W), F32), jax.ShapeDtypeStruct((n, FOX_W), F32),
                   jax.ShapeDtypeStruct((n, LANES), F32)],
        compiler_params=_cparams(("parallel",)),
        name="foxprep_sample",
    )(fq, fk, fl, qg, kg, fb)


def _foxprep_prompt(fq, fk, fl, fv, qg, kg, fb, nseq, tm):
    n = fq.shape[0]
    nt = n // nseq // tm
    row = lambda wd: pl.BlockSpec((tm, wd), lambda b, j: (b * nt + j, 0))
    return pl.pallas_call(
        functools.partial(_foxprep_prompt_kernel, tm=tm),
        grid=(nseq, nt),
        in_specs=[row(FOX_W), row(FOX_W), row(LANES), _resident((1, FOX_W)), _resident((1, FOX_W)),
                  _resident((1, LANES)), row(FOX_W)],
        out_specs=[row(FOX_W), row(LANES), row(FOX_H * LANES), row(FOX_H * LANES), row(FOX_W)],
        out_shape=[jax.ShapeDtypeStruct((n, FOX_W), F32), jax.ShapeDtypeStruct((n, LANES), F32),
                   jax.ShapeDtypeStruct((n, FOX_H * LANES), BF16), jax.ShapeDtypeStruct((n, FOX_H * LANES), BF16),
                   jax.ShapeDtypeStruct((n, FOX_W), BF16)],
        scratch_shapes=[pltpu.VMEM((1, LANES), F32)],
        compiler_params=_cparams(("parallel", "arbitrary")),
        name="foxprep_prompt",
    )(fq, fk, fl, qg, kg, fb, fv)


def _foxattn_kernel(qx_ref, kx_ref, vb_ref, o_ref, *, tq, tk):
    qi = pl.program_id(2)
    q = [qx_ref[:, hh * LANES:(hh + 1) * LANES] for hh in range(HEADS_PER_TILE)]
    n_full = (qi * tq) // tk

    def block(off, carry, diagonal):
        vb = vb_ref[pl.ds(off, tk), :]
        new = []
        for hh in range(HEADS_PER_TILE):
            m, l, acc = carry[hh]
            kb = kx_ref[pl.ds(off, tk), hh * LANES:(hh + 1) * LANES]
            s = lax.dot_general(q[hh], kb, (((1,), (1,)), ((), ())), preferred_element_type=F32)
            if diagonal:
                s = jnp.where(off + _iota((tq, tk), 1) <= qi * tq + _iota((tq, tk), 0), s, NEG_BIG)
            m_new = jnp.maximum(m, jnp.max(s, axis=-1, keepdims=True))
            alpha = jnp.exp(m - m_new)
            p = jnp.exp(s - m_new)
            l = alpha * l + jnp.sum(p, axis=-1, keepdims=True)
            acc = alpha * acc + jnp.dot(p.astype(BF16), vb, preferred_element_type=F32)
            new.append((m_new, l, acc))
        return tuple(new)

    init = tuple((jnp.full((tq, 1), NEG_BIG, F32), jnp.zeros((tq, 1), F32), jnp.zeros((tq, LANES), F32))
                 for _ in range(HEADS_PER_TILE))
    carry = lax.fori_loop(0, n_full, lambda j, c: block(pl.multiple_of(j * tk, tk), c, False), init)
    carry = block(pl.multiple_of(n_full * tk, tk), carry, True)
    outs = [acc / l for _, l, acc in carry]
    o_ref[...] = jnp.where(_iota((tq, LANES), 1) // HEAD_DIM == 0, outs[0], outs[1])


def _foxattn(qx, kx, vb, nseq, tq, tk):
    n = qx.shape[0]
    t = n // nseq
    nq = t // tq
    assert tk % tq == 0 and t % tk == 0
    npair = FOX_H // HEADS_PER_TILE
    pair_w = HEADS_PER_TILE * LANES
    return pl.pallas_call(
        functools.partial(_foxattn_kernel, tq=tq, tk=tk),
        grid=(nseq, npair, nq),
        in_specs=[
            pl.BlockSpec((tq, pair_w), lambda b, h, i: (b * nq + i, h)),
            pl.BlockSpec((t, pair_w), lambda b, h, i: (b, h)),
            pl.BlockSpec((t, LANES), lambda b, h, i: (b, h)),
        ],
        out_specs=pl.BlockSpec((tq, LANES), lambda b, h, i: (b * nq + i, h)),
        out_shape=jax.ShapeDtypeStruct((n, FOX_W), F32),
        compiler_params=_cparams(("parallel", "parallel", "arbitrary")),
        name="foxattn",
    )(qx, kx, vb)


def _foxsample_kernel(pt_ref, q_ref, kn_ref, vn_ref, lfn_ref, *rest, n_pages, t_new):
    del pt_ref
    kp = rest[:n_pages]
    vp = rest[n_pages:2 * n_pages]
    lp = rest[2 * n_pages:3 * n_pages]
    o_ref = rest[3 * n_pages]
    nrow = t_new * FOX_H

    ncol = PAGE_SIZE * FOX_H
    t_pad = kn_ref.shape[0]

    rep = lambda x: jnp.concatenate([x] * t_new, axis=0)
    q = q_ref[...].reshape(nrow, HEAD_DIM).astype(BF16)

    lfn = lfn_ref[...]
    diag8 = _iota((FOX_H, LANES), 1) == _iota((FOX_H, LANES), 0)
    cn_cols = []
    run = jnp.zeros((1, LANES), F32)
    for t in range(t_new):
        run = run + lfn[t:t + 1, :]
        cn_cols.append(jnp.sum(jnp.where(diag8, jnp.broadcast_to(run, (FOX_H, LANES)), 0.0), axis=-1, keepdims=True))
    cn = jnp.concatenate(cn_cols, axis=0)

    ncol_new = t_pad * FOX_H
    col = _iota((nrow, ncol_new), 1)
    row = _iota((nrow, ncol_new), 0)
    ckey = jnp.zeros((nrow, ncol_new), F32)
    for j in range(t_new):
        ckey = jnp.where(col // FOX_H == j, rep(cn_cols[j]), ckey)
    s = _bdot_nt(q, kn_ref[...].reshape(ncol_new, HEAD_DIM)) + (cn - ckey)
    s_new = jnp.where(col % FOX_H == row % FOX_H, jnp.where(col // FOX_H <= row // FOX_H, s, NEG_BIG), NEG_BIG)

    lf_all = jnp.concatenate([lp[i][...] for i in range(n_pages)], axis=0)
    reptri = (_iota((ncol, PAGE_SIZE), 1) <= _iota((ncol, PAGE_SIZE), 0) // FOX_H).astype(BF16)
    cp_all = sum(lax.dot_general(part, reptri, (((1,), (1,)), ((), ())), preferred_element_type=F32)
                 for part in _split3(lf_all))
    before = [jnp.zeros((FOX_H, 1), F32)]
    for i in range(n_pages):
        before.append(before[-1] + cp_all[i * FOX_H:(i + 1) * FOX_H, ncol - 1:ncol])
    cq_abs = rep(before[n_pages]) + cn
    same_head = _iota((nrow, ncol), 1) % FOX_H == _iota((nrow, ncol), 0) % FOX_H
    s_past = []
    for i in range(n_pages):
        cp = rep(cp_all[i * FOX_H:(i + 1) * FOX_H, :] + before[i])
        s = _bdot_nt(q, kp[i][...].reshape(ncol, HEAD_DIM)) + (cq_abs - cp)
        s_past.append(jnp.where(same_head, s, NEG_BIG))

    m = jnp.max(s_new, axis=-1, keepdims=True)
    for s in s_past:
        m = jnp.maximum(m, jnp.max(s, axis=-1, keepdims=True))
    p = jnp.exp(s_new - m)
    l = jnp.sum(p, axis=-1, keepdims=True)
    o = _bdot(p, vn_ref[...].reshape(ncol_new, HEAD_DIM))
    for i in range(n_pages):
        p = jnp.exp(s_past[i] - m)
        l = l + jnp.sum(p, axis=-1, keepdims=True)
        o = o + _bdot(p, vp[i][...].reshape(ncol, HEAD_DIM))
    o_ref[...] = (o / l).reshape(t_new, FOX_H, HEAD_DIM)


def _foxsample(page_table, layer, qn, kn, vn, lfn, cache_k, cache_v, cache_lft, t_new):
    db, n_pages = page_table.shape
    pt = page_table.reshape(-1)
    t_pad = kn.shape[1]
    new = lambda t: pl.BlockSpec((None, t, FOX_H, HEAD_DIM), lambda b, pt: (b, 0, 0, 0))

    def page(shape, i):
        nd = len(shape)
        return pl.BlockSpec((None, None) + shape, lambda b, pt: (layer, pt[b * n_pages + i]) + (0,) * nd)

    in_specs = [new(t_new), new(t_pad), new(t_pad), pl.BlockSpec((None, SUBLANES, LANES), lambda b, pt: (b, 0, 0))]
    in_specs += [page((PAGE_SIZE, FOX_H, HEAD_DIM), i) for i in range(n_pages)]
    in_specs += [page((PAGE_SIZE, FOX_H, HEAD_DIM), i) for i in range(n_pages)]
    in_specs += [page((FOX_H, PAGE_SIZE), i) for i in range(n_pages)]
    return pl.pallas_call(
        functools.partial(_foxsample_kernel, n_pages=n_pages, t_new=t_new),
        grid_spec=pltpu.PrefetchScalarGridSpec(
            num_scalar_prefetch=1,
            grid=(db,),
            in_specs=in_specs,
            out_specs=new(t_new),
        ),
        out_shape=jax.ShapeDtypeStruct((db, t_new, FOX_H, HEAD_DIM), F32),
        compiler_params=_cparams(("arbitrary",)),
        name="foxsample",
    )(pt, qn, kn, vn, lfn, *([cache_k] * n_pages), *([cache_v] * n_pages), *([cache_lft] * n_pages))


def _rwkvprep_kernel(p_ref, pp_ref, mu_ref, w0_ref, w2_ref, a0_ref, a2_ref, g2_ref, kk_ref, ka_ref,
                     r_out, w_out, k_out, v_out, a_out, b_out, g_out):
    e = _seg_ones()
    p = p_ref[...]
    xs = p + (pp_ref[...] - p) * mu_ref[...]
    r = xs[:, 0:RWKV_W]
    k = xs[:, RWKV_W:2 * RWKV_W]
    v = xs[:, 2 * RWKV_W:3 * RWKV_W]
    lo = xs[:, 3 * RWKV_W:]
    w = -_softplus(-(w0_ref[...] + _bdot(jnp.tanh(lo), w2_ref[...]))) - 0.5
    a = _sigmoid(a0_ref[...] + _bdot(lo, a2_ref[...]))
    kk = k * kk_ref[...]
    nrm = jnp.sqrt(_seg_sum(kk * kk, e))
    kk = kk / jnp.maximum(nrm, 1e-12)
    r_out[...] = r
    w_out[...] = jnp.exp(-jnp.exp(w))
    k_out[...] = k * (1.0 + (a - 1.0) * ka_ref[...])
    v_out[...] = v
    a_out[...] = -kk
    b_out[...] = kk * a
    g_out[...] = _bdot(_sigmoid(lo), g2_ref[...])


def _rwkvprep(p, pp, mu, w0, w2p, a0, a2p, g2p, k_k, k_a, tm):
    n = p.shape[0]
    row = lambda wd: pl.BlockSpec((tm, wd), lambda i: (i, 0))
    vec = _resident((1, RWKV_W))
    lora = _resident((LANES, RWKV_W))
    return pl.pallas_call(
        _rwkvprep_kernel,
        grid=(n // tm,),
        in_specs=[row(RWKV_PROJ), row(RWKV_PROJ), _resident((1, RWKV_PROJ)), vec, lora, vec, lora, lora, vec, vec],
        out_specs=[row(RWKV_W)] * 7,
        out_shape=[jax.ShapeDtypeStruct((n, RWKV_W), F32)] * 7,
        compiler_params=_cparams(("parallel",)),
        name="rwkvprep",
    )(p, pp, mu, w0, w2p, a0, a2p, g2p, k_k, k_a)


def _rwkvscan_kernel(r_ref, w_ref, k_ref, v_ref, a_ref, b_ref, s0_ref, y_ref, st_ref, s_scr, *, bblk, tblk):
    tb = pl.program_id(1)
    ng = HEADS_PER_TILE * bblk

    @pl.when(tb == 0)
    def _():
        s_scr[...] = s0_ref[...]

    e = _seg_ones()
    diag = (_iota((HEAD_DIM, LANES), 1) % HEAD_DIM == _iota((HEAD_DIM, LANES), 0))[None]

    sub = min(SUBLANES, tblk)

    def seg(x):
        return _seg_sum(x.reshape(ng * HEAD_DIM, LANES), e).reshape(ng, HEAD_DIM, LANES)

    def chunk(c, carry):
        off = pl.multiple_of(c * sub, sub)
        tiles = [[ref[b, pl.ds(off, sub), :] for b in range(bblk)] for ref in (r_ref, w_ref, k_ref, v_ref, a_ref, b_ref)]

        def rows(kind, i):
            return jnp.concatenate(
                [tiles[kind][b][i:i + 1, hp * LANES:(hp + 1) * LANES][None] for b in range(bblk) for hp in range(2)],
                axis=0)

        s = s_scr[...]
        yrows = []
        for i in range(sub):
            sa = seg(s * rows(4, i))
            vcol = seg(jnp.where(diag, rows(3, i), 0.0))
            s = s * rows(1, i) + sa * rows(5, i) + vcol * rows(2, i)
            ycol = seg(s * rows(0, i))
            yrows.append(jnp.sum(jnp.where(diag, ycol, 0.0), axis=1))
        s_scr[...] = s
        for b in range(bblk):
            y_ref[b, pl.ds(off, sub), :] = jnp.concatenate(
                [jnp.concatenate([yr[2 * b + hp:2 * b + hp + 1, :] for yr in yrows], axis=0) for hp in range(2)],
                axis=1)
        return carry

    lax.fori_loop(0, tblk // sub, chunk, 0)

    @pl.when(tb == pl.num_programs(1) - 1)
    def _():
        st_ref[...] = s_scr[...]


def _rwkvscan(r, w, k, v, a, b, s0, bblk, tblk):
    nb, t, _ = r.shape
    ng = HEADS_PER_TILE * bblk
    seq = pl.BlockSpec((bblk, tblk, RWKV_W), lambda i, j: (i, j, 0))
    st = pl.BlockSpec((ng, HEAD_DIM, LANES), lambda i, j: (i, 0, 0))
    return pl.pallas_call(
        functools.partial(_rwkvscan_kernel, bblk=bblk, tblk=tblk),
        grid=(nb // bblk, t // tblk),
        in_specs=[seq] * 6 + [st],
        out_specs=[seq, st],
        out_shape=[jax.ShapeDtypeStruct((nb, t, RWKV_W), F32), jax.ShapeDtypeStruct(s0.shape, F32)],
        scratch_shapes=[pltpu.VMEM((ng, HEAD_DIM, LANES), F32)],
        compiler_params=_cparams(("parallel", "arbitrary")),
        name="rwkvscan",
    )(r, w, k, v, a, b, s0)


def _rope_kernel(inv_ref, cos_ref, sin_ref, *, pos0):
    t = cos_ref.shape[0]
    pos = (pos0 + _iota((t, LANES), 0)).astype(F32)
    ang = pos * inv_ref[...]
    first_half = _iota((t, LANES), 1) % HEAD_DIM < HEAD_DIM // 2
    cos_ref[...] = jnp.cos(ang)
    sin_ref[...] = jnp.where(first_half, -jnp.sin(ang), jnp.sin(ang))


def _rope_tables(inv_lanes, t, pos0):
    return pl.pallas_call(
        functools.partial(_rope_kernel, pos0=pos0),
        out_shape=[jax.ShapeDtypeStruct((t, LANES), F32)] * 2,
        name="rope",
    )(inv_lanes)


def _ret_kernel(q_ref, k_ref, v_ref, g_ref, cos_ref, sin_ref, lg_ref, lgh_ref, gn_ref, r0_ref,
                o_ref, rt_ref, r_scr, *, l_in, l_true):
    lp = RET_CHUNK
    c = pl.program_id(2)

    @pl.when(c == 0)
    def _():
        r_scr[...] = r0_ref[...]

    def pad(x):
        if l_in == lp:
            return x
        return jnp.concatenate([x, jnp.zeros((lp - l_in, x.shape[1]), F32)], axis=0)

    lane = _iota((lp, LANES), 1)
    first_half = lane % HEAD_DIM < HEAD_DIM // 2
    cos = pad(cos_ref[...])
    sin = pad(sin_ref[...])

    def rope(x):
        swapped = jnp.where(first_half, pltpu.roll(x, LANES - HEAD_DIM // 2, 1), pltpu.roll(x, HEAD_DIM // 2, 1))
        return x * cos + swapped * sin

    q = rope(pad(q_ref[...]))
    k = rope(pad(k_ref[...])) * (HEAD_DIM ** -0.5)
    v = pad(v_ref[...])
    lg = lg_ref[...]
    idx = _iota((lp, 1), 0).astype(F32)
    diff = (_iota((lp, lp), 0) - _iota((lp, lp), 1)).astype(F32)
    vb = v.astype(BF16)
    kb = k.astype(BF16)
    outs = []
    for hh in range(HEADS_PER_TILE):
        dmask = jnp.where(diff >= 0.0, jnp.exp(lgh_ref[hh:hh + 1, :] * jnp.maximum(diff, 0.0)), 0.0)
        qh = jnp.where(lane // HEAD_DIM == hh, q, 0.0).astype(BF16)
        s = lax.dot_general(qh, kb, (((1,), (1,)), ((), ())), preferred_element_type=F32) * dmask
        outs.append(jnp.dot(s.astype(BF16), vb, preferred_element_type=F32))
    o = jnp.where(lane // HEAD_DIM == 0, outs[0], outs[1])
    r = r_scr[...]
    o = o + _bdot(q, r) * jnp.exp(lg * (idx + 1.0))
    kw = k * jnp.exp(lg * jnp.maximum(l_true - 1.0 - idx, 0.0))
    same_head = _iota((LANES, LANES), 0) // HEAD_DIM == _iota((LANES, LANES), 1) // HEAD_DIM
    upd = jnp.dot(kw.T.astype(BF16), vb, preferred_element_type=F32)
    r_new = r * jnp.exp(lg * float(l_true)) + jnp.where(same_head, upd, 0.0)
    r_scr[...] = r_new

    @pl.when(c == pl.num_programs(2) - 1)
    def _():
        rt_ref[...] = r_new

    e = _seg_ones()
    mean = _seg_sum(o, e) * (1.0 / HEAD_DIM)
    xc = o - mean
    var = _seg_sum(xc * xc, e) * (1.0 / HEAD_DIM)
    g = pad(g_ref[...])
    y = xc * lax.rsqrt(var + RET_GN_EPS) * gn_ref[...] * (g * _sigmoid(g))
    o_ref[...] = y[:l_in, :]


def _retention(proj, cos, sin, lg, lgh, gn, r0, l_in, l_true):
    nb, t, _ = proj.shape
    nc = t // l_in
    npair = RET_H // HEADS_PER_TILE
    col = lambda base: pl.BlockSpec((None, l_in, LANES), lambda b, h, c: (b, c, base + h))
    tab = pl.BlockSpec((l_in, LANES), lambda b, h, c: (c, 0))
    st = pl.BlockSpec((None, LANES, LANES), lambda b, h, c: (b * npair + h, 0, 0))
    return pl.pallas_call(
        functools.partial(_ret_kernel, l_in=l_in, l_true=l_true),
        grid=(nb, npair, nc),
        in_specs=[col(0), col(npair), col(2 * npair), col(3 * npair), tab, tab,
                  pl.BlockSpec((None, 1, LANES), lambda b, h, c: (h, 0, 0)),
                  pl.BlockSpec((None, HEADS_PER_TILE, LANES), lambda b, h, c: (h, 0, 0)),
                  pl.BlockSpec((1, LANES), lambda b, h, c: (0, h)),
                  st],
        out_specs=[pl.BlockSpec((None, l_in, LANES), lambda b, h, c: (b, c, h)), st],
        out_shape=[jax.ShapeDtypeStruct((nb, t, RET_W), F32), jax.ShapeDtypeStruct(r0.shape, F32)],
        scratch_shapes=[pltpu.VMEM((LANES, LANES), F32)],
        compiler_params=_cparams(("parallel", "parallel", "arbitrary")),
        name="retention",
    )(proj, proj, proj, proj, cos, sin, lg, lgh, gn, r0)


def _outproj_kernel(x_ref, fo_ref, y_ref, r_ref, k_ref, v_ref, g_ref, eo_ref, lw_ref, lb_ref, rk_ref,
                    wf_ref, wr_ref, we_ref, o_ref):
    e = _seg_ones()
    y = y_ref[...]
    mean = _seg_sum(y, e) * (1.0 / HEAD_DIM)
    yc = y - mean
    var = _seg_sum(yc * yc, e) * (1.0 / HEAD_DIM)
    yn = yc * lax.rsqrt(var + RWKV_GN_EPS) * lw_ref[...] + lb_ref[...]
    v = v_ref[...]
    bonus = _seg_sum(r_ref[...] * k_ref[...] * rk_ref[...], e) * v
    ro = (yn + bonus) * g_ref[...]
    o_ref[...] = (x_ref[...] + _bdot(fo_ref[...], wf_ref[...]) + _bdot(ro, wr_ref[...])
                  + _bdot(eo_ref[...], we_ref[...]))


def _outproj(x, fo, y, r, k, v, g, eo, lw, lb, rk, wf, wr, we, tm):
    n = x.shape[0]
    row = lambda wd: pl.BlockSpec((tm, wd), lambda i: (i, 0))
    vec = _resident((1, RWKV_W))
    return pl.pallas_call(
        _outproj_kernel,
        grid=(n // tm,),
        in_specs=[row(D_MODEL), row(FOX_W)] + [row(RWKV_W)] * 6 + [vec, vec, vec,
                  _resident(wf.shape), _resident(wr.shape), _resident(we.shape)],
        out_specs=row(D_MODEL),
        out_shape=jax.ShapeDtypeStruct((n, D_MODEL), F32),
        compiler_params=_cparams(("parallel",)),
        name="outproj",
    )(x, fo, y, r, k, v, g, eo, lw, lb, rk, wf, wr, we)


def _ffn_kernel(x_ref, g_ref, wg_ref, wu_ref, wd_ref, o_ref):
    x = x_ref[...]
    h = (x * lax.rsqrt(jnp.mean(x * x, -1, keepdims=True) + NORM_EPS) * g_ref[...]).astype(BF16)
    gate = jnp.dot(h, wg_ref[...], preferred_element_type=F32)
    up = jnp.dot(h, wu_ref[...], preferred_element_type=F32)
    act = gate * _sigmoid(gate) * up
    o_ref[...] = x + _bdot(act, wd_ref[...])


def _ffn(x, g, wg, wu, wd, tm):
    n = x.shape[0]
    row = pl.BlockSpec((tm, D_MODEL), lambda i: (i, 0))
    return pl.pallas_call(
        _ffn_kernel,
        grid=(n // tm,),
        in_specs=[row, _resident((1, D_MODEL)), _resident(wg.shape), _resident(wu.shape), _resident(wd.shape)],
        out_specs=row,
        out_shape=jax.ShapeDtypeStruct((n, D_MODEL), F32),
        compiler_params=_cparams(("parallel",)),
        name="ffn",
    )(x, g, wg, wu, wd)


def _rwkv_state_in(s):
    nb = s.shape[0]
    s = s.reshape(nb, RWKV_H // 2, 2, HEAD_DIM, HEAD_DIM).transpose(0, 1, 3, 2, 4)
    return s.reshape(nb * (RWKV_H // 2), HEAD_DIM, LANES)


def _rwkv_state_out(s, nb):
    s = s.reshape(nb, RWKV_H // 2, HEAD_DIM, 2, HEAD_DIM).transpose(0, 1, 3, 2, 4)
    return s.reshape(nb, RWKV_H, HEAD_DIM, HEAD_DIM)


def _ret_state_in(r):
    nb = r.shape[0]
    r = r.reshape(nb, RET_H // 2, 2, HEAD_DIM, HEAD_DIM)
    z = jnp.zeros_like(r[:, :, 0])
    top = jnp.concatenate([r[:, :, 0], z], axis=-1)
    bot = jnp.concatenate([z, r[:, :, 1]], axis=-1)
    return jnp.concatenate([top, bot], axis=-2).reshape(nb * (RET_H // 2), LANES, LANES)


def _ret_state_out(r, nb):
    r = r.reshape(nb, RET_H // 2, LANES, LANES)
    return jnp.stack([r[:, :, :HEAD_DIM, :HEAD_DIM], r[:, :, HEAD_DIM:, HEAD_DIM:]], axis=2).reshape(
        nb, RET_H, HEAD_DIM, HEAD_DIM)


def _layer_weights(l, ln_mix_g, w_in, fox_qn_g, fox_kn_g, fox_f_b, rwkv_mu, rwkv_w0, rwkv_w2, rwkv_a0, rwkv_a2,
                   rwkv_g2, rwkv_k_k, rwkv_k_a, rwkv_r_k, rwkv_lnx_w, rwkv_lnx_b, ret_gn_w, w_out, ln_ffn_g,
                   w_gate, w_up, w_down):
    wi = w_in[l].astype(BF16)
    o_f = 3 * FOX_W
    o_r = o_f + FOX_H
    o_e = o_r + RWKV_PROJ
    pad_rows = lambda w, off: jnp.zeros((LANES, RWKV_W), BF16).at[off:off + w.shape[0]].set(w.astype(BF16))
    wo = w_out[l].astype(BF16)
    return dict(
        ln_mix_g=ln_mix_g[l][None],
        w_in=[wi[:, 0:FOX_W], wi[:, FOX_W:2 * FOX_W], wi[:, 2 * FOX_W:o_f],
              jnp.pad(wi[:, o_f:o_r], ((0, 0), (0, LANES - FOX_H))), wi[:, o_r:o_e], wi[:, o_e:]],
        qg=jnp.tile(fox_qn_g[l], FOX_H)[None], kg=jnp.tile(fox_kn_g[l], FOX_H)[None],
        fb=jnp.pad(fox_f_b[l], (0, LANES - FOX_H))[None],
        mu=rwkv_mu[l][None], w0=rwkv_w0[l][None], a0=rwkv_a0[l][None],
        w2=pad_rows(rwkv_w2[l], 0), a2=pad_rows(rwkv_a2[l], RWKV_LORA_W),
        g2=pad_rows(rwkv_g2[l], RWKV_LORA_W + RWKV_LORA_A),
        k_k=rwkv_k_k[l][None], k_a=rwkv_k_a[l][None], r_k=rwkv_r_k[l].reshape(1, RWKV_W),
        lnx_w=rwkv_lnx_w[l][None], lnx_b=rwkv_lnx_b[l][None], gn=ret_gn_w[l][None],
        wo_f=wo[:FOX_W], wo_r=wo[FOX_W:FOX_W + RWKV_W], wo_e=wo[FOX_W + RWKV_W:],
        ln_ffn_g=ln_ffn_g[l][None],
        w_gate=w_gate[l].astype(BF16), w_up=w_up[l].astype(BF16), w_down=w_down[l].astype(BF16),
    )


def _mix_and_ffn(x2, fo, p_rwkv, p_prev, s0, proj_ret, ret_tabs, r0, lw, nb, t, tm, bblk, tblk, l_in, l_true):
    r, w, k, v, a, b, g = _rwkvprep(p_rwkv, p_prev, lw['mu'], lw['w0'], lw['w2'], lw['a0'], lw['a2'], lw['g2'],
                                    lw['k_k'], lw['k_a'], tm)
    seq = lambda z: z.reshape(nb, t, RWKV_W)
    y, s_t = _rwkvscan(seq(r), seq(w), seq(k), seq(v), seq(a), seq(b), s0, bblk, tblk)
    cos, sin, lg, lgh = ret_tabs
    eo, r_t = _retention(proj_ret, cos, sin, lg, lgh, lw['gn'], r0, l_in, l_true)
    eo = eo[:, :t].reshape(nb * t, RET_W)
    x2 = _outproj(x2, fo, y.reshape(nb * t, RWKV_W), r, k, v, g, eo, lw['lnx_w'], lw['lnx_b'], lw['r_k'],
                  lw['wo_f'], lw['wo_r'], lw['wo_e'], tm)
    x2 = _ffn(x2, lw['ln_ffn_g'], lw['w_gate'], lw['w_up'], lw['w_down'], tm)
    return x2, s_t, r_t


def kernel(x_prompt, x_sample, cache_fox_k, cache_fox_v, cache_fox_logf, state_rwkv, state_rwkv_shift, state_ret,
           page_table, ln_mix_g, w_in, fox_qn_g, fox_kn_g, fox_f_b, rwkv_mu, rwkv_w0, rwkv_w2, rwkv_a0, rwkv_a2,
           rwkv_g2, rwkv_k_k, rwkv_k_a, rwkv_r_k, rwkv_lnx_w, rwkv_lnx_b, ret_gn_w, w_out, ln_ffn_g, w_gate,
           w_up, w_down):
    nb, t, _ = x_prompt.shape
    db, ts, _ = x_sample.shape
    depth = w_in.shape[0]
    n_pages = page_table.shape[1]
    past_len = n_pages * PAGE_SIZE
    n_pool = cache_fox_k.shape[1]

    half = HEAD_DIM // 2
    inv = ROPE_BASE ** (-jnp.arange(half, dtype=F32) / half)
    inv_lanes = jnp.tile(inv, LANES // half)[None]
    log_gamma = jnp.log1p(-jnp.exp2(-5.0 - jnp.arange(RET_H, dtype=F32)))
    lg = jnp.repeat(log_gamma, HEAD_DIM).reshape(RET_H // 2, 1, LANES)
    lgh = jnp.broadcast_to(log_gamma.reshape(RET_H // 2, 2, 1), (RET_H // 2, 2, LANES))
    cos_p, sin_p = _rope_tables(inv_lanes, t, 0)
    cos_s, sin_s = _rope_tables(inv_lanes, SUBLANES, past_len)

    clt = jnp.swapaxes(cache_fox_logf, 2, 3)

    pad8 = lambda z: jnp.pad(z.reshape(db, ts, -1), ((0, 0), (0, SUBLANES - ts), (0, 0)))
    heads = lambda z: z.reshape(db, ts, FOX_H, HEAD_DIM)
    t_pad = PAGE_SIZE // FOX_H
    pad_rows = lambda z: jnp.pad(heads(z), ((0, 0), (0, t_pad - ts), (0, 0), (0, 0)))

    yp = x_prompt.reshape(nb * t, D_MODEL)
    ys = x_sample.reshape(db * ts, D_MODEL)
    outs = [[] for _ in range(12)]
    for l in range(depth):
        lw = _layer_weights(l, ln_mix_g, w_in, fox_qn_g, fox_kn_g, fox_f_b, rwkv_mu, rwkv_w0, rwkv_w2, rwkv_a0,
                            rwkv_a2, rwkv_g2, rwkv_k_k, rwkv_k_a, rwkv_r_k, rwkv_lnx_w, rwkv_lnx_b, ret_gn_w,
                            w_out, ln_ffn_g, w_gate, w_up, w_down)
        fq, fk, fv, fl, p_rwkv, p_ret = _inproj(yp, lw['ln_mix_g'], lw['w_in'], 256)
        kn, lf, qx, kx, vb = _foxprep_prompt(fq, fk, fl, fv, lw['qg'], lw['kg'], lw['fb'], nb, 256)
        fo = _foxattn(qx, kx, vb, nb, 512, 512)
        p3 = p_rwkv.reshape(nb, t, RWKV_PROJ)
        p_prev = jnp.concatenate([jnp.zeros((nb, 1, RWKV_PROJ), F32), p3[:, :-1]], axis=1).reshape(nb * t, RWKV_PROJ)
        yp, s_t, r_t = _mix_and_ffn(
            yp, fo, p_rwkv, p_prev, jnp.zeros((2 * nb, HEAD_DIM, LANES), F32), p_ret.reshape(nb, t, RET_PROJ),
            (cos_p, sin_p, lg, lgh), jnp.zeros((2 * nb, LANES, LANES), F32), lw, nb, t, 256, nb, 128,
            RET_CHUNK, RET_CHUNK)
        outs[0].append(kn); outs[1].append(fv); outs[2].append(lf[:, :FOX_H])
        outs[3].append(_rwkv_state_out(s_t, nb)); outs[4].append(p3[:, -1]); outs[5].append(_ret_state_out(r_t, nb))
        fq, fk, fv, fl, p_rwkv, p_ret = _inproj(ys, lw['ln_mix_g'], lw['w_in'], 256)
        qn, kn, lf = _foxprep_sample(fq, fk, fl, lw['qg'], lw['kg'], lw['fb'], 256)
        fo = _foxsample(page_table, l, heads(qn), pad_rows(kn), pad_rows(fv), pad8(lf), cache_fox_k, cache_fox_v,
                        clt, ts)
        p3 = p_rwkv.reshape(db, ts, RWKV_PROJ)
        p_prev = jnp.concatenate([state_rwkv_shift[l][:, None, :], p3[:, :-1]], axis=1).reshape(db * ts, RWKV_PROJ)
        ys, s_t, r_t = _mix_and_ffn(
            ys, fo.reshape(db * ts, FOX_W), p_rwkv, p_prev, _rwkv_state_in(state_rwkv[l]), pad8(p_ret),
            (cos_s, sin_s, lg, lgh), _ret_state_in(state_ret[l]), lw, db, ts, 256, 8, ts, SUBLANES, ts)
        outs[6].append(kn); outs[7].append(fv); outs[8].append(lf[:, :FOX_H])
        outs[9].append(_rwkv_state_out(s_t, db)); outs[10].append(p3[:, -1]); outs[11].append(_ret_state_out(r_t, db))

    n_pp = nb * t // PAGE_SIZE
    st = lambda i: jnp.stack(outs[i])
    return (yp.reshape(nb, t, D_MODEL), ys.reshape(db, ts, D_MODEL),
            st(0).reshape(depth, n_pp, PAGE_SIZE, FOX_H, HEAD_DIM),
            st(1).reshape(depth, n_pp, PAGE_SIZE, FOX_H, HEAD_DIM),
            st(2).reshape(depth, n_pp, PAGE_SIZE, FOX_H),
            st(3), st(4), st(5),
            st(6).reshape(depth, db, ts, FOX_H, HEAD_DIM),
            st(7).reshape(depth, db, ts, FOX_H, HEAD_DIM),
            st(8).reshape(depth, db, ts, FOX_H),
            st(9), st(10), st(11))
```

```python
import functools

import jax
import jax.numpy as jnp
import numpy as np
from jax import lax
from jax.experimental import pallas as pl
from jax.experimental.pallas import tpu as pltpu

F32 = jnp.float32
BF16 = jnp.bfloat16

LANES = 128
SUBLANES = 8
VMEM_LIMIT = 56 * 1024 * 1024

D_MODEL = 1024
HEAD_DIM = 64
FOX_H = 8
RWKV_H = 4
RET_H = 4
FOX_W = FOX_H * HEAD_DIM
RWKV_W = RWKV_H * HEAD_DIM
RET_W = RET_H * HEAD_DIM
RWKV_LORA_W = 32
RWKV_LORA_A = 32
RWKV_LORA_G = 64
RWKV_PROJ = 3 * RWKV_W + RWKV_LORA_W + RWKV_LORA_A + RWKV_LORA_G
RET_PROJ = 4 * RET_W
D_FF = 2816
PAGE_SIZE = 128
RET_CHUNK = 128
ROPE_BASE = 10000.0
NORM_EPS = 1e-6
RWKV_GN_EPS = 64e-5
RET_GN_EPS = 1e-5
NEG_BIG = -1e30
HEADS_PER_TILE = LANES // HEAD_DIM


def _cparams(sem):
    return pltpu.CompilerParams(dimension_semantics=sem, vmem_limit_bytes=VMEM_LIMIT)


def _resident(shape):
    nd = len(shape)
    return pl.BlockSpec(shape, lambda *_: (0,) * nd, pipeline_mode=pl.Buffered(1))


def _bdot(a, b):
    return jnp.dot(a.astype(BF16), b.astype(BF16), preferred_element_type=F32)


def _bdot_nt(a, b):
    return lax.dot_general(a.astype(BF16), b.astype(BF16), (((1,), (1,)), ((), ())),
                           preferred_element_type=F32)


def _split2(x):
    hi = x.astype(BF16)
    lo = (x - hi.astype(F32)).astype(BF16)
    return hi, lo


def _split3(x):
    hi = x.astype(BF16)
    r = x - hi.astype(F32)
    mid = r.astype(BF16)
    lo = (r - mid.astype(F32)).astype(BF16)
    return hi, mid, lo


def _iota(shape, axis):
    return lax.broadcasted_iota(jnp.int32, shape, axis)


def _seg_ones():
    return (_iota((LANES, LANES), 0) // HEAD_DIM == _iota((LANES, LANES), 1) // HEAD_DIM).astype(BF16)


def _seg_sum(x, e):
    outs = []
    for c in range(x.shape[-1] // LANES):
        hi, lo = _split2(x[:, c * LANES:(c + 1) * LANES])
        outs.append(jnp.dot(hi, e, preferred_element_type=F32) + jnp.dot(lo, e, preferred_element_type=F32))
    return outs[0] if len(outs) == 1 else jnp.concatenate(outs, axis=-1)


def _exact_dot(x, m01, left):
    parts = _split3(x)
    if left:
        return sum(jnp.dot(m01, p, preferred_element_type=F32) for p in parts)
    return sum(jnp.dot(p, m01, preferred_element_type=F32) for p in parts)


def _sigmoid(x):
    return 1.0 / (1.0 + jnp.exp(-x))


def _softplus(x):
    return jnp.maximum(x, 0.0) + jnp.log1p(jnp.exp(-jnp.abs(x)))


def _inproj_kernel(x_ref, g_ref, wq_ref, wk_ref, wv_ref, wl_ref, wr_ref, we_ref,
                   oq_ref, ok_ref, ov_ref, ol_ref, or_ref, oe_ref):
    x = x_ref[...]
    h = x * lax.rsqrt(jnp.mean(x * x, -1, keepdims=True) + NORM_EPS) * g_ref[...]
    hb = h.astype(BF16)
    for w_ref, o_ref in ((wq_ref, oq_ref), (wk_ref, ok_ref), (wv_ref, ov_ref), (wl_ref, ol_ref),
                         (wr_ref, or_ref), (we_ref, oe_ref)):
        o_ref[...] = jnp.dot(hb, w_ref[...], preferred_element_type=F32)


def _inproj(x, g, ws, tm):
    n = x.shape[0]
    widths = [w.shape[1] for w in ws]
    row = lambda wd: pl.BlockSpec((tm, wd), lambda i: (i, 0))
    return pl.pallas_call(
        _inproj_kernel,
        grid=(n // tm,),
        in_specs=[row(D_MODEL), _resident((1, D_MODEL))] + [_resident(w.shape) for w in ws],
        out_specs=[row(wd) for wd in widths],
        out_shape=[jax.ShapeDtypeStruct((n, wd), F32) for wd in widths],
        compiler_params=_cparams(("parallel",)),
        name="inproj",
    )(x, g, *ws)


def _fox_norms(q_ref, k_ref, fl_ref, qg_ref, kg_ref, fb_ref):
    e = _seg_ones()

    def hnorm(x, g):
        ms = _seg_sum(x * x, e) * (1.0 / HEAD_DIM)
        return x * lax.rsqrt(ms + NORM_EPS) * g

    qn = hnorm(q_ref[...], qg_ref[...]) * (HEAD_DIM ** -0.5)
    kn = hnorm(k_ref[...], kg_ref[...])
    z = fl_ref[...] + fb_ref[...]
    lf = jnp.minimum(z, 0.0) - jnp.log1p(jnp.exp(-jnp.abs(z)))
    return qn, kn, lf


def _foxprep_sample_kernel(q_ref, k_ref, fl_ref, qg_ref, kg_ref, fb_ref, qn_ref, kn_ref, lf_ref):
    qn_ref[...], kn_ref[...], lf_ref[...] = _fox_norms(q_ref, k_ref, fl_ref, qg_ref, kg_ref, fb_ref)


def _foxprep_prompt_kernel(q_ref, k_ref, fl_ref, qg_ref, kg_ref, fb_ref, v_ref,
                           kn_ref, lf_ref, qx_ref, kx_ref, vb_ref, carry_ref, *, tm):
    qn, kn, lf = _fox_norms(q_ref, k_ref, fl_ref, qg_ref, kg_ref, fb_ref)
    kn_ref[...] = kn
    lf_ref[...] = lf
    vb_ref[...] = v_ref[...].astype(BF16)

    @pl.when(pl.program_id(1) == 0)
    def _():
        carry_ref[...] = jnp.zeros_like(carry_ref)

    tri = (_iota((tm, tm), 0) >= _iota((tm, tm), 1)).astype(BF16)
    c = _exact_dot(lf, tri, left=True) + carry_ref[...]
    carry_ref[...] = c[tm - 1:tm, :]

    lane = _iota((tm, LANES), 1)
    j = lane % HEAD_DIM
    for h in range(FOX_H):
        hp, hh = divmod(h, HEADS_PER_TILE)
        c_hi, c_mid, c_lo = (p.astype(F32) for p in _split3(jnp.broadcast_to(c[:, h:h + 1], (tm, LANES))))
        ext_q = jnp.where(j == 0, c_hi, jnp.where(j == 1, c_mid, jnp.where(j == 2, c_lo, jnp.where(j < 6, 1.0, 0.0))))
        ext_k = jnp.where(j < 3, 1.0, jnp.where(j == 3, -c_hi, jnp.where(j == 4, -c_mid, jnp.where(j == 5, -c_lo, 0.0))))
        own = lane // HEAD_DIM == hh
        pair = slice(hp * LANES, (hp + 1) * LANES)
        tile = slice(h * LANES, (h + 1) * LANES)
        qx_ref[:, tile] = jnp.where(own, qn[:, pair], ext_q).astype(BF16)
        kx_ref[:, tile] = jnp.where(own, kn[:, pair], ext_k).astype(BF16)


def _foxprep_sample(fq, fk, fl, qg, kg, fb, tm):
    n = fq.shape[0]
    row = lambda wd: pl.BlockSpec((tm, wd), lambda i: (i, 0))
    return pl.pallas_call(
        _foxprep_sample_kernel,
        grid=(n // tm,),
        in_specs=[row(FOX_W), row(FOX_W), row(LANES), _resident((1, FOX_W)), _resident((1, FOX_W)),
                  _resident((1, LANES))],
        out_specs=[row(FOX_W), row(FOX_W), row(LANES)],
        out_shape=[jax.ShapeDtypeStruct((n, FOX_W), F32), jax.ShapeDtypeStruct((n, FOX_W), F32),
                   jax.ShapeDtypeStruct((n, LANES), F32)],
        compiler_params=_cparams(("parallel",)),
        name="foxprep_sample",
    )(fq, fk, fl, qg, kg, fb)


def _foxprep_prompt(fq, fk, fl, fv, qg, kg, fb, nseq, tm):
    n = fq.shape[0]
    nt = n // nseq // tm
    row = lambda wd: pl.BlockSpec((tm, wd), lambda b, j: (b * nt + j, 0))
    return pl.pallas_call(
        functools.partial(_foxprep_prompt_kernel, tm=tm),
        grid=(nseq, nt),
        in_specs=[row(FOX_W), row(FOX_W), row(LANES), _resident((1, FOX_W)), _resident((1, FOX_W)),
                  _resident((1, LANES)), row(FOX_W)],
        out_specs=[row(FOX_W), row(LANES), row(FOX_H * LANES), row(FOX_H * LANES), row(FOX_W)],
        out_shape=[jax.ShapeDtypeStruct((n, FOX_W), F32), jax.ShapeDtypeStruct((n, LANES), F32),
                   jax.ShapeDtypeStruct((n, FOX_H * LANES), BF16), jax.ShapeDtypeStruct((n, FOX_H * LANES), BF16),
                   jax.ShapeDtypeStruct((n, FOX_W), BF16)],
        scratch_shapes=[pltpu.VMEM((1, LANES), F32)],
        compiler_params=_cparams(("parallel", "arbitrary")),
        name="foxprep_prompt",
    )(fq, fk, fl, qg, kg, fb, fv)


def _foxattn_kernel(qx_ref, kx_ref, vb_ref, o_ref, *, tq, tk):
    qi = pl.program_id(2)
    q = [qx_ref[:, hh * LANES:(hh + 1) * LANES] for hh in range(HEADS_PER_TILE)]
    n_full = (qi * tq) // tk

    def block(off, carry, diagonal):
        vb = vb_ref[pl.ds(off, tk), :]
        new = []
        for hh in range(HEADS_PER_TILE):
            m, l, acc = carry[hh]
            kb = kx_ref[pl.ds(off, tk), hh * LANES:(hh + 1) * LANES]
            s = lax.dot_general(q[hh], kb, (((1,), (1,)), ((), ())), preferred_element_type=F32)
            if diagonal:
                s = jnp.where(off + _iota((tq, tk), 1) <= qi * tq + _iota((tq, tk), 0), s, NEG_BIG)
            m_new = jnp.maximum(m, jnp.max(s, axis=-1, keepdims=True))
            alpha = jnp.exp(m - m_new)
            p = jnp.exp(s - m_new)
            l = alpha * l + jnp.sum(p, axis=-1, keepdims=True)
            acc = alpha * acc + jnp.dot(p.astype(BF16), vb, preferred_element_type=F32)
            new.append((m_new, l, acc))
        return tuple(new)

    init = tuple((jnp.full((tq, 1), NEG_BIG, F32), jnp.zeros((tq, 1), F32), jnp.zeros((tq, LANES), F32))
                 for _ in range(HEADS_PER_TILE))
    carry = lax.fori_loop(0, n_full, lambda j, c: block(pl.multiple_of(j * tk, tk), c, False), init)
    carry = block(pl.multiple_of(n_full * tk, tk), carry, True)
    outs = [acc / l for _, l, acc in carry]
    o_ref[...] = jnp.where(_iota((tq, LANES), 1) // HEAD_DIM == 0, outs[0], outs[1])


def _foxattn(qx, kx, vb, nseq, tq, tk):
    n = qx.shape[0]
    t = n // nseq
    nq = t // tq
    assert tk % tq == 0 and t % tk == 0
    npair = FOX_H // HEADS_PER_TILE
    pair_w = HEADS_PER_TILE * LANES
    return pl.pallas_call(
        functools.partial(_foxattn_kernel, tq=tq, tk=tk),
        grid=(nseq, npair, nq),
        in_specs=[
            pl.BlockSpec((tq, pair_w), lambda b, h, i: (b * nq + i, h)),
            pl.BlockSpec((t, pair_w), lambda b, h, i: (b, h)),
            pl.BlockSpec((t, LANES), lambda b, h, i: (b, h)),
        ],
        out_specs=pl.BlockSpec((tq, LANES), lambda b, h, i: (b * nq + i, h)),
        out_shape=jax.ShapeDtypeStruct((n, FOX_W), F32),
        compiler_params=_cparams(("parallel", "parallel", "arbitrary")),
        name="foxattn",
    )(qx, kx, vb)


def _foxsample_kernel(pt_ref, q_ref, kn_ref, vn_ref, lfn_ref, *rest, n_pages, t_new):
    del pt_ref
    kp = rest[:n_pages]
    vp = rest[n_pages:2 * n_pages]
    lp = rest[2 * n_pages:3 * n_pages]
    o_ref = rest[3 * n_pages]
    nrow = t_new * FOX_H

    rep = lambda x: jnp.concatenate([x] * t_new, axis=0)
    q = q_ref[...]
    hmask = _iota((FOX_H, FOX_W), 1) // HEAD_DIM == _iota((FOX_H, FOX_W), 0)
    qbd = jnp.concatenate(
        [jnp.where(hmask, jnp.broadcast_to(q[t:t + 1, :], (FOX_H, FOX_W)), 0.0) for t in range(t_new)],
        axis=0).astype(BF16)

    lfn = lfn_ref[...]
    diag8 = _iota((FOX_H, LANES), 1) == _iota((FOX_H, LANES), 0)
    cn_cols = []
    run = jnp.zeros((1, LANES), F32)
    for t in range(t_new):
        run = run + lfn[t:t + 1, :]
        cn_cols.append(jnp.sum(jnp.where(diag8, jnp.broadcast_to(run, (FOX_H, LANES)), 0.0), axis=-1, keepdims=True))
    cn = jnp.concatenate(cn_cols, axis=0)

    zeros_tail = jnp.zeros((PAGE_SIZE - SUBLANES, FOX_W), F32)
    key = _iota((nrow, PAGE_SIZE), 1)
    trow = _iota((nrow, PAGE_SIZE), 0) // FOX_H
    ckey = jnp.zeros((nrow, PAGE_SIZE), F32)
    for j in range(t_new):
        ckey = jnp.where(key == j, rep(cn_cols[j]), ckey)
    s_new = _bdot_nt(qbd, jnp.concatenate([kn_ref[...], zeros_tail], axis=0)) + (cn - ckey)
    s_new = jnp.where(key <= trow, s_new, NEG_BIG)

    lf_all = jnp.concatenate([lp[i][...] for i in range(n_pages)], axis=0)
    upper = (_iota((PAGE_SIZE, PAGE_SIZE), 0) <= _iota((PAGE_SIZE, PAGE_SIZE), 1)).astype(BF16)
    cp_all = _exact_dot(lf_all, upper, left=False)
    before = [jnp.zeros((FOX_H, 1), F32)]
    for i in range(n_pages):
        before.append(before[-1] + cp_all[i * FOX_H:(i + 1) * FOX_H, PAGE_SIZE - 1:PAGE_SIZE])
    cq_abs = rep(before[n_pages]) + cn
    s_past = []
    for i in range(n_pages):
        cp = rep(cp_all[i * FOX_H:(i + 1) * FOX_H, :] + before[i])
        s_past.append(_bdot(qbd, kp[i][...].reshape(FOX_W, PAGE_SIZE)) + (cq_abs - cp))

    m = jnp.max(s_new, axis=-1, keepdims=True)
    for s in s_past:
        m = jnp.maximum(m, jnp.max(s, axis=-1, keepdims=True))
    p = jnp.exp(s_new - m)
    l = jnp.sum(p, axis=-1, keepdims=True)
    o = _bdot(p, jnp.concatenate([vn_ref[...], zeros_tail], axis=0))
    for i in range(n_pages):
        p = jnp.exp(s_past[i] - m)
        l = l + jnp.sum(p, axis=-1, keepdims=True)
        o = o + _bdot_nt(p, vp[i][...].reshape(FOX_W, PAGE_SIZE))
    o = o / l
    omask = _iota((nrow, FOX_W), 1) // HEAD_DIM == _iota((nrow, FOX_W), 0) % FOX_H
    o_ref[...] = jnp.sum(jnp.where(omask, o, 0.0).reshape(t_new, FOX_H, FOX_W), axis=1)


def _foxsample(page_table, layer, qn, kn, vn, lfn, cache_kt, cache_vt, cache_lft, t_new):
    db, n_pages = page_table.shape
    pt = page_table.reshape(-1)
    new = lambda wd: pl.BlockSpec((None, SUBLANES, wd), lambda b, pt: (b, 0, 0))

    def page(shape, i):
        nd = len(shape)
        return pl.BlockSpec((None, None) + shape, lambda b, pt: (layer, pt[b * n_pages + i]) + (0,) * nd)

    in_specs = [new(FOX_W), new(FOX_W), new(FOX_W), new(LANES)]
    in_specs += [page((FOX_H, HEAD_DIM, PAGE_SIZE), i) for i in range(n_pages)]
    in_specs += [page((FOX_H, HEAD_DIM, PAGE_SIZE), i) for i in range(n_pages)]
    in_specs += [page((FOX_H, PAGE_SIZE), i) for i in range(n_pages)]
    return pl.pallas_call(
        functools.partial(_foxsample_kernel, n_pages=n_pages, t_new=t_new),
        grid_spec=pltpu.PrefetchScalarGridSpec(
            num_scalar_prefetch=1,
            grid=(db,),
            in_specs=in_specs,
            out_specs=pl.BlockSpec((None, t_new, FOX_W), lambda b, pt: (b, 0, 0)),
        ),
        out_shape=jax.ShapeDtypeStruct((db, t_new, FOX_W), F32),
        compiler_params=_cparams(("arbitrary",)),
        name="foxsample",
    )(pt, qn, kn, vn, lfn, *([cache_kt] * n_pages), *([cache_vt] * n_pages), *([cache_lft] * n_pages))


def _rwkvprep_kernel(p_ref, pp_ref, mu_ref, w0_ref, w2_ref, a0_ref, a2_ref, g2_ref, kk_ref, ka_ref,
                     r_out, w_out, k_out, v_out, a_out, b_out, g_out):
    e = _seg_ones()
    p = p_ref[...]
    xs = p + (pp_ref[...] - p) * mu_ref[...]
    r = xs[:, 0:RWKV_W]
    k = xs[:, RWKV_W:2 * RWKV_W]
    v = xs[:, 2 * RWKV_W:3 * RWKV_W]
    lo = xs[:, 3 * RWKV_W:]
    w = -_softplus(-(w0_ref[...] + _bdot(jnp.tanh(lo), w2_ref[...]))) - 0.5
    a = _sigmoid(a0_ref[...] + _bdot(lo, a2_ref[...]))
    kk = k * kk_ref[...]
    nrm = jnp.sqrt(_seg_sum(kk * kk, e))
    kk = kk / jnp.maximum(nrm, 1e-12)
    r_out[...] = r
    w_out[...] = jnp.exp(-jnp.exp(w))
    k_out[...] = k * (1.0 + (a - 1.0) * ka_ref[...])
    v_out[...] = v
    a_out[...] = -kk
    b_out[...] = kk * a
    g_out[...] = _bdot(_sigmoid(lo), g2_ref[...])


def _rwkvprep(p, pp, mu, w0, w2p, a0, a2p, g2p, k_k, k_a, tm):
    n = p.shape[0]
    row = lambda wd: pl.BlockSpec((tm, wd), lambda i: (i, 0))
    vec = _resident((1, RWKV_W))
    lora = _resident((LANES, RWKV_W))
    return pl.pallas_call(
        _rwkvprep_kernel,
        grid=(n // tm,),
        in_specs=[row(RWKV_PROJ), row(RWKV_PROJ), _resident((1, RWKV_PROJ)), vec, lora, vec, lora, lora, vec, vec],
        out_specs=[row(RWKV_W)] * 7,
        out_shape=[jax.ShapeDtypeStruct((n, RWKV_W), F32)] * 7,
        compiler_params=_cparams(("parallel",)),
        name="rwkvprep",
    )(p, pp, mu, w0, w2p, a0, a2p, g2p, k_k, k_a)


def _rwkvscan_kernel(r_ref, w_ref, k_ref, v_ref, a_ref, b_ref, s0_ref, y_ref, st_ref, s_scr, *, bblk, tblk):
    tb = pl.program_id(1)
    ng = HEADS_PER_TILE * bblk

    @pl.when(tb == 0)
    def _():
        s_scr[...] = s0_ref[...]

    e = _seg_ones()
    diag = (_iota((HEAD_DIM, LANES), 1) % HEAD_DIM == _iota((HEAD_DIM, LANES), 0)).astype(BF16)[None]
    half = (_iota((SUBLANES, LANES), 1) // HEAD_DIM == _iota((SUBLANES, LANES), 0)).astype(BF16)
    first = _iota((1, LANES), 1) < HEAD_DIM

    sub = min(SUBLANES, tblk)

    def chunk(c, carry):
        off = pl.multiple_of(c * sub, sub)
        tiles = [[ref[b, pl.ds(off, sub), :] for b in range(bblk)] for ref in (r_ref, w_ref, k_ref, v_ref, a_ref, b_ref)]

        def rows(kind, i):
            return jnp.concatenate(
                [tiles[kind][b][i:i + 1, hp * LANES:(hp + 1) * LANES][None] for b in range(bblk) for hp in range(2)],
                axis=0)

        s = s_scr[...]
        ytiles = [[] for _ in range(bblk)]
        for i in range(sub):
            sa = _seg_sum((s * rows(4, i)).reshape(ng * HEAD_DIM, LANES), e).reshape(ng, HEAD_DIM, LANES)
            vdiag = (rows(3, i).astype(BF16) * diag).reshape(ng * HEAD_DIM, LANES)
            vcol = jnp.dot(vdiag, e, preferred_element_type=F32).reshape(ng, HEAD_DIM, LANES)
            s = s * rows(1, i) + sa * rows(5, i) + vcol * rows(2, i)
            sr = (s * rows(0, i)).astype(BF16)
            for b in range(bblk):
                out = lax.dot_general(half, sr[2 * b:2 * b + 2].reshape(2 * HEAD_DIM, LANES),
                                      (((1,), (1,)), ((), ())), preferred_element_type=F32)
                h0, h1 = out[0:1, :], out[1:2, :]
                ytiles[b].append(jnp.concatenate(
                    [jnp.where(first, h0, pltpu.roll(h1, HEAD_DIM, 1)),
                     jnp.where(first, pltpu.roll(h0, HEAD_DIM, 1), h1)], axis=1))
        s_scr[...] = s
        for b in range(bblk):
            y_ref[b, pl.ds(off, sub), :] = jnp.concatenate(ytiles[b], axis=0)
        return carry

    lax.fori_loop(0, tblk // sub, chunk, 0)

    @pl.when(tb == pl.num_programs(1) - 1)
    def _():
        st_ref[...] = s_scr[...]


def _rwkvscan(r, w, k, v, a, b, s0, bblk, tblk):
    nb, t, _ = r.shape
    ng = HEADS_PER_TILE * bblk
    seq = pl.BlockSpec((bblk, tblk, RWKV_W), lambda i, j: (i, j, 0))
    st = pl.BlockSpec((ng, HEAD_DIM, LANES), lambda i, j: (i, 0, 0))
    return pl.pallas_call(
        functools.partial(_rwkvscan_kernel, bblk=bblk, tblk=tblk),
        grid=(nb // bblk, t // tblk),
        in_specs=[seq] * 6 + [st],
        out_specs=[seq, st],
        out_shape=[jax.ShapeDtypeStruct((nb, t, RWKV_W), F32), jax.ShapeDtypeStruct(s0.shape, F32)],
        scratch_shapes=[pltpu.VMEM((ng, HEAD_DIM, LANES), F32)],
        compiler_params=_cparams(("parallel", "arbitrary")),
        name="rwkvscan",
    )(r, w, k, v, a, b, s0)


def _rope_kernel(inv_ref, cos_ref, sin_ref, *, pos0):
    t = cos_ref.shape[0]
    pos = (pos0 + _iota((t, LANES), 0)).astype(F32)
    ang = pos * inv_ref[...]
    first_half = _iota((t, LANES), 1) % HEAD_DIM < HEAD_DIM // 2
    cos_ref[...] = jnp.cos(ang)
    sin_ref[...] = jnp.where(first_half, -jnp.sin(ang), jnp.sin(ang))


def _rope_tables(inv_lanes, t, pos0):
    return pl.pallas_call(
        functools.partial(_rope_kernel, pos0=pos0),
        out_shape=[jax.ShapeDtypeStruct((t, LANES), F32)] * 2,
        name="rope",
    )(inv_lanes)


def _ret_kernel(q_ref, k_ref, v_ref, g_ref, cos_ref, sin_ref, lg_ref, lgh_ref, gn_ref, r0_ref,
                o_ref, rt_ref, r_scr, *, l_in, l_true):
    lp = RET_CHUNK
    c = pl.program_id(2)

    @pl.when(c == 0)
    def _():
        r_scr[...] = r0_ref[...]

    def pad(x):
        if l_in == lp:
            return x
        return jnp.concatenate([x, jnp.zeros((lp - l_in, x.shape[1]), F32)], axis=0)

    lane = _iota((lp, LANES), 1)
    first_half = lane % HEAD_DIM < HEAD_DIM // 2
    cos = pad(cos_ref[...])
    sin = pad(sin_ref[...])

    def rope(x):
        swapped = jnp.where(first_half, pltpu.roll(x, LANES - HEAD_DIM // 2, 1), pltpu.roll(x, HEAD_DIM // 2, 1))
        return x * cos + swapped * sin

    q = rope(pad(q_ref[...]))
    k = rope(pad(k_ref[...])) * (HEAD_DIM ** -0.5)
    v = pad(v_ref[...])
    lg = lg_ref[...]
    idx = _iota((lp, 1), 0).astype(F32)
    diff = (_iota((lp, lp), 0) - _iota((lp, lp), 1)).astype(F32)
    vb = v.astype(BF16)
    kb = k.astype(BF16)
    outs = []
    for hh in range(HEADS_PER_TILE):
        dmask = jnp.where(diff >= 0.0, jnp.exp(lgh_ref[hh:hh + 1, :] * jnp.maximum(diff, 0.0)), 0.0)
        qh = jnp.where(lane // HEAD_DIM == hh, q, 0.0).astype(BF16)
        s = lax.dot_general(qh, kb, (((1,), (1,)), ((), ())), preferred_element_type=F32) * dmask
        outs.append(jnp.dot(s.astype(BF16), vb, preferred_element_type=F32))
    o = jnp.where(lane // HEAD_DIM == 0, outs[0], outs[1])
    r = r_scr[...]
    o = o + _bdot(q, r) * jnp.exp(lg * (idx + 1.0))
    kw = k * jnp.exp(lg * jnp.maximum(l_true - 1.0 - idx, 0.0))
    same_head = _iota((LANES, LANES), 0) // HEAD_DIM == _iota((LANES, LANES), 1) // HEAD_DIM
    upd = jnp.dot(kw.T.astype(BF16), vb, preferred_element_type=F32)
    r_new = r * jnp.exp(lg * float(l_true)) + jnp.where(same_head, upd, 0.0)
    r_scr[...] = r_new

    @pl.when(c == pl.num_programs(2) - 1)
    def _():
        rt_ref[...] = r_new

    e = _seg_ones()
    mean = _seg_sum(o, e) * (1.0 / HEAD_DIM)
    xc = o - mean
    var = _seg_sum(xc * xc, e) * (1.0 / HEAD_DIM)
    g = pad(g_ref[...])
    y = xc * lax.rsqrt(var + RET_GN_EPS) * gn_ref[...] * (g * _sigmoid(g))
    o_ref[...] = y[:l_in, :]


def _retention(proj, cos, sin, lg, lgh, gn, r0, l_in, l_true):
    nb, t, _ = proj.shape
    nc = t // l_in
    npair = RET_H // HEADS_PER_TILE
    col = lambda base: pl.BlockSpec((None, l_in, LANES), lambda b, h, c: (b, c, base + h))
    tab = pl.BlockSpec((l_in, LANES), lambda b, h, c: (c, 0))
    st = pl.BlockSpec((None, LANES, LANES), lambda b, h, c: (b * npair + h, 0, 0))
    return pl.pallas_call(
        functools.partial(_ret_kernel, l_in=l_in, l_true=l_true),
        grid=(nb, npair, nc),
        in_specs=[col(0), col(npair), col(2 * npair), col(3 * npair), tab, tab,
                  pl.BlockSpec((None, 1, LANES), lambda b, h, c: (h, 0, 0)),
                  pl.BlockSpec((None, HEADS_PER_TILE, LANES), lambda b, h, c: (h, 0, 0)),
                  pl.BlockSpec((1, LANES), lambda b, h, c: (0, h)),
                  st],
        out_specs=[pl.BlockSpec((None, l_in, LANES), lambda b, h, c: (b, c, h)), st],
        out_shape=[jax.ShapeDtypeStruct((nb, t, RET_W), F32), jax.ShapeDtypeStruct(r0.shape, F32)],
        scratch_shapes=[pltpu.VMEM((LANES, LANES), F32)],
        compiler_params=_cparams(("parallel", "parallel", "arbitrary")),
        name="retention",
    )(proj, proj, proj, proj, cos, sin, lg, lgh, gn, r0)


def _outproj_kernel(x_ref, fo_ref, y_ref, r_ref, k_ref, v_ref, g_ref, eo_ref, lw_ref, lb_ref, rk_ref,
                    wf_ref, wr_ref, we_ref, o_ref):
    e = _seg_ones()
    y = y_ref[...]
    mean = _seg_sum(y, e) * (1.0 / HEAD_DIM)
    yc = y - mean
    var = _seg_sum(yc * yc, e) * (1.0 / HEAD_DIM)
    yn = yc * lax.rsqrt(var + RWKV_GN_EPS) * lw_ref[...] + lb_ref[...]
    v = v_ref[...]
    bonus = _seg_sum(r_ref[...] * k_ref[...] * rk_ref[...], e) * v
    ro = (yn + bonus) * g_ref[...]
    o_ref[...] = (x_ref[...] + _bdot(fo_ref[...], wf_ref[...]) + _bdot(ro, wr_ref[...])
                  + _bdot(eo_ref[...], we_ref[...]))


def _outproj(x, fo, y, r, k, v, g, eo, lw, lb, rk, wf, wr, we, tm):
    n = x.shape[0]
    row = lambda wd: pl.BlockSpec((tm, wd), lambda i: (i, 0))
    vec = _resident((1, RWKV_W))
    return pl.pallas_call(
        _outproj_kernel,
        grid=(n // tm,),
        in_specs=[row(D_MODEL), row(FOX_W)] + [row(RWKV_W)] * 6 + [vec, vec, vec,
                  _resident(wf.shape), _resident(wr.shape), _resident(we.shape)],
        out_specs=row(D_MODEL),
        out_shape=jax.ShapeDtypeStruct((n, D_MODEL), F32),
        compiler_params=_cparams(("parallel",)),
        name="outproj",
    )(x, fo, y, r, k, v, g, eo, lw, lb, rk, wf, wr, we)


def _ffn_kernel(x_ref, g_ref, wg_ref, wu_ref, wd_ref, o_ref):
    x = x_ref[...]
    h = (x * lax.rsqrt(jnp.mean(x * x, -1, keepdims=True) + NORM_EPS) * g_ref[...]).astype(BF16)
    gate = jnp.dot(h, wg_ref[...], preferred_element_type=F32)
    up = jnp.dot(h, wu_ref[...], preferred_element_type=F32)
    act = gate * _sigmoid(gate) * up
    o_ref[...] = x + _bdot(act, wd_ref[...])


def _ffn(x, g, wg, wu, wd, tm):
    n = x.shape[0]
    row = pl.BlockSpec((tm, D_MODEL), lambda i: (i, 0))
    return pl.pallas_call(
        _ffn_kernel,
        grid=(n // tm,),
        in_specs=[row, _resident((1, D_MODEL)), _resident(wg.shape), _resident(wu.shape), _resident(wd.shape)],
        out_specs=row,
        out_shape=jax.ShapeDtypeStruct((n, D_MODEL), F32),
        compiler_params=_cparams(("parallel",)),
        name="ffn",
    )(x, g, wg, wu, wd)


def _rwkv_state_in(s):
    nb = s.shape[0]
    s = s.reshape(nb, RWKV_H // 2, 2, HEAD_DIM, HEAD_DIM).transpose(0, 1, 3, 2, 4)
    return s.reshape(nb * (RWKV_H // 2), HEAD_DIM, LANES)


def _rwkv_state_out(s, nb):
    s = s.reshape(nb, RWKV_H // 2, HEAD_DIM, 2, HEAD_DIM).transpose(0, 1, 3, 2, 4)
    return s.reshape(nb, RWKV_H, HEAD_DIM, HEAD_DIM)


def _ret_state_in(r):
    nb = r.shape[0]
    r = r.reshape(nb, RET_H // 2, 2, HEAD_DIM, HEAD_DIM)
    z = jnp.zeros_like(r[:, :, 0])
    top = jnp.concatenate([r[:, :, 0], z], axis=-1)
    bot = jnp.concatenate([z, r[:, :, 1]], axis=-1)
    return jnp.concatenate([top, bot], axis=-2).reshape(nb * (RET_H // 2), LANES, LANES)


def _ret_state_out(r, nb):
    r = r.reshape(nb, RET_H // 2, LANES, LANES)
    return jnp.stack([r[:, :, :HEAD_DIM, :HEAD_DIM], r[:, :, HEAD_DIM:, HEAD_DIM:]], axis=2).reshape(
        nb, RET_H, HEAD_DIM, HEAD_DIM)


def _layer_weights(l, ln_mix_g, w_in, fox_qn_g, fox_kn_g, fox_f_b, rwkv_mu, rwkv_w0, rwkv_w2, rwkv_a0, rwkv_a2,
                   rwkv_g2, rwkv_k_k, rwkv_k_a, rwkv_r_k, rwkv_lnx_w, rwkv_lnx_b, ret_gn_w, w_out, ln_ffn_g,
                   w_gate, w_up, w_down):
    wi = w_in[l].astype(BF16)
    o_f = 3 * FOX_W
    o_r = o_f + FOX_H
    o_e = o_r + RWKV_PROJ
    pad_rows = lambda w, off: jnp.zeros((LANES, RWKV_W), BF16).at[off:off + w.shape[0]].set(w.astype(BF16))
    wo = w_out[l].astype(BF16)
    return dict(
        ln_mix_g=ln_mix_g[l][None],
        w_in=[wi[:, 0:FOX_W], wi[:, FOX_W:2 * FOX_W], wi[:, 2 * FOX_W:o_f],
              jnp.pad(wi[:, o_f:o_r], ((0, 0), (0, LANES - FOX_H))), wi[:, o_r:o_e], wi[:, o_e:]],
        qg=jnp.tile(fox_qn_g[l], FOX_H)[None], kg=jnp.tile(fox_kn_g[l], FOX_H)[None],
        fb=jnp.pad(fox_f_b[l], (0, LANES - FOX_H))[None],
        mu=rwkv_mu[l][None], w0=rwkv_w0[l][None], a0=rwkv_a0[l][None],
        w2=pad_rows(rwkv_w2[l], 0), a2=pad_rows(rwkv_a2[l], RWKV_LORA_W),
        g2=pad_rows(rwkv_g2[l], RWKV_LORA_W + RWKV_LORA_A),
        k_k=rwkv_k_k[l][None], k_a=rwkv_k_a[l][None], r_k=rwkv_r_k[l].reshape(1, RWKV_W),
        lnx_w=rwkv_lnx_w[l][None], lnx_b=rwkv_lnx_b[l][None], gn=ret_gn_w[l][None],
        wo_f=wo[:FOX_W], wo_r=wo[FOX_W:FOX_W + RWKV_W], wo_e=wo[FOX_W + RWKV_W:],
        ln_ffn_g=ln_ffn_g[l][None],
        w_gate=w_gate[l].astype(BF16), w_up=w_up[l].astype(BF16), w_down=w_down[l].astype(BF16),
    )


def _mix_and_ffn(x2, fo, p_rwkv, p_prev, s0, proj_ret, ret_tabs, r0, lw, nb, t, tm, bblk, tblk, l_in, l_true):
    r, w, k, v, a, b, g = _rwkvprep(p_rwkv, p_prev, lw['mu'], lw['w0'], lw['w2'], lw['a0'], lw['a2'], lw['g2'],
                                    lw['k_k'], lw['k_a'], tm)
    seq = lambda z: z.reshape(nb, t, RWKV_W)
    y, s_t = _rwkvscan(seq(r), seq(w), seq(k), seq(v), seq(a), seq(b), s0, bblk, tblk)
    cos, sin, lg, lgh = ret_tabs
    eo, r_t = _retention(proj_ret, cos, sin, lg, lgh, lw['gn'], r0, l_in, l_true)
    eo = eo[:, :t].reshape(nb * t, RET_W)
    x2 = _outproj(x2, fo, y.reshape(nb * t, RWKV_W), r, k, v, g, eo, lw['lnx_w'], lw['lnx_b'], lw['r_k'],
                  lw['wo_f'], lw['wo_r'], lw['wo_e'], tm)
    x2 = _ffn(x2, lw['ln_ffn_g'], lw['w_gate'], lw['w_up'], lw['w_down'], tm)
    return x2, s_t, r_t


def kernel(x_prompt, x_sample, cache_fox_k, cache_fox_v, cache_fox_logf, state_rwkv, state_rwkv_shift, state_ret,
           page_table, ln_mix_g, w_in, fox_qn_g, fox_kn_g, fox_f_b, rwkv_mu, rwkv_w0, rwkv_w2, rwkv_a0, rwkv_a2,
           rwkv_g2, rwkv_k_k, rwkv_k_a, rwkv_r_k, rwkv_lnx_w, rwkv_lnx_b, ret_gn_w, w_out, ln_ffn_g, w_gate,
           w_up, w_down):
    nb, t, _ = x_prompt.shape
    db, ts, _ = x_sample.shape
    depth = w_in.shape[0]
    n_pages = page_table.shape[1]
    past_len = n_pages * PAGE_SIZE
    n_pool = cache_fox_k.shape[1]

    half = HEAD_DIM // 2
    inv = ROPE_BASE ** (-jnp.arange(half, dtype=F32) / half)
    inv_lanes = jnp.tile(inv, LANES // half)[None]
    log_gamma = jnp.log1p(-jnp.exp2(-5.0 - jnp.arange(RET_H, dtype=F32)))
    lg = jnp.repeat(log_gamma, HEAD_DIM).reshape(RET_H // 2, 1, LANES)
    lgh = jnp.broadcast_to(log_gamma.reshape(RET_H // 2, 2, 1), (RET_H // 2, 2, LANES))
    cos_p, sin_p = _rope_tables(inv_lanes, t, 0)
    cos_s, sin_s = _rope_tables(inv_lanes, SUBLANES, past_len)

    ckt = jnp.transpose(cache_fox_k, (0, 1, 3, 4, 2))
    cvt = jnp.transpose(cache_fox_v, (0, 1, 3, 4, 2))
    clt = jnp.swapaxes(cache_fox_logf, 2, 3)

    pad8 = lambda z: jnp.pad(z.reshape(db, ts, -1), ((0, 0), (0, SUBLANES - ts), (0, 0)))

    yp = x_prompt.reshape(nb * t, D_MODEL)
    ys = x_sample.reshape(db * ts, D_MODEL)
    outs = [[] for _ in range(12)]
    for l in range(depth):
        lw = _layer_weights(l, ln_mix_g, w_in, fox_qn_g, fox_kn_g, fox_f_b, rwkv_mu, rwkv_w0, rwkv_w2, rwkv_a0,
                            rwkv_a2, rwkv_g2, rwkv_k_k, rwkv_k_a, rwkv_r_k, rwkv_lnx_w, rwkv_lnx_b, ret_gn_w,
                            w_out, ln_ffn_g, w_gate, w_up, w_down)
        fq, fk, fv, fl, p_rwkv, p_ret = _inproj(yp, lw['ln_mix_g'], lw['w_in'], 256)
        kn, lf, qx, kx, vb = _foxprep_prompt(fq, fk, fl, fv, lw['qg'], lw['kg'], lw['fb'], nb, 256)
        fo = _foxattn(qx, kx, vb, nb, 512, 512)
        p3 = p_rwkv.reshape(nb, t, RWKV_PROJ)
        p_prev = jnp.concatenate([jnp.zeros((nb, 1, RWKV_PROJ), F32), p3[:, :-1]], axis=1).reshape(nb * t, RWKV_PROJ)
        yp, s_t, r_t = _mix_and_ffn(
            yp, fo, p_rwkv, p_prev, jnp.zeros((2 * nb, HEAD_DIM, LANES), F32), p_ret.reshape(nb, t, RET_PROJ),
            (cos_p, sin_p, lg, lgh), jnp.zeros((2 * nb, LANES, LANES), F32), lw, nb, t, 256, nb, 128,
            RET_CHUNK, RET_CHUNK)
        outs[0].append(kn); outs[1].append(fv); outs[2].append(lf[:, :FOX_H])
        outs[3].append(_rwkv_state_out(s_t, nb)); outs[4].append(p3[:, -1]); outs[5].append(_ret_state_out(r_t, nb))
        fq, fk, fv, fl, p_rwkv, p_ret = _inproj(ys, lw['ln_mix_g'], lw['w_in'], 256)
        qn, kn, lf = _foxprep_sample(fq, fk, fl, lw['qg'], lw['kg'], lw['fb'], 256)
        fo = _foxsample(page_table, l, pad8(qn), pad8(kn), pad8(fv), pad8(lf), ckt, cvt, clt, ts)
        p3 = p_rwkv.reshape(db, ts, RWKV_PROJ)
        p_prev = jnp.concatenate([state_rwkv_shift[l][:, None, :], p3[:, :-1]], axis=1).reshape(db * ts, RWKV_PROJ)
        ys, s_t, r_t = _mix_and_ffn(
            ys, fo.reshape(db * ts, FOX_W), p_rwkv, p_prev, _rwkv_state_in(state_rwkv[l]), pad8(p_ret),
            (cos_s, sin_s, lg, lgh), _ret_state_in(state_ret[l]), lw, db, ts, 256, 8, ts, SUBLANES, ts)
        outs[6].append(kn); outs[7].append(fv); outs[8].append(lf[:, :FOX_H])
        outs[9].append(_rwkv_state_out(s_t, db)); outs[10].append(p3[:, -1]); outs[11].append(_ret_state_out(r_t, db))

    n_pp = nb * t // PAGE_SIZE
    st = lambda i: jnp.stack(outs[i])
    return (yp.reshape(nb, t, D_MODEL), ys.reshape(db, ts, D_MODEL),
            st(0).reshape(depth, n_pp, PAGE_SIZE, FOX_H, HEAD_DIM),
            st(1).reshape(depth, n_pp, PAGE_SIZE, FOX_H, HEAD_DIM),
            st(2).reshape(depth, n_pp, PAGE_SIZE, FOX_H),
            st(3), st(4), st(5),
            st(6).reshape(depth, db, ts, FOX_H, HEAD_DIM),
            st(7).reshape(depth, db, ts, FOX_H, HEAD_DIM),
            st(8).reshape(depth, db, ts, FOX_H),
            st(9), st(10), st(11))
```

```python
import functools

import jax
import jax.numpy as jnp
import numpy as np
from jax import lax
from jax.experimental import pallas as pl
from jax.experimental.pallas import tpu as pltpu

F32 = jnp.float32
BF16 = jnp.bfloat16

LANES = 128
SUBLANES = 8
VMEM_LIMIT = 56 * 1024 * 1024

D_MODEL = 1024
HEAD_DIM = 64
FOX_H = 8
RWKV_H = 4
RET_H = 4
FOX_W = FOX_H * HEAD_DIM
RWKV_W = RWKV_H * HEAD_DIM
RET_W = RET_H * HEAD_DIM
RWKV_LORA_W = 32
RWKV_LORA_A = 32
RWKV_LORA_G = 64
RWKV_PROJ = 3 * RWKV_W + RWKV_LORA_W + RWKV_LORA_A + RWKV_LORA_G
RET_PROJ = 4 * RET_W
D_FF = 2816
PAGE_SIZE = 128
RET_CHUNK = 128
ROPE_BASE = 10000.0
NORM_EPS = 1e-6
RWKV_GN_EPS = 64e-5
RET_GN_EPS = 1e-5
NEG_BIG = -1e30
HEADS_PER_TILE = LANES // HEAD_DIM


def _cparams(sem):
    return pltpu.CompilerParams(dimension_semantics=sem, vmem_limit_bytes=VMEM_LIMIT)


def _resident(shape):
    nd = len(shape)
    return pl.BlockSpec(shape, lambda *_: (0,) * nd, pipeline_mode=pl.Buffered(1))


def _bdot(a, b):
    return jnp.dot(a.astype(BF16), b.astype(BF16), preferred_element_type=F32)


def _bdot_nt(a, b):
    return lax.dot_general(a.astype(BF16), b.astype(BF16), (((1,), (1,)), ((), ())),
                           preferred_element_type=F32)


def _split2(x):
    hi = x.astype(BF16)
    lo = (x - hi.astype(F32)).astype(BF16)
    return hi, lo


def _split3(x):
    hi = x.astype(BF16)
    r = x - hi.astype(F32)
    mid = r.astype(BF16)
    lo = (r - mid.astype(F32)).astype(BF16)
    return hi, mid, lo


def _iota(shape, axis):
    return lax.broadcasted_iota(jnp.int32, shape, axis)


def _seg_ones():
    return (_iota((LANES, LANES), 0) // HEAD_DIM == _iota((LANES, LANES), 1) // HEAD_DIM).astype(BF16)


def _seg_sum(x, e):
    outs = []
    for c in range(x.shape[-1] // LANES):
        hi, lo = _split2(x[:, c * LANES:(c + 1) * LANES])
        outs.append(jnp.dot(hi, e, preferred_element_type=F32) + jnp.dot(lo, e, preferred_element_type=F32))
    return outs[0] if len(outs) == 1 else jnp.concatenate(outs, axis=-1)


def _exact_dot(x, m01, left):
    parts = _split3(x)
    if left:
        return sum(jnp.dot(m01, p, preferred_element_type=F32) for p in parts)
    return sum(jnp.dot(p, m01, preferred_element_type=F32) for p in parts)


def _sigmoid(x):
    return 1.0 / (1.0 + jnp.exp(-x))


def _softplus(x):
    return jnp.maximum(x, 0.0) + jnp.log1p(jnp.exp(-jnp.abs(x)))


def _inproj_kernel(x_ref, g_ref, wq_ref, wk_ref, wv_ref, wl_ref, wr_ref, we_ref,
                   oq_ref, ok_ref, ov_ref, ol_ref, or_ref, oe_ref):
    x = x_ref[...]
    h = x * lax.rsqrt(jnp.mean(x * x, -1, keepdims=True) + NORM_EPS) * g_ref[...]
    hb = h.astype(BF16)
    for w_ref, o_ref in ((wq_ref, oq_ref), (wk_ref, ok_ref), (wv_ref, ov_ref), (wl_ref, ol_ref),
                         (wr_ref, or_ref), (we_ref, oe_ref)):
        o_ref[...] = jnp.dot(hb, w_ref[...], preferred_element_type=F32)


def _inproj(x, g, ws, tm):
    n = x.shape[0]
    widths = [w.shape[1] for w in ws]
    row = lambda wd: pl.BlockSpec((tm, wd), lambda i: (i, 0))
    return pl.pallas_call(
        _inproj_kernel,
        grid=(n // tm,),
        in_specs=[row(D_MODEL), _resident((1, D_MODEL))] + [_resident(w.shape) for w in ws],
        out_specs=[row(wd) for wd in widths],
        out_shape=[jax.ShapeDtypeStruct((n, wd), F32) for wd in widths],
        compiler_params=_cparams(("parallel",)),
        name="inproj",
    )(x, g, *ws)


def _fox_norms(q_ref, k_ref, fl_ref, qg_ref, kg_ref, fb_ref):
    e = _seg_ones()

    def hnorm(x, g):
        ms = _seg_sum(x * x, e) * (1.0 / HEAD_DIM)
        return x * lax.rsqrt(ms + NORM_EPS) * g

    qn = hnorm(q_ref[...], qg_ref[...]) * (HEAD_DIM ** -0.5)
    kn = hnorm(k_ref[...], kg_ref[...])
    z = fl_ref[...] + fb_ref[...]
    lf = jnp.minimum(z, 0.0) - jnp.log1p(jnp.exp(-jnp.abs(z)))
    return qn, kn, lf


def _foxprep_sample_kernel(q_ref, k_ref, fl_ref, qg_ref, kg_ref, fb_ref, qn_ref, kn_ref, lf_ref):
    qn_ref[...], kn_ref[...], lf_ref[...] = _fox_norms(q_ref, k_ref, fl_ref, qg_ref, kg_ref, fb_ref)


def _foxprep_prompt_kernel(q_ref, k_ref, fl_ref, qg_ref, kg_ref, fb_ref, v_ref,
                           kn_ref, lf_ref, qx_ref, kx_ref, vb_ref, carry_ref, *, tm):
    qn, kn, lf = _fox_norms(q_ref, k_ref, fl_ref, qg_ref, kg_ref, fb_ref)
    kn_ref[...] = kn
    lf_ref[...] = lf
    vb_ref[...] = v_ref[...].astype(BF16)

    @pl.when(pl.program_id(1) == 0)
    def _():
        carry_ref[...] = jnp.zeros_like(carry_ref)

    tri = (_iota((tm, tm), 0) >= _iota((tm, tm), 1)).astype(BF16)
    c = _exact_dot(lf, tri, left=True) + carry_ref[...]
    carry_ref[...] = c[tm - 1:tm, :]

    lane = _iota((tm, LANES), 1)
    j = lane % HEAD_DIM
    for h in range(FOX_H):
        hp, hh = divmod(h, HEADS_PER_TILE)
        c_hi, c_mid, c_lo = (p.astype(F32) for p in _split3(jnp.broadcast_to(c[:, h:h + 1], (tm, LANES))))
        ext_q = jnp.where(j == 0, c_hi, jnp.where(j == 1, c_mid, jnp.where(j == 2, c_lo, jnp.where(j < 6, 1.0, 0.0))))
        ext_k = jnp.where(j < 3, 1.0, jnp.where(j == 3, -c_hi, jnp.where(j == 4, -c_mid, jnp.where(j == 5, -c_lo, 0.0))))
        own = lane // HEAD_DIM == hh
        pair = slice(hp * LANES, (hp + 1) * LANES)
        tile = slice(h * LANES, (h + 1) * LANES)
        qx_ref[:, tile] = jnp.where(own, qn[:, pair], ext_q).astype(BF16)
        kx_ref[:, tile] = jnp.where(own, kn[:, pair], ext_k).astype(BF16)


def _foxprep_sample(fq, fk, fl, qg, kg, fb, tm):
    n = fq.shape[0]
    row = lambda wd: pl.BlockSpec((tm, wd), lambda i: (i, 0))
    return pl.pallas_call(
        _foxprep_sample_kernel,
        grid=(n // tm,),
        in_specs=[row(FOX_W), row(FOX_W), row(LANES), _resident((1, FOX_W)), _resident((1, FOX_W)),
                  _resident((1, LANES))],
        out_specs=[row(FOX_W), row(FOX_W), row(LANES)],
        out_shape=[jax.ShapeDtypeStruct((n, FOX_W), F32), jax.ShapeDtypeStruct((n, FOX_W), F32),
                   jax.ShapeDtypeStruct((n, LANES), F32)],
        compiler_params=_cparams(("parallel",)),
        name="foxprep_sample",
    )(fq, fk, fl, qg, kg, fb)


def _foxprep_prompt(fq, fk, fl, fv, qg, kg, fb, nseq, tm):
    n = fq.shape[0]
    nt = n // nseq // tm
    row = lambda wd: pl.BlockSpec((tm, wd), lambda b, j: (b * nt + j, 0))
    return pl.pallas_call(
        functools.partial(_foxprep_prompt_kernel, tm=tm),
        grid=(nseq, nt),
        in_specs=[row(FOX_W), row(FOX_W), row(LANES), _resident((1, FOX_W)), _resident((1, FOX_W)),
                  _resident((1, LANES)), row(FOX_W)],
        out_specs=[row(FOX_W), row(LANES), row(FOX_H * LANES), row(FOX_H * LANES), row(FOX_W)],
        out_shape=[jax.ShapeDtypeStruct((n, FOX_W), F32), jax.ShapeDtypeStruct((n, LANES), F32),
                   jax.ShapeDtypeStruct((n, FOX_H * LANES), BF16), jax.ShapeDtypeStruct((n, FOX_H * LANES), BF16),
                   jax.ShapeDtypeStruct((n, FOX_W), BF16)],
        scratch_shapes=[pltpu.VMEM((1, LANES), F32)],
        compiler_params=_cparams(("parallel", "arbitrary")),
        name="foxprep_prompt",
    )(fq, fk, fl, qg, kg, fb, fv)


def _foxattn_kernel(qx_ref, kx_ref, vb_ref, o_ref, *, tq, tk):
    qi = pl.program_id(2)
    q = [qx_ref[:, hh * LANES:(hh + 1) * LANES] for hh in range(HEADS_PER_TILE)]
    n_full = (qi * tq) // tk

    def block(off, carry, diagonal):
        vb = vb_ref[pl.ds(off, tk), :]
        new = []
        for hh in range(HEADS_PER_TILE):
            m, l, acc = carry[hh]
            kb = kx_ref[pl.ds(off, tk), hh * LANES:(hh + 1) * LANES]
            s = lax.dot_general(q[hh], kb, (((1,), (1,)), ((), ())), preferred_element_type=F32)
            if diagonal:
                s = jnp.where(off + _iota((tq, tk), 1) <= qi * tq + _iota((tq, tk), 0), s, NEG_BIG)
            m_new = jnp.maximum(m, jnp.max(s, axis=-1, keepdims=True))
            alpha = jnp.exp(m - m_new)
            p = jnp.exp(s - m_new)
            l = alpha * l + jnp.sum(p, axis=-1, keepdims=True)
            acc = alpha * acc + jnp.dot(p.astype(BF16), vb, preferred_element_type=F32)
            new.append((m_new, l, acc))
        return tuple(new)

    init = tuple((jnp.full((tq, 1), NEG_BIG, F32), jnp.zeros((tq, 1), F32), jnp.zeros((tq, LANES), F32))
                 for _ in range(HEADS_PER_TILE))
    carry = lax.fori_loop(0, n_full, lambda j, c: block(pl.multiple_of(j * tk, tk), c, False), init)
    carry = block(pl.multiple_of(n_full * tk, tk), carry, True)
    outs = [acc / l for _, l, acc in carry]
    o_ref[...] = jnp.where(_iota((tq, LANES), 1) // HEAD_DIM == 0, outs[0], outs[1])


def _foxattn(qx, kx, vb, nseq, tq, tk):
    n = qx.shape[0]
    t = n // nseq
    nq = t // tq
    assert tk % tq == 0 and t % tk == 0
    npair = FOX_H // HEADS_PER_TILE
    pair_w = HEADS_PER_TILE * LANES
    return pl.pallas_call(
        functools.partial(_foxattn_kernel, tq=tq, tk=tk),
        grid=(nseq, npair, nq),
        in_specs=[
            pl.BlockSpec((tq, pair_w), lambda b, h, i: (b * nq + i, h)),
            pl.BlockSpec((t, pair_w), lambda b, h, i: (b, h)),
            pl.BlockSpec((t, LANES), lambda b, h, i: (b, h)),
        ],
        out_specs=pl.BlockSpec((tq, LANES), lambda b, h, i: (b * nq + i, h)),
        out_shape=jax.ShapeDtypeStruct((n, FOX_W), F32),
        compiler_params=_cparams(("parallel", "parallel", "arbitrary")),
        name="foxattn",
    )(qx, kx, vb)


def _foxsample_kernel(pt_ref, q_ref, kn_ref, vn_ref, lfn_ref, *rest, n_pages, t_new):
    del pt_ref
    kp = rest[:n_pages]
    vp = rest[n_pages:2 * n_pages]
    lp = rest[2 * n_pages:3 * n_pages]
    o_ref = rest[3 * n_pages]
    nrow = t_new * FOX_H

    rep = lambda x: jnp.concatenate([x] * t_new, axis=0)
    q = q_ref[...]
    hmask = _iota((FOX_H, FOX_W), 1) // HEAD_DIM == _iota((FOX_H, FOX_W), 0)
    qbd = jnp.concatenate(
        [jnp.where(hmask, jnp.broadcast_to(q[t:t + 1, :], (FOX_H, FOX_W)), 0.0) for t in range(t_new)],
        axis=0).astype(BF16)

    lfn = lfn_ref[...]
    diag8 = _iota((FOX_H, LANES), 1) == _iota((FOX_H, LANES), 0)
    cn_cols = []
    run = jnp.zeros((1, LANES), F32)
    for t in range(t_new):
        run = run + lfn[t:t + 1, :]
        cn_cols.append(jnp.sum(jnp.where(diag8, jnp.broadcast_to(run, (FOX_H, LANES)), 0.0), axis=-1, keepdims=True))
    cn = jnp.concatenate(cn_cols, axis=0)

    zeros_tail = jnp.zeros((PAGE_SIZE - SUBLANES, FOX_W), F32)
    key = _iota((nrow, PAGE_SIZE), 1)
    trow = _iota((nrow, PAGE_SIZE), 0) // FOX_H
    ckey = jnp.zeros((nrow, PAGE_SIZE), F32)
    for j in range(t_new):
        ckey = jnp.where(key == j, rep(cn_cols[j]), ckey)
    s_new = _bdot_nt(qbd, jnp.concatenate([kn_ref[...], zeros_tail], axis=0)) + (cn - ckey)
    s_new = jnp.where(key <= trow, s_new, NEG_BIG)

    lf_all = jnp.concatenate([lp[i][...] for i in range(n_pages)], axis=0)
    upper = (_iota((PAGE_SIZE, PAGE_SIZE), 0) <= _iota((PAGE_SIZE, PAGE_SIZE), 1)).astype(BF16)
    cp_all = _exact_dot(lf_all, upper, left=False)
    before = [jnp.zeros((FOX_H, 1), F32)]
    for i in range(n_pages):
        before.append(before[-1] + cp_all[i * FOX_H:(i + 1) * FOX_H, PAGE_SIZE - 1:PAGE_SIZE])
    cq_abs = rep(before[n_pages]) + cn
    s_past = []
    for i in range(n_pages):
        cp = rep(cp_all[i * FOX_H:(i + 1) * FOX_H, :] + before[i])
        s_past.append(_bdot(qbd, kp[i][...].reshape(FOX_W, PAGE_SIZE)) + (cq_abs - cp))

    m = jnp.max(s_new, axis=-1, keepdims=True)
    for s in s_past:
        m = jnp.maximum(m, jnp.max(s, axis=-1, keepdims=True))
    p = jnp.exp(s_new - m)
    l = jnp.sum(p, axis=-1, keepdims=True)
    o = _bdot(p, jnp.concatenate([vn_ref[...], zeros_tail], axis=0))
    for i in range(n_pages):
        p = jnp.exp(s_past[i] - m)
        l = l + jnp.sum(p, axis=-1, keepdims=True)
        o = o + _bdot_nt(p, vp[i][...].reshape(FOX_W, PAGE_SIZE))
    o = o / l
    omask = _iota((nrow, FOX_W), 1) // HEAD_DIM == _iota((nrow, FOX_W), 0) % FOX_H
    o_ref[...] = jnp.sum(jnp.where(omask, o, 0.0).reshape(t_new, FOX_H, FOX_W), axis=1)


def _foxsample(page_table, layer, qn, kn, vn, lfn, cache_kt, cache_vt, cache_lft, t_new):
    db, n_pages = page_table.shape
    pt = page_table.reshape(-1)
    new = lambda wd: pl.BlockSpec((None, SUBLANES, wd), lambda b, pt: (b, 0, 0))

    def page(shape, i):
        nd = len(shape)
        return pl.BlockSpec((None, None) + shape, lambda b, pt: (layer, pt[b * n_pages + i]) + (0,) * nd)

    in_specs = [new(FOX_W), new(FOX_W), new(FOX_W), new(LANES)]
    in_specs += [page((FOX_H, HEAD_DIM, PAGE_SIZE), i) for i in range(n_pages)]
    in_specs += [page((FOX_H, HEAD_DIM, PAGE_SIZE), i) for i in range(n_pages)]
    in_specs += [page((FOX_H, PAGE_SIZE), i) for i in range(n_pages)]
    return pl.pallas_call(
        functools.partial(_foxsample_kernel, n_pages=n_pages, t_new=t_new),
        grid_spec=pltpu.PrefetchScalarGridSpec(
            num_scalar_prefetch=1,
            grid=(db,),
            in_specs=in_specs,
            out_specs=pl.BlockSpec((None, t_new, FOX_W), lambda b, pt: (b, 0, 0)),
        ),
        out_shape=jax.ShapeDtypeStruct((db, t_new, FOX_W), F32),
        compiler_params=_cparams(("arbitrary",)),
        name="foxsample",
    )(pt, qn, kn, vn, lfn, *([cache_kt] * n_pages), *([cache_vt] * n_pages), *([cache_lft] * n_pages))


def _rwkvprep_kernel(p_ref, init_ref, mu_ref, w0_ref, w2_ref, a0_ref, a2_ref, g2_ref, kk_ref, ka_ref,
                     r_out, w_out, k_out, v_out, a_out, b_out, g_out, *carry, tm, period):
    e = _seg_ones()
    p = p_ref[...]
    rolled = pltpu.roll(p, 1, 0)
    row = _iota((tm, RWKV_PROJ), 0)
    if period >= tm:
        carry_ref, = carry

        @pl.when(pl.program_id(1) == 0)
        def _():
            carry_ref[...] = init_ref[...]

        pp = jnp.where(row == 0, carry_ref[...], rolled)
        carry_ref[...] = p[tm - 1:tm, :]
    else:
        pp = jnp.where(row % period == 0, init_ref[...], rolled)
    xs = p + (pp - p) * mu_ref[...]
    r = xs[:, 0:RWKV_W]
    k = xs[:, RWKV_W:2 * RWKV_W]
    v = xs[:, 2 * RWKV_W:3 * RWKV_W]
    lo = xs[:, 3 * RWKV_W:]
    w = -_softplus(-(w0_ref[...] + _bdot(jnp.tanh(lo), w2_ref[...]))) - 0.5
    a = _sigmoid(a0_ref[...] + _bdot(lo, a2_ref[...]))
    kk = k * kk_ref[...]
    nrm = jnp.sqrt(_seg_sum(kk * kk, e))
    kk = kk / jnp.maximum(nrm, 1e-12)
    r_out[...] = r
    w_out[...] = jnp.exp(-jnp.exp(w))
    k_out[...] = k * (1.0 + (a - 1.0) * ka_ref[...])
    v_out[...] = v
    a_out[...] = -kk
    b_out[...] = kk * a
    g_out[...] = _bdot(_sigmoid(lo), g2_ref[...])


def _rwkvprep(p, shift, period, mu, w0, w2p, a0, a2p, g2p, k_k, k_a, tm):
    n = p.shape[0]
    nseq = n // period
    vec = _resident((1, RWKV_W))
    lora = _resident((LANES, RWKV_W))
    if period >= tm:
        nt = period // tm
        grid = (nseq, nt)
        row = lambda wd: pl.BlockSpec((tm, wd), lambda b, j: (b * nt + j, 0))
        init = shift[:, None, :]
        init_spec = pl.BlockSpec((None, 1, RWKV_PROJ), lambda b, j: (b, 0, 0))
        scratch = [pltpu.VMEM((1, RWKV_PROJ), F32)]
        sem = ("parallel", "arbitrary")
    else:
        grid = (n // tm,)
        row = lambda wd: pl.BlockSpec((tm, wd), lambda i: (i, 0))
        init = jnp.repeat(shift, period, axis=0)
        init_spec = row(RWKV_PROJ)
        scratch = []
        sem = ("parallel",)
    return pl.pallas_call(
        functools.partial(_rwkvprep_kernel, tm=tm, period=period),
        grid=grid,
        in_specs=[row(RWKV_PROJ), init_spec, _resident((1, RWKV_PROJ)), vec, lora, vec, lora, lora, vec, vec],
        out_specs=[row(RWKV_W)] * 7,
        out_shape=[jax.ShapeDtypeStruct((n, RWKV_W), F32)] * 7,
        scratch_shapes=scratch,
        compiler_params=_cparams(sem),
        name="rwkvprep",
    )(p, init, mu, w0, w2p, a0, a2p, g2p, k_k, k_a)


def _rwkvscan_kernel(r_ref, w_ref, k_ref, v_ref, a_ref, b_ref, s0_ref, y_ref, st_ref, s_scr, *, bblk, tblk):
    tb = pl.program_id(1)
    ng = HEADS_PER_TILE * bblk

    @pl.when(tb == 0)
    def _():
        s_scr[...] = s0_ref[...]

    e = _seg_ones()
    diag = (_iota((HEAD_DIM, LANES), 1) % HEAD_DIM == _iota((HEAD_DIM, LANES), 0)).astype(BF16)[None]
    half = (_iota((SUBLANES, LANES), 1) // HEAD_DIM == _iota((SUBLANES, LANES), 0)).astype(BF16)
    first = _iota((1, LANES), 1) < HEAD_DIM

    sub = min(SUBLANES, tblk)

    def chunk(c, carry):
        off = pl.multiple_of(c * sub, sub)
        tiles = [[ref[b, pl.ds(off, sub), :] for b in range(bblk)] for ref in (r_ref, w_ref, k_ref, v_ref, a_ref, b_ref)]

        def rows(kind, i):
            return jnp.concatenate(
                [tiles[kind][b][i:i + 1, hp * LANES:(hp + 1) * LANES][None] for b in range(bblk) for hp in range(2)],
                axis=0)

        s = s_scr[...]
        ytiles = [[] for _ in range(bblk)]
        for i in range(sub):
            sa = jnp.dot((s * rows(4, i)).reshape(ng * HEAD_DIM, LANES).astype(BF16), e,
                         preferred_element_type=F32).reshape(ng, HEAD_DIM, LANES)
            vdiag = (rows(3, i).astype(BF16) * diag).reshape(ng * HEAD_DIM, LANES)
            vcol = jnp.dot(vdiag, e, preferred_element_type=F32).reshape(ng, HEAD_DIM, LANES)
            s = s * rows(1, i) + sa * rows(5, i) + vcol * rows(2, i)
            sr = (s * rows(0, i)).astype(BF16)
            for b in range(bblk):
                out = lax.dot_general(half, sr[2 * b:2 * b + 2].reshape(2 * HEAD_DIM, LANES),
                                      (((1,), (1,)), ((), ())), preferred_element_type=F32)
                h0, h1 = out[0:1, :], out[1:2, :]
                ytiles[b].append(jnp.concatenate(
                    [jnp.where(first, h0, pltpu.roll(h1, HEAD_DIM, 1)),
                     jnp.where(first, pltpu.roll(h0, HEAD_DIM, 1), h1)], axis=1))
        s_scr[...] = s
        for b in range(bblk):
            y_ref[b, pl.ds(off, sub), :] = jnp.concatenate(ytiles[b], axis=0)
        return carry

    lax.fori_loop(0, tblk // sub, chunk, 0)

    @pl.when(tb == pl.num_programs(1) - 1)
    def _():
        st_ref[...] = s_scr[...]


def _rwkvscan(r, w, k, v, a, b, s0, bblk, tblk):
    nb, t, _ = r.shape
    ng = HEADS_PER_TILE * bblk
    seq = pl.BlockSpec((bblk, tblk, RWKV_W), lambda i, j: (i, j, 0))
    st = pl.BlockSpec((ng, HEAD_DIM, LANES), lambda i, j: (i, 0, 0))
    return pl.pallas_call(
        functools.partial(_rwkvscan_kernel, bblk=bblk, tblk=tblk),
        grid=(nb // bblk, t // tblk),
        in_specs=[seq] * 6 + [st],
        out_specs=[seq, st],
        out_shape=[jax.ShapeDtypeStruct((nb, t, RWKV_W), F32), jax.ShapeDtypeStruct(s0.shape, F32)],
        scratch_shapes=[pltpu.VMEM((ng, HEAD_DIM, LANES), F32)],
        compiler_params=_cparams(("parallel", "arbitrary")),
        name="rwkvscan",
    )(r, w, k, v, a, b, s0)


def _rope_kernel(inv_ref, cos_ref, sin_ref, *, pos0):
    t = cos_ref.shape[0]
    pos = (pos0 + _iota((t, LANES), 0)).astype(F32)
    ang = pos * inv_ref[...]
    first_half = _iota((t, LANES), 1) % HEAD_DIM < HEAD_DIM // 2
    cos_ref[...] = jnp.cos(ang)
    sin_ref[...] = jnp.where(first_half, -jnp.sin(ang), jnp.sin(ang))


def _rope_tables(inv_lanes, t, pos0):
    return pl.pallas_call(
        functools.partial(_rope_kernel, pos0=pos0),
        out_shape=[jax.ShapeDtypeStruct((t, LANES), F32)] * 2,
        name="rope",
    )(inv_lanes)


def _ret_kernel(x_ref, cos_ref, sin_ref, lg_ref, lgh_ref, gn_ref, r0_ref, o_ref, rt_ref, r_scr,
                *, bs, l_in, l_true):
    lp = RET_CHUNK
    npair = RET_H // HEADS_PER_TILE
    c = pl.program_id(1)

    @pl.when(c == 0)
    def _():
        r_scr[...] = r0_ref[...]

    def pad(x):
        if l_in == lp:
            return x
        return jnp.concatenate([x, jnp.zeros((lp - l_in, x.shape[1]), F32)], axis=0)

    lane = _iota((lp, LANES), 1)
    first_half = lane % HEAD_DIM < HEAD_DIM // 2
    cos = pad(cos_ref[...])
    sin = pad(sin_ref[...])

    def rope(x):
        swapped = jnp.where(first_half, pltpu.roll(x, LANES - HEAD_DIM // 2, 1), pltpu.roll(x, HEAD_DIM // 2, 1))
        return x * cos + swapped * sin

    idx = _iota((lp, 1), 0).astype(F32)
    diff = (_iota((lp, lp), 0) - _iota((lp, lp), 1)).astype(F32)
    same_head = _iota((LANES, LANES), 0) // HEAD_DIM == _iota((LANES, LANES), 1) // HEAD_DIM
    e = _seg_ones()
    dmask, cross, kdec, cdec = [], [], [], []
    for hp in range(npair):
        lg = lg_ref[hp]
        dmask.append([jnp.where(diff >= 0.0, jnp.exp(lgh_ref[hp, hh:hh + 1, :] * jnp.maximum(diff, 0.0)), 0.0)
                      for hh in range(HEADS_PER_TILE)])
        cross.append(jnp.exp(lg * (idx + 1.0)))
        kdec.append(jnp.exp(lg * jnp.maximum(l_true - 1.0 - idx, 0.0)))
        cdec.append(jnp.exp(lg * float(l_true)))

    for s in range(bs):
        for hp in range(npair):
            col = lambda j: x_ref[s, :, (j * npair + hp) * LANES:(j * npair + hp + 1) * LANES]
            q = rope(pad(col(0)))
            k = rope(pad(col(1))) * (HEAD_DIM ** -0.5)
            v = pad(col(2))
            g = pad(col(3))
            vb = v.astype(BF16)
            kb = k.astype(BF16)
            outs = []
            for hh in range(HEADS_PER_TILE):
                qh = jnp.where(lane // HEAD_DIM == hh, q, 0.0).astype(BF16)
                sc = lax.dot_general(qh, kb, (((1,), (1,)), ((), ())), preferred_element_type=F32) * dmask[hp][hh]
                outs.append(jnp.dot(sc.astype(BF16), vb, preferred_element_type=F32))
            o = jnp.where(lane // HEAD_DIM == 0, outs[0], outs[1])
            r = r_scr[s * npair + hp]
            o = o + _bdot(q, r) * cross[hp]
            upd = jnp.dot((k * kdec[hp]).T.astype(BF16), vb, preferred_element_type=F32)
            r_new = r * cdec[hp] + jnp.where(same_head, upd, 0.0)
            r_scr[s * npair + hp] = r_new
            mean = _seg_sum(o, e) * (1.0 / HEAD_DIM)
            xc = o - mean
            var = _seg_sum(xc * xc, e) * (1.0 / HEAD_DIM)
            y = xc * lax.rsqrt(var + RET_GN_EPS) * gn_ref[:, hp * LANES:(hp + 1) * LANES] * (g * _sigmoid(g))
            o_ref[s, :, hp * LANES:(hp + 1) * LANES] = y[:l_in, :]

    @pl.when(c == pl.num_programs(1) - 1)
    def _():
        rt_ref[...] = r_scr[...]


def _retention(proj, cos, sin, lg, lgh, gn, r0, bs, l_in, l_true):
    nb, t, _ = proj.shape
    nc = t // l_in
    npair = RET_H // HEADS_PER_TILE
    st = pl.BlockSpec((bs * npair, LANES, LANES), lambda b, c: (b, 0, 0))
    tab = pl.BlockSpec((l_in, LANES), lambda b, c: (c, 0))
    return pl.pallas_call(
        functools.partial(_ret_kernel, bs=bs, l_in=l_in, l_true=l_true),
        grid=(nb // bs, nc),
        in_specs=[pl.BlockSpec((bs, l_in, 4 * RET_W), lambda b, c: (b, c, 0)), tab, tab,
                  _resident(lg.shape), _resident(lgh.shape), _resident(gn.shape), st],
        out_specs=[pl.BlockSpec((bs, l_in, RET_W), lambda b, c: (b, c, 0)), st],
        out_shape=[jax.ShapeDtypeStruct((nb, t, RET_W), F32), jax.ShapeDtypeStruct(r0.shape, F32)],
        scratch_shapes=[pltpu.VMEM((bs * npair, LANES, LANES), F32)],
        compiler_params=_cparams(("parallel", "arbitrary")),
        name="retention",
    )(proj, cos, sin, lg, lgh, gn, r0)


def _outproj_kernel(x_ref, fo_ref, y_ref, r_ref, k_ref, v_ref, g_ref, eo_ref, lw_ref, lb_ref, rk_ref,
                    wf_ref, wr_ref, we_ref, o_ref):
    e = _seg_ones()
    y = y_ref[...]
    mean = _seg_sum(y, e) * (1.0 / HEAD_DIM)
    yc = y - mean
    var = _seg_sum(yc * yc, e) * (1.0 / HEAD_DIM)
    yn = yc * lax.rsqrt(var + RWKV_GN_EPS) * lw_ref[...] + lb_ref[...]
    v = v_ref[...]
    bonus = _seg_sum(r_ref[...] * k_ref[...] * rk_ref[...], e) * v
    ro = (yn + bonus) * g_ref[...]
    o_ref[...] = (x_ref[...] + _bdot(fo_ref[...], wf_ref[...]) + _bdot(ro, wr_ref[...])
                  + _bdot(eo_ref[...], we_ref[...]))


def _outproj(x, fo, y, r, k, v, g, eo, lw, lb, rk, wf, wr, we, tm):
    n = x.shape[0]
    row = lambda wd: pl.BlockSpec((tm, wd), lambda i: (i, 0))
    vec = _resident((1, RWKV_W))
    return pl.pallas_call(
        _outproj_kernel,
        grid=(n // tm,),
        in_specs=[row(D_MODEL), row(FOX_W)] + [row(RWKV_W)] * 6 + [vec, vec, vec,
                  _resident(wf.shape), _resident(wr.shape), _resident(we.shape)],
        out_specs=row(D_MODEL),
        out_shape=jax.ShapeDtypeStruct((n, D_MODEL), F32),
        compiler_params=_cparams(("parallel",)),
        name="outproj",
    )(x, fo, y, r, k, v, g, eo, lw, lb, rk, wf, wr, we)


def _ffn_kernel(x_ref, g_ref, wg_ref, wu_ref, wd_ref, o_ref):
    x = x_ref[...]
    h = (x * lax.rsqrt(jnp.mean(x * x, -1, keepdims=True) + NORM_EPS) * g_ref[...]).astype(BF16)
    gate = jnp.dot(h, wg_ref[...], preferred_element_type=F32)
    up = jnp.dot(h, wu_ref[...], preferred_element_type=F32)
    act = gate * _sigmoid(gate) * up
    o_ref[...] = x + _bdot(act, wd_ref[...])


def _ffn(x, g, wg, wu, wd, tm):
    n = x.shape[0]
    row = pl.BlockSpec((tm, D_MODEL), lambda i: (i, 0))
    return pl.pallas_call(
        _ffn_kernel,
        grid=(n // tm,),
        in_specs=[row, _resident((1, D_MODEL)), _resident(wg.shape), _resident(wu.shape), _resident(wd.shape)],
        out_specs=row,
        out_shape=jax.ShapeDtypeStruct((n, D_MODEL), F32),
        compiler_params=_cparams(("parallel",)),
        name="ffn",
    )(x, g, wg, wu, wd)


def _rwkv_state_in(s):
    nb = s.shape[0]
    s = s.reshape(nb, RWKV_H // 2, 2, HEAD_DIM, HEAD_DIM).transpose(0, 1, 3, 2, 4)
    return s.reshape(nb * (RWKV_H // 2), HEAD_DIM, LANES)


def _rwkv_state_out(s, nb):
    s = s.reshape(nb, RWKV_H // 2, HEAD_DIM, 2, HEAD_DIM).transpose(0, 1, 3, 2, 4)
    return s.reshape(nb, RWKV_H, HEAD_DIM, HEAD_DIM)


def _ret_state_in(r):
    nb = r.shape[0]
    r = r.reshape(nb, RET_H // 2, 2, HEAD_DIM, HEAD_DIM)
    z = jnp.zeros_like(r[:, :, 0])
    top = jnp.concatenate([r[:, :, 0], z], axis=-1)
    bot = jnp.concatenate([z, r[:, :, 1]], axis=-1)
    return jnp.concatenate([top, bot], axis=-2).reshape(nb * (RET_H // 2), LANES, LANES)


def _ret_state_out(r, nb):
    r = r.reshape(nb, RET_H // 2, LANES, LANES)
    return jnp.stack([r[:, :, :HEAD_DIM, :HEAD_DIM], r[:, :, HEAD_DIM:, HEAD_DIM:]], axis=2).reshape(
        nb, RET_H, HEAD_DIM, HEAD_DIM)


def _layer_weights(l, ln_mix_g, w_in, fox_qn_g, fox_kn_g, fox_f_b, rwkv_mu, rwkv_w0, rwkv_w2, rwkv_a0, rwkv_a2,
                   rwkv_g2, rwkv_k_k, rwkv_k_a, rwkv_r_k, rwkv_lnx_w, rwkv_lnx_b, ret_gn_w, w_out, ln_ffn_g,
                   w_gate, w_up, w_down):
    wi = w_in[l].astype(BF16)
    o_f = 3 * FOX_W
    o_r = o_f + FOX_H
    o_e = o_r + RWKV_PROJ
    pad_rows = lambda w, off: jnp.zeros((LANES, RWKV_W), BF16).at[off:off + w.shape[0]].set(w.astype(BF16))
    wo = w_out[l].astype(BF16)
    return dict(
        ln_mix_g=ln_mix_g[l][None],
        w_in=[wi[:, 0:FOX_W], wi[:, FOX_W:2 * FOX_W], wi[:, 2 * FOX_W:o_f],
              jnp.pad(wi[:, o_f:o_r], ((0, 0), (0, LANES - FOX_H))), wi[:, o_r:o_e], wi[:, o_e:]],
        qg=jnp.tile(fox_qn_g[l], FOX_H)[None], kg=jnp.tile(fox_kn_g[l], FOX_H)[None],
        fb=jnp.pad(fox_f_b[l], (0, LANES - FOX_H))[None],
        mu=rwkv_mu[l][None], w0=rwkv_w0[l][None], a0=rwkv_a0[l][None],
        w2=pad_rows(rwkv_w2[l], 0), a2=pad_rows(rwkv_a2[l], RWKV_LORA_W),
        g2=pad_rows(rwkv_g2[l], RWKV_LORA_W + RWKV_LORA_A),
        k_k=rwkv_k_k[l][None], k_a=rwkv_k_a[l][None], r_k=rwkv_r_k[l].reshape(1, RWKV_W),
        lnx_w=rwkv_lnx_w[l][None], lnx_b=rwkv_lnx_b[l][None], gn=ret_gn_w[l][None],
        wo_f=wo[:FOX_W], wo_r=wo[FOX_W:FOX_W + RWKV_W], wo_e=wo[FOX_W + RWKV_W:],
        ln_ffn_g=ln_ffn_g[l][None],
        w_gate=w_gate[l].astype(BF16), w_up=w_up[l].astype(BF16), w_down=w_down[l].astype(BF16),
    )


def _mix_and_ffn(x2, fo, p_rwkv, shift, s0, proj_ret, ret_tabs, r0, lw, nb, t, tm, bblk, tblk, ret_bs, l_in, l_true):
    r, w, k, v, a, b, g = _rwkvprep(p_rwkv, shift, t, lw['mu'], lw['w0'], lw['w2'], lw['a0'], lw['a2'], lw['g2'],
                                    lw['k_k'], lw['k_a'], tm)
    seq = lambda z: z.reshape(nb, t, RWKV_W)
    y, s_t = _rwkvscan(seq(r), seq(w), seq(k), seq(v), seq(a), seq(b), s0, bblk, tblk)
    cos, sin, lg, lgh = ret_tabs
    eo, r_t = _retention(proj_ret, cos, sin, lg, lgh, lw['gn'], r0, ret_bs, l_in, l_true)
    eo = eo[:, :t].reshape(nb * t, RET_W)
    x2 = _outproj(x2, fo, y.reshape(nb * t, RWKV_W), r, k, v, g, eo, lw['lnx_w'], lw['lnx_b'], lw['r_k'],
                  lw['wo_f'], lw['wo_r'], lw['wo_e'], tm)
    x2 = _ffn(x2, lw['ln_ffn_g'], lw['w_gate'], lw['w_up'], lw['w_down'], tm)
    return x2, s_t, r_t


def kernel(x_prompt, x_sample, cache_fox_k, cache_fox_v, cache_fox_logf, state_rwkv, state_rwkv_shift, state_ret,
           page_table, ln_mix_g, w_in, fox_qn_g, fox_kn_g, fox_f_b, rwkv_mu, rwkv_w0, rwkv_w2, rwkv_a0, rwkv_a2,
           rwkv_g2, rwkv_k_k, rwkv_k_a, rwkv_r_k, rwkv_lnx_w, rwkv_lnx_b, ret_gn_w, w_out, ln_ffn_g, w_gate,
           w_up, w_down):
    nb, t, _ = x_prompt.shape
    db, ts, _ = x_sample.shape
    depth = w_in.shape[0]
    n_pages = page_table.shape[1]
    past_len = n_pages * PAGE_SIZE
    n_pool = cache_fox_k.shape[1]

    half = HEAD_DIM // 2
    inv = ROPE_BASE ** (-jnp.arange(half, dtype=F32) / half)
    inv_lanes = jnp.tile(inv, LANES // half)[None]
    log_gamma = jnp.log1p(-jnp.exp2(-5.0 - jnp.arange(RET_H, dtype=F32)))
    lg = jnp.repeat(log_gamma, HEAD_DIM).reshape(RET_H // 2, 1, LANES)
    lgh = jnp.broadcast_to(log_gamma.reshape(RET_H // 2, 2, 1), (RET_H // 2, 2, LANES))
    cos_p, sin_p = _rope_tables(inv_lanes, t, 0)
    cos_s, sin_s = _rope_tables(inv_lanes, SUBLANES, past_len)

    ckt = jnp.transpose(cache_fox_k, (0, 1, 3, 4, 2))
    cvt = jnp.transpose(cache_fox_v, (0, 1, 3, 4, 2))
    clt = jnp.swapaxes(cache_fox_logf, 2, 3)

    pad8 = lambda z: jnp.pad(z.reshape(db, ts, -1), ((0, 0), (0, SUBLANES - ts), (0, 0)))

    yp = x_prompt.reshape(nb * t, D_MODEL)
    ys = x_sample.reshape(db * ts, D_MODEL)
    outs = [[] for _ in range(12)]
    for l in range(depth):
        lw = _layer_weights(l, ln_mix_g, w_in, fox_qn_g, fox_kn_g, fox_f_b, rwkv_mu, rwkv_w0, rwkv_w2, rwkv_a0,
                            rwkv_a2, rwkv_g2, rwkv_k_k, rwkv_k_a, rwkv_r_k, rwkv_lnx_w, rwkv_lnx_b, ret_gn_w,
                            w_out, ln_ffn_g, w_gate, w_up, w_down)
        fq, fk, fv, fl, p_rwkv, p_ret = _inproj(yp, lw['ln_mix_g'], lw['w_in'], 256)
        kn, lf, qx, kx, vb = _foxprep_prompt(fq, fk, fl, fv, lw['qg'], lw['kg'], lw['fb'], nb, 256)
        fo = _foxattn(qx, kx, vb, nb, 512, 512)
        p3 = p_rwkv.reshape(nb, t, RWKV_PROJ)
        yp, s_t, r_t = _mix_and_ffn(
            yp, fo, p_rwkv, jnp.zeros((nb, RWKV_PROJ), F32), jnp.zeros((2 * nb, HEAD_DIM, LANES), F32),
            p_ret.reshape(nb, t, RET_PROJ), (cos_p, sin_p, lg, lgh), jnp.zeros((2 * nb, LANES, LANES), F32),
            lw, nb, t, 256, nb, 128, 4, RET_CHUNK, RET_CHUNK)
        outs[0].append(kn); outs[1].append(fv); outs[2].append(lf[:, :FOX_H])
        outs[3].append(_rwkv_state_out(s_t, nb)); outs[4].append(p3[:, -1]); outs[5].append(_ret_state_out(r_t, nb))
        fq, fk, fv, fl, p_rwkv, p_ret = _inproj(ys, lw['ln_mix_g'], lw['w_in'], 256)
        qn, kn, lf = _foxprep_sample(fq, fk, fl, lw['qg'], lw['kg'], lw['fb'], 256)
        fo = _foxsample(page_table, l, pad8(qn), pad8(kn), pad8(fv), pad8(lf), ckt, cvt, clt, ts)
        p3 = p_rwkv.reshape(db, ts, RWKV_PROJ)
        ys, s_t, r_t = _mix_and_ffn(
            ys, fo.reshape(db * ts, FOX_W), p_rwkv, state_rwkv_shift[l], _rwkv_state_in(state_rwkv[l]), pad8(p_ret),
            (cos_s, sin_s, lg, lgh), _ret_state_in(state_ret[l]), lw, db, ts, 256, 8, ts, 8, SUBLANES, ts)
        outs[6].append(kn); outs[7].append(fv); outs[8].append(lf[:, :FOX_H])
        outs[9].append(_rwkv_state_out(s_t, db)); outs[10].append(p3[:, -1]); outs[11].append(_ret_state_out(r_t, db))

    n_pp = nb * t // PAGE_SIZE
    st = lambda i: jnp.stack(outs[i])
    return (yp.reshape(nb, t, D_MODEL), ys.reshape(db, ts, D_MODEL),
            st(0).reshape(depth, n_pp, PAGE_SIZE, FOX_H, HEAD_DIM),
            st(1).reshape(depth, n_pp, PAGE_SIZE, FOX_H, HEAD_DIM),
            st(2).reshape(depth, n_pp, PAGE_SIZE, FOX_H),
            st(3), st(4), st(5),
            st(6).reshape(depth, db, ts, FOX_H, HEAD_DIM),
            st(7).reshape(depth, db, ts, FOX_H, HEAD_DIM),
            st(8).reshape(depth, db, ts, FOX_H),
            st(9), st(10), st(11))
```

```python
import functools

import jax
import jax.numpy as jnp
import numpy as np
from jax import lax
from jax.experimental import pallas as pl
from jax.experimental.pallas import tpu as pltpu

F32 = jnp.float32
BF16 = jnp.bfloat16

LANES = 128
SUBLANES = 8
VMEM_LIMIT = 56 * 1024 * 1024

D_MODEL = 1024
HEAD_DIM = 64
FOX_H = 8
RWKV_H = 4
RET_H = 4
FOX_W = FOX_H * HEAD_DIM
RWKV_W = RWKV_H * HEAD_DIM
RET_W = RET_H * HEAD_DIM
RWKV_LORA_W = 32
RWKV_LORA_A = 32
RWKV_LORA_G = 64
RWKV_PROJ = 3 * RWKV_W + RWKV_LORA_W + RWKV_LORA_A + RWKV_LORA_G
RET_PROJ = 4 * RET_W
D_FF = 2816
PAGE_SIZE = 128
RET_CHUNK = 128
ROPE_BASE = 10000.0
NORM_EPS = 1e-6
RWKV_GN_EPS = 64e-5
RET_GN_EPS = 1e-5
NEG_BIG = -1e30
HEADS_PER_TILE = LANES // HEAD_DIM


def _cparams(sem):
    return pltpu.CompilerParams(dimension_semantics=sem, vmem_limit_bytes=VMEM_LIMIT)


def _resident(shape):
    nd = len(shape)
    return pl.BlockSpec(shape, lambda *_: (0,) * nd, pipeline_mode=pl.Buffered(1))


def _bdot(a, b):
    return jnp.dot(a.astype(BF16), b.astype(BF16), preferred_element_type=F32)


def _bdot_nt(a, b):
    return lax.dot_general(a.astype(BF16), b.astype(BF16), (((1,), (1,)), ((), ())),
                           preferred_element_type=F32)


def _split2(x):
    hi = x.astype(BF16)
    lo = (x - hi.astype(F32)).astype(BF16)
    return hi, lo


def _split3(x):
    hi = x.astype(BF16)
    r = x - hi.astype(F32)
    mid = r.astype(BF16)
    lo = (r - mid.astype(F32)).astype(BF16)
    return hi, mid, lo


def _iota(shape, axis):
    return lax.broadcasted_iota(jnp.int32, shape, axis)


def _seg_ones():
    return (_iota((LANES, LANES), 0) // HEAD_DIM == _iota((LANES, LANES), 1) // HEAD_DIM).astype(BF16)


def _seg_sum(x, e):
    outs = []
    for c in range(x.shape[-1] // LANES):
        hi, lo = _split2(x[:, c * LANES:(c + 1) * LANES])
        outs.append(jnp.dot(hi, e, preferred_element_type=F32) + jnp.dot(lo, e, preferred_element_type=F32))
    return outs[0] if len(outs) == 1 else jnp.concatenate(outs, axis=-1)


def _exact_dot(x, m01, left):
    parts = _split3(x)
    if left:
        return sum(jnp.dot(m01, p, preferred_element_type=F32) for p in parts)
    return sum(jnp.dot(p, m01, preferred_element_type=F32) for p in parts)


def _sigmoid(x):
    return 1.0 / (1.0 + jnp.exp(-x))


def _softplus(x):
    return jnp.maximum(x, 0.0) + jnp.log1p(jnp.exp(-jnp.abs(x)))


def _inproj_kernel(x_ref, g_ref, wq_ref, wk_ref, wv_ref, wl_ref, wr_ref, we_ref,
                   oq_ref, ok_ref, ov_ref, ol_ref, or_ref, oe_ref):
    x = x_ref[...]
    h = x * lax.rsqrt(jnp.mean(x * x, -1, keepdims=True) + NORM_EPS) * g_ref[...]
    hb = h.astype(BF16)
    for w_ref, o_ref in ((wq_ref, oq_ref), (wk_ref, ok_ref), (wv_ref, ov_ref), (wl_ref, ol_ref),
                         (wr_ref, or_ref), (we_ref, oe_ref)):
        o_ref[...] = jnp.dot(hb, w_ref[...], preferred_element_type=F32)


def _inproj(x, g, ws, tm):
    n = x.shape[0]
    widths = [w.shape[1] for w in ws]
    row = lambda wd: pl.BlockSpec((tm, wd), lambda i: (i, 0))
    return pl.pallas_call(
        _inproj_kernel,
        grid=(n // tm,),
        in_specs=[row(D_MODEL), _resident((1, D_MODEL))] + [_resident(w.shape) for w in ws],
        out_specs=[row(wd) for wd in widths],
        out_shape=[jax.ShapeDtypeStruct((n, wd), F32) for wd in widths],
        compiler_params=_cparams(("parallel",)),
        name="inproj",
    )(x, g, *ws)


def _fox_norms(q_ref, k_ref, fl_ref, qg_ref, kg_ref, fb_ref):
    e = _seg_ones()

    def hnorm(x, g):
        ms = _seg_sum(x * x, e) * (1.0 / HEAD_DIM)
        return x * lax.rsqrt(ms + NORM_EPS) * g

    qn = hnorm(q_ref[...], qg_ref[...]) * (HEAD_DIM ** -0.5)
    kn = hnorm(k_ref[...], kg_ref[...])
    z = fl_ref[...] + fb_ref[...]
    lf = jnp.minimum(z, 0.0) - jnp.log1p(jnp.exp(-jnp.abs(z)))
    return qn, kn, lf


def _foxprep_sample_kernel(q_ref, k_ref, fl_ref, qg_ref, kg_ref, fb_ref, qn_ref, kn_ref, lf_ref):
    qn_ref[...], kn_ref[...], lf_ref[...] = _fox_norms(q_ref, k_ref, fl_ref, qg_ref, kg_ref, fb_ref)


def _foxprep_prompt_kernel(q_ref, k_ref, fl_ref, qg_ref, kg_ref, fb_ref, v_ref,
                           kt_ref, vt_ref, lft_ref, qx_ref, kx_ref, vb_ref, carry_ref, *, tm):
    qn, kn, lf = _fox_norms(q_ref, k_ref, fl_ref, qg_ref, kg_ref, fb_ref)
    v = v_ref[...]
    vb_ref[...] = v.astype(BF16)
    for pg in range(tm // PAGE_SIZE):
        rows = slice(pg * PAGE_SIZE, (pg + 1) * PAGE_SIZE)
        kt_ref[pg] = kn[rows, :].T
        vt_ref[pg] = v[rows, :].T
        lft_ref[pg] = lf[rows, :].T[:FOX_H, :]

    @pl.when(pl.program_id(1) == 0)
    def _():
        carry_ref[...] = jnp.zeros_like(carry_ref)

    tri = (_iota((tm, tm), 0) >= _iota((tm, tm), 1)).astype(BF16)
    c = _exact_dot(lf, tri, left=True) + carry_ref[...]
    carry_ref[...] = c[tm - 1:tm, :]

    lane = _iota((tm, LANES), 1)
    j = lane % HEAD_DIM
    for h in range(FOX_H):
        hp, hh = divmod(h, HEADS_PER_TILE)
        c_hi, c_mid, c_lo = (p.astype(F32) for p in _split3(jnp.broadcast_to(c[:, h:h + 1], (tm, LANES))))
        ext_q = jnp.where(j == 0, c_hi, jnp.where(j == 1, c_mid, jnp.where(j == 2, c_lo, jnp.where(j < 6, 1.0, 0.0))))
        ext_k = jnp.where(j < 3, 1.0, jnp.where(j == 3, -c_hi, jnp.where(j == 4, -c_mid, jnp.where(j == 5, -c_lo, 0.0))))
        own = lane // HEAD_DIM == hh
        pair = slice(hp * LANES, (hp + 1) * LANES)
        tile = slice(h * LANES, (h + 1) * LANES)
        qx_ref[:, tile] = jnp.where(own, qn[:, pair], ext_q).astype(BF16)
        kx_ref[:, tile] = jnp.where(own, kn[:, pair], ext_k).astype(BF16)


def _foxprep_sample(fq, fk, fl, qg, kg, fb, tm):
    n = fq.shape[0]
    row = lambda wd: pl.BlockSpec((tm, wd), lambda i: (i, 0))
    return pl.pallas_call(
        _foxprep_sample_kernel,
        grid=(n // tm,),
        in_specs=[row(FOX_W), row(FOX_W), row(LANES), _resident((1, FOX_W)), _resident((1, FOX_W)),
                  _resident((1, LANES))],
        out_specs=[row(FOX_W), row(FOX_W), row(LANES)],
        out_shape=[jax.ShapeDtypeStruct((n, FOX_W), F32), jax.ShapeDtypeStruct((n, FOX_W), F32),
                   jax.ShapeDtypeStruct((n, LANES), F32)],
        compiler_params=_cparams(("parallel",)),
        name="foxprep_sample",
    )(fq, fk, fl, qg, kg, fb)


def _foxprep_prompt(fq, fk, fl, fv, qg, kg, fb, nseq, tm):
    n = fq.shape[0]
    nt = n // nseq // tm
    ppt = tm // PAGE_SIZE
    n_pp = n // PAGE_SIZE
    row = lambda wd: pl.BlockSpec((tm, wd), lambda b, j: (b * nt + j, 0))
    pages = lambda r: pl.BlockSpec((ppt, r, PAGE_SIZE), lambda b, j: (b * nt + j, 0, 0))
    return pl.pallas_call(
        functools.partial(_foxprep_prompt_kernel, tm=tm),
        grid=(nseq, nt),
        in_specs=[row(FOX_W), row(FOX_W), row(LANES), _resident((1, FOX_W)), _resident((1, FOX_W)),
                  _resident((1, LANES)), row(FOX_W)],
        out_specs=[pages(FOX_W), pages(FOX_W), pages(FOX_H), row(FOX_H * LANES), row(FOX_H * LANES), row(FOX_W)],
        out_shape=[jax.ShapeDtypeStruct((n_pp, FOX_W, PAGE_SIZE), F32), jax.ShapeDtypeStruct((n_pp, FOX_W, PAGE_SIZE), F32),
                   jax.ShapeDtypeStruct((n_pp, FOX_H, PAGE_SIZE), F32),
                   jax.ShapeDtypeStruct((n, FOX_H * LANES), BF16), jax.ShapeDtypeStruct((n, FOX_H * LANES), BF16),
                   jax.ShapeDtypeStruct((n, FOX_W), BF16)],
        scratch_shapes=[pltpu.VMEM((1, LANES), F32)],
        compiler_params=_cparams(("parallel", "arbitrary")),
        name="foxprep_prompt",
    )(fq, fk, fl, qg, kg, fb, fv)


def _foxattn_kernel(qx_ref, kx_ref, vb_ref, o_ref, *, tq, tk):
    qi = pl.program_id(2)
    q = [qx_ref[:, hh * LANES:(hh + 1) * LANES] for hh in range(HEADS_PER_TILE)]
    n_full = (qi * tq) // tk

    def block(off, carry, diagonal):
        vb = vb_ref[pl.ds(off, tk), :]
        new = []
        for hh in range(HEADS_PER_TILE):
            m, l, acc = carry[hh]
            kb = kx_ref[pl.ds(off, tk), hh * LANES:(hh + 1) * LANES]
            s = lax.dot_general(q[hh], kb, (((1,), (1,)), ((), ())), preferred_element_type=F32)
            if diagonal:
                s = jnp.where(off + _iota((tq, tk), 1) <= qi * tq + _iota((tq, tk), 0), s, NEG_BIG)
            m_new = jnp.maximum(m, jnp.max(s, axis=-1, keepdims=True))
            alpha = jnp.exp(m - m_new)
            p = jnp.exp(s - m_new)
            l = alpha * l + jnp.sum(p, axis=-1, keepdims=True)
            acc = alpha * acc + jnp.dot(p.astype(BF16), vb, preferred_element_type=F32)
            new.append((m_new, l, acc))
        return tuple(new)

    init = tuple((jnp.full((tq, 1), NEG_BIG, F32), jnp.zeros((tq, 1), F32), jnp.zeros((tq, LANES), F32))
                 for _ in range(HEADS_PER_TILE))
    carry = lax.fori_loop(0, n_full, lambda j, c: block(pl.multiple_of(j * tk, tk), c, False), init)
    carry = block(pl.multiple_of(n_full * tk, tk), carry, True)
    outs = [acc / l for _, l, acc in carry]
    o_ref[...] = jnp.where(_iota((tq, LANES), 1) // HEAD_DIM == 0, outs[0], outs[1])


def _foxattn(qx, kx, vb, nseq, tq, tk):
    n = qx.shape[0]
    t = n // nseq
    nq = t // tq
    assert tk % tq == 0 and t % tk == 0
    npair = FOX_H // HEADS_PER_TILE
    pair_w = HEADS_PER_TILE * LANES
    return pl.pallas_call(
        functools.partial(_foxattn_kernel, tq=tq, tk=tk),
        grid=(nseq, npair, nq),
        in_specs=[
            pl.BlockSpec((tq, pair_w), lambda b, h, i: (b * nq + i, h)),
            pl.BlockSpec((t, pair_w), lambda b, h, i: (b, h)),
            pl.BlockSpec((t, LANES), lambda b, h, i: (b, h)),
        ],
        out_specs=pl.BlockSpec((tq, LANES), lambda b, h, i: (b * nq + i, h)),
        out_shape=jax.ShapeDtypeStruct((n, FOX_W), F32),
        compiler_params=_cparams(("parallel", "parallel", "arbitrary")),
        name="foxattn",
    )(qx, kx, vb)


def _foxsample_kernel(pt_ref, q_ref, kn_ref, vn_ref, lfn_ref, *rest, n_pages, t_new):
    del pt_ref
    kp = rest[:n_pages]
    vp = rest[n_pages:2 * n_pages]
    lp = rest[2 * n_pages:3 * n_pages]
    o_ref = rest[3 * n_pages]
    nrow = t_new * FOX_H

    rep = lambda x: jnp.concatenate([x] * t_new, axis=0)
    q = q_ref[...]
    hmask = _iota((FOX_H, FOX_W), 1) // HEAD_DIM == _iota((FOX_H, FOX_W), 0)
    qbd = jnp.concatenate(
        [jnp.where(hmask, jnp.broadcast_to(q[t:t + 1, :], (FOX_H, FOX_W)), 0.0) for t in range(t_new)],
        axis=0).astype(BF16)

    lfn = lfn_ref[...]
    diag8 = _iota((FOX_H, LANES), 1) == _iota((FOX_H, LANES), 0)
    cn_cols = []
    run = jnp.zeros((1, LANES), F32)
    for t in range(t_new):
        run = run + lfn[t:t + 1, :]
        cn_cols.append(jnp.sum(jnp.where(diag8, jnp.broadcast_to(run, (FOX_H, LANES)), 0.0), axis=-1, keepdims=True))
    cn = jnp.concatenate(cn_cols, axis=0)

    zeros_tail = jnp.zeros((PAGE_SIZE - SUBLANES, FOX_W), F32)
    key = _iota((nrow, PAGE_SIZE), 1)
    trow = _iota((nrow, PAGE_SIZE), 0) // FOX_H
    ckey = jnp.zeros((nrow, PAGE_SIZE), F32)
    for j in range(t_new):
        ckey = jnp.where(key == j, rep(cn_cols[j]), ckey)
    s_new = _bdot_nt(qbd, jnp.concatenate([kn_ref[...], zeros_tail], axis=0)) + (cn - ckey)
    s_new = jnp.where(key <= trow, s_new, NEG_BIG)

    lf_all = jnp.concatenate([lp[i][...] for i in range(n_pages)], axis=0)
    upper = (_iota((PAGE_SIZE, PAGE_SIZE), 0) <= _iota((PAGE_SIZE, PAGE_SIZE), 1)).astype(BF16)
    cp_all = _exact_dot(lf_all, upper, left=False)
    before = [jnp.zeros((FOX_H, 1), F32)]
    for i in range(n_pages):
        before.append(before[-1] + cp_all[i * FOX_H:(i + 1) * FOX_H, PAGE_SIZE - 1:PAGE_SIZE])
    cq_abs = rep(before[n_pages]) + cn
    s_past = []
    for i in range(n_pages):
        cp = rep(cp_all[i * FOX_H:(i + 1) * FOX_H, :] + before[i])
        s_past.append(_bdot(qbd, kp[i][...].reshape(FOX_W, PAGE_SIZE)) + (cq_abs - cp))

    m = jnp.max(s_new, axis=-1, keepdims=True)
    for s in s_past:
        m = jnp.maximum(m, jnp.max(s, axis=-1, keepdims=True))
    p = jnp.exp(s_new - m)
    l = jnp.sum(p, axis=-1, keepdims=True)
    o = _bdot(p, jnp.concatenate([vn_ref[...], zeros_tail], axis=0))
    for i in range(n_pages):
        p = jnp.exp(s_past[i] - m)
        l = l + jnp.sum(p, axis=-1, keepdims=True)
        o = o + _bdot_nt(p, vp[i][...].reshape(FOX_W, PAGE_SIZE))
    o = o / l
    omask = _iota((nrow, FOX_W), 1) // HEAD_DIM == _iota((nrow, FOX_W), 0) % FOX_H
    o_ref[...] = jnp.sum(jnp.where(omask, o, 0.0).reshape(t_new, FOX_H, FOX_W), axis=1)


def _foxsample(page_table, layer, qn, kn, vn, lfn, cache_kt, cache_vt, cache_lft, t_new):
    db, n_pages = page_table.shape
    pt = page_table.reshape(-1)
    new = lambda wd: pl.BlockSpec((None, SUBLANES, wd), lambda b, pt: (b, 0, 0))

    def page(shape, i):
        nd = len(shape)
        return pl.BlockSpec((None, None) + shape, lambda b, pt: (layer, pt[b * n_pages + i]) + (0,) * nd)

    in_specs = [new(FOX_W), new(FOX_W), new(FOX_W), new(LANES)]
    in_specs += [page((FOX_H, HEAD_DIM, PAGE_SIZE), i) for i in range(n_pages)]
    in_specs += [page((FOX_H, HEAD_DIM, PAGE_SIZE), i) for i in range(n_pages)]
    in_specs += [page((FOX_H, PAGE_SIZE), i) for i in range(n_pages)]
    return pl.pallas_call(
        functools.partial(_foxsample_kernel, n_pages=n_pages, t_new=t_new),
        grid_spec=pltpu.PrefetchScalarGridSpec(
            num_scalar_prefetch=1,
            grid=(db,),
            in_specs=in_specs,
            out_specs=pl.BlockSpec((None, t_new, FOX_W), lambda b, pt: (b, 0, 0)),
        ),
        out_shape=jax.ShapeDtypeStruct((db, t_new, FOX_W), F32),
        compiler_params=_cparams(("arbitrary",)),
        name="foxsample",
    )(pt, qn, kn, vn, lfn, *([cache_kt] * n_pages), *([cache_vt] * n_pages), *([cache_lft] * n_pages))


def _rwkvprep_kernel(p_ref, init_ref, mu_ref, w0_ref, w2_ref, a0_ref, a2_ref, g2_ref, kk_ref, ka_ref,
                     r_out, w_out, k_out, v_out, a_out, b_out, g_out, *carry, tm, period):
    e = _seg_ones()
    p = p_ref[...]
    rolled = pltpu.roll(p, 1, 0)
    row = _iota((tm, RWKV_PROJ), 0)
    if period >= tm:
        carry_ref, = carry

        @pl.when(pl.program_id(1) == 0)
        def _():
            carry_ref[...] = init_ref[...]

        pp = jnp.where(row == 0, carry_ref[...], rolled)
        carry_ref[...] = p[tm - 1:tm, :]
    else:
        pp = jnp.where(row % period == 0, init_ref[...], rolled)
    xs = p + (pp - p) * mu_ref[...]
    r = xs[:, 0:RWKV_W]
    k = xs[:, RWKV_W:2 * RWKV_W]
    v = xs[:, 2 * RWKV_W:3 * RWKV_W]
    lo = xs[:, 3 * RWKV_W:]
    w = -_softplus(-(w0_ref[...] + _bdot(jnp.tanh(lo), w2_ref[...]))) - 0.5
    a = _sigmoid(a0_ref[...] + _bdot(lo, a2_ref[...]))
    kk = k * kk_ref[...]
    nrm = jnp.sqrt(_seg_sum(kk * kk, e))
    kk = kk / jnp.maximum(nrm, 1e-12)
    r_out[...] = r
    w_out[...] = jnp.exp(-jnp.exp(w))
    k_out[...] = k * (1.0 + (a - 1.0) * ka_ref[...])
    v_out[...] = v
    a_out[...] = -kk
    b_out[...] = kk * a
    g_out[...] = _bdot(_sigmoid(lo), g2_ref[...])


def _rwkvprep(p, shift, period, mu, w0, w2p, a0, a2p, g2p, k_k, k_a, tm):
    n = p.shape[0]
    nseq = n // period
    vec = _resident((1, RWKV_W))
    lora = _resident((LANES, RWKV_W))
    if period >= tm:
        nt = period // tm
        grid = (nseq, nt)
        row = lambda wd: pl.BlockSpec((tm, wd), lambda b, j: (b * nt + j, 0))
        init = shift[:, None, :]
        init_spec = pl.BlockSpec((None, 1, RWKV_PROJ), lambda b, j: (b, 0, 0))
        scratch = [pltpu.VMEM((1, RWKV_PROJ), F32)]
        sem = ("parallel", "arbitrary")
    else:
        grid = (n // tm,)
        row = lambda wd: pl.BlockSpec((tm, wd), lambda i: (i, 0))
        init = jnp.repeat(shift, period, axis=0)
        init_spec = row(RWKV_PROJ)
        scratch = []
        sem = ("parallel",)
    return pl.pallas_call(
        functools.partial(_rwkvprep_kernel, tm=tm, period=period),
        grid=grid,
        in_specs=[row(RWKV_PROJ), init_spec, _resident((1, RWKV_PROJ)), vec, lora, vec, lora, lora, vec, vec],
        out_specs=[row(RWKV_W)] * 7,
        out_shape=[jax.ShapeDtypeStruct((n, RWKV_W), F32)] * 7,
        scratch_shapes=scratch,
        compiler_params=_cparams(sem),
        name="rwkvprep",
    )(p, init, mu, w0, w2p, a0, a2p, g2p, k_k, k_a)


def _rwkvscan_kernel(r_ref, w_ref, k_ref, v_ref, a_ref, b_ref, s0_ref, y_ref, st_ref, s_scr, *, bblk, tblk):
    tb = pl.program_id(1)
    ng = HEADS_PER_TILE * bblk

    @pl.when(tb == 0)
    def _():
        s_scr[...] = s0_ref[...]

    e = _seg_ones()
    diag = (_iota((HEAD_DIM, LANES), 1) % HEAD_DIM == _iota((HEAD_DIM, LANES), 0)).astype(BF16)[None]
    half = (_iota((SUBLANES, LANES), 1) // HEAD_DIM == _iota((SUBLANES, LANES), 0)).astype(BF16)
    first = _iota((1, LANES), 1) < HEAD_DIM

    sub = min(SUBLANES, tblk)

    def chunk(c, carry):
        off = pl.multiple_of(c * sub, sub)
        tiles = [[ref[b, pl.ds(off, sub), :] for b in range(bblk)] for ref in (r_ref, w_ref, k_ref, v_ref, a_ref, b_ref)]

        def rows(kind, i):
            return jnp.concatenate(
                [tiles[kind][b][i:i + 1, hp * LANES:(hp + 1) * LANES][None] for b in range(bblk) for hp in range(2)],
                axis=0)

        def seg(x):
            return jnp.dot(x.reshape(ng * HEAD_DIM, LANES).astype(BF16), e,
                           preferred_element_type=F32).reshape(ng, HEAD_DIM, LANES)

        s = s_scr[...]
        ytiles = [[] for _ in range(bblk)]
        for i in range(sub):
            sa = seg(s * rows(4, i))
            vdiag = (rows(3, i).astype(BF16) * diag).reshape(ng * HEAD_DIM, LANES)
            vcol = jnp.dot(vdiag, e, preferred_element_type=F32).reshape(ng, HEAD_DIM, LANES)
            s = s * rows(1, i) + sa * rows(5, i) + vcol * rows(2, i)
            sr = (s * rows(0, i)).astype(BF16)
            for b in range(bblk):
                out = lax.dot_general(half, sr[2 * b:2 * b + 2].reshape(2 * HEAD_DIM, LANES),
                                      (((1,), (1,)), ((), ())), preferred_element_type=F32)
                h0, h1 = out[0:1, :], out[1:2, :]
                ytiles[b].append(jnp.concatenate(
                    [jnp.where(first, h0, pltpu.roll(h1, HEAD_DIM, 1)),
                     jnp.where(first, pltpu.roll(h0, HEAD_DIM, 1), h1)], axis=1))
        s_scr[...] = s
        for b in range(bblk):
            y_ref[b, pl.ds(off, sub), :] = jnp.concatenate(ytiles[b], axis=0)
        return carry

    lax.fori_loop(0, tblk // sub, chunk, 0)

    @pl.when(tb == pl.num_programs(1) - 1)
    def _():
        st_ref[...] = s_scr[...]


def _rwkvscan(r, w, k, v, a, b, s0, bblk, tblk):
    nb, t, _ = r.shape
    ng = HEADS_PER_TILE * bblk
    seq = pl.BlockSpec((bblk, tblk, RWKV_W), lambda i, j: (i, j, 0))
    st = pl.BlockSpec((ng, HEAD_DIM, LANES), lambda i, j: (i, 0, 0))
    return pl.pallas_call(
        functools.partial(_rwkvscan_kernel, bblk=bblk, tblk=tblk),
        grid=(nb // bblk, t // tblk),
        in_specs=[seq] * 6 + [st],
        out_specs=[seq, st],
        out_shape=[jax.ShapeDtypeStruct((nb, t, RWKV_W), F32), jax.ShapeDtypeStruct(s0.shape, F32)],
        scratch_shapes=[pltpu.VMEM((ng, HEAD_DIM, LANES), F32)],
        compiler_params=_cparams(("parallel", "arbitrary")),
        name="rwkvscan",
    )(r, w, k, v, a, b, s0)


def _rope_kernel(inv_ref, cos_ref, sin_ref, *, pos0):
    t = cos_ref.shape[0]
    pos = (pos0 + _iota((t, LANES), 0)).astype(F32)
    ang = pos * inv_ref[...]
    first_half = _iota((t, LANES), 1) % HEAD_DIM < HEAD_DIM // 2
    cos_ref[...] = jnp.cos(ang)
    sin_ref[...] = jnp.where(first_half, -jnp.sin(ang), jnp.sin(ang))


def _rope_tables(inv_lanes, t, pos0):
    return pl.pallas_call(
        functools.partial(_rope_kernel, pos0=pos0),
        out_shape=[jax.ShapeDtypeStruct((t, LANES), F32)] * 2,
        name="rope",
    )(inv_lanes)


def _ret_kernel(x_ref, cos_ref, sin_ref, lg_ref, lgh_ref, gn_ref, r0_ref, o_ref, rt_ref, r_scr,
                *, bs, l_in, l_true):
    lp = RET_CHUNK
    npair = RET_H // HEADS_PER_TILE
    c = pl.program_id(1)

    @pl.when(c == 0)
    def _():
        r_scr[...] = r0_ref[...]

    def pad(x):
        if l_in == lp:
            return x
        return jnp.concatenate([x, jnp.zeros((lp - l_in, x.shape[1]), F32)], axis=0)

    lane = _iota((lp, LANES), 1)
    first_half = lane % HEAD_DIM < HEAD_DIM // 2
    cos = pad(cos_ref[...])
    sin = pad(sin_ref[...])

    def rope(x):
        swapped = jnp.where(first_half, pltpu.roll(x, LANES - HEAD_DIM // 2, 1), pltpu.roll(x, HEAD_DIM // 2, 1))
        return x * cos + swapped * sin

    idx = _iota((lp, 1), 0).astype(F32)
    diff = (_iota((lp, lp), 0) - _iota((lp, lp), 1)).astype(F32)
    same_head = _iota((LANES, LANES), 0) // HEAD_DIM == _iota((LANES, LANES), 1) // HEAD_DIM
    e = _seg_ones()
    dmask, cross, kdec, cdec = [], [], [], []
    for hp in range(npair):
        lg = lg_ref[hp]
        dmask.append([jnp.where(diff >= 0.0, jnp.exp(lgh_ref[hp, hh:hh + 1, :] * jnp.maximum(diff, 0.0)), 0.0)
                      for hh in range(HEADS_PER_TILE)])
        cross.append(jnp.exp(lg * (idx + 1.0)))
        kdec.append(jnp.exp(lg * jnp.maximum(l_true - 1.0 - idx, 0.0)))
        cdec.append(jnp.exp(lg * float(l_true)))

    for s in range(bs):
        for hp in range(npair):
            col = lambda j: x_ref[s, :, (j * npair + hp) * LANES:(j * npair + hp + 1) * LANES]
            q = rope(pad(col(0)))
            k = rope(pad(col(1))) * (HEAD_DIM ** -0.5)
            v = pad(col(2))
            g = pad(col(3))
            vb = v.astype(BF16)
            kb = k.astype(BF16)
            outs = []
            for hh in range(HEADS_PER_TILE):
                qh = jnp.where(lane // HEAD_DIM == hh, q, 0.0).astype(BF16)
                sc = lax.dot_general(qh, kb, (((1,), (1,)), ((), ())), preferred_element_type=F32) * dmask[hp][hh]
                outs.append(jnp.dot(sc.astype(BF16), vb, preferred_element_type=F32))
            o = jnp.where(lane // HEAD_DIM == 0, outs[0], outs[1])
            r = r_scr[s * npair + hp]
            o = o + _bdot(q, r) * cross[hp]
            upd = jnp.dot((k * kdec[hp]).T.astype(BF16), vb, preferred_element_type=F32)
            r_new = r * cdec[hp] + jnp.where(same_head, upd, 0.0)
            r_scr[s * npair + hp] = r_new
            mean = _seg_sum(o, e) * (1.0 / HEAD_DIM)
            xc = o - mean
            var = _seg_sum(xc * xc, e) * (1.0 / HEAD_DIM)
            y = xc * lax.rsqrt(var + RET_GN_EPS) * gn_ref[:, hp * LANES:(hp + 1) * LANES] * (g * _sigmoid(g))
            o_ref[s, :, hp * LANES:(hp + 1) * LANES] = y[:l_in, :]

    @pl.when(c == pl.num_programs(1) - 1)
    def _():
        rt_ref[...] = r_scr[...]


def _retention(proj, cos, sin, lg, lgh, gn, r0, bs, l_in, l_true):
    nb, t, _ = proj.shape
    nc = t // l_in
    npair = RET_H // HEADS_PER_TILE
    st = pl.BlockSpec((bs * npair, LANES, LANES), lambda b, c: (b, 0, 0))
    tab = pl.BlockSpec((l_in, LANES), lambda b, c: (c, 0))
    return pl.pallas_call(
        functools.partial(_ret_kernel, bs=bs, l_in=l_in, l_true=l_true),
        grid=(nb // bs, nc),
        in_specs=[pl.BlockSpec((bs, l_in, 4 * RET_W), lambda b, c: (b, c, 0)), tab, tab,
                  _resident(lg.shape), _resident(lgh.shape), _resident(gn.shape), st],
        out_specs=[pl.BlockSpec((bs, l_in, RET_W), lambda b, c: (b, c, 0)), st],
        out_shape=[jax.ShapeDtypeStruct((nb, t, RET_W), F32), jax.ShapeDtypeStruct(r0.shape, F32)],
        scratch_shapes=[pltpu.VMEM((bs * npair, LANES, LANES), F32)],
        compiler_params=_cparams(("parallel", "arbitrary")),
        name="retention",
    )(proj, cos, sin, lg, lgh, gn, r0)


def _outproj_kernel(x_ref, fo_ref, y_ref, r_ref, k_ref, v_ref, g_ref, eo_ref, lw_ref, lb_ref, rk_ref,
                    wf_ref, wr_ref, we_ref, o_ref):
    e = _seg_ones()
    y = y_ref[...]
    mean = _seg_sum(y, e) * (1.0 / HEAD_DIM)
    yc = y - mean
    var = _seg_sum(yc * yc, e) * (1.0 / HEAD_DIM)
    yn = yc * lax.rsqrt(var + RWKV_GN_EPS) * lw_ref[...] + lb_ref[...]
    v = v_ref[...]
    bonus = _seg_sum(r_ref[...] * k_ref[...] * rk_ref[...], e) * v
    ro = (yn + bonus) * g_ref[...]
    o_ref[...] = (x_ref[...] + _bdot(fo_ref[...], wf_ref[...]) + _bdot(ro, wr_ref[...])
                  + _bdot(eo_ref[...], we_ref[...]))


def _outproj(x, fo, y, r, k, v, g, eo, lw, lb, rk, wf, wr, we, tm):
    n = x.shape[0]
    row = lambda wd: pl.BlockSpec((tm, wd), lambda i: (i, 0))
    vec = _resident((1, RWKV_W))
    return pl.pallas_call(
        _outproj_kernel,
        grid=(n // tm,),
        in_specs=[row(D_MODEL), row(FOX_W)] + [row(RWKV_W)] * 6 + [vec, vec, vec,
                  _resident(wf.shape), _resident(wr.shape), _resident(we.shape)],
        out_specs=row(D_MODEL),
        out_shape=jax.ShapeDtypeStruct((n, D_MODEL), F32),
        compiler_params=_cparams(("parallel",)),
        name="outproj",
    )(x, fo, y, r, k, v, g, eo, lw, lb, rk, wf, wr, we)


def _ffn_kernel(x_ref, g_ref, wg_ref, wu_ref, wd_ref, o_ref):
    x = x_ref[...]
    h = (x * lax.rsqrt(jnp.mean(x * x, -1, keepdims=True) + NORM_EPS) * g_ref[...]).astype(BF16)
    gate = jnp.dot(h, wg_ref[...], preferred_element_type=F32)
    up = jnp.dot(h, wu_ref[...], preferred_element_type=F32)
    act = gate * _sigmoid(gate) * up
    o_ref[...] = x + _bdot(act, wd_ref[...])


def _ffn(x, g, wg, wu, wd, tm):
    n = x.shape[0]
    row = pl.BlockSpec((tm, D_MODEL), lambda i: (i, 0))
    return pl.pallas_call(
        _ffn_kernel,
        grid=(n // tm,),
        in_specs=[row, _resident((1, D_MODEL)), _resident(wg.shape), _resident(wu.shape), _resident(wd.shape)],
        out_specs=row,
        out_shape=jax.ShapeDtypeStruct((n, D_MODEL), F32),
        compiler_params=_cparams(("parallel",)),
        name="ffn",
    )(x, g, wg, wu, wd)


def _rwkv_state_in(s):
    nb = s.shape[0]
    s = s.reshape(nb, RWKV_H // 2, 2, HEAD_DIM, HEAD_DIM).transpose(0, 1, 3, 2, 4)
    return s.reshape(nb * (RWKV_H // 2), HEAD_DIM, LANES)


def _rwkv_state_out(s, nb):
    s = s.reshape(nb, RWKV_H // 2, HEAD_DIM, 2, HEAD_DIM).transpose(0, 1, 3, 2, 4)
    return s.reshape(nb, RWKV_H, HEAD_DIM, HEAD_DIM)


def _ret_state_in(r):
    nb = r.shape[0]
    r = r.reshape(nb, RET_H // 2, 2, HEAD_DIM, HEAD_DIM)
    z = jnp.zeros_like(r[:, :, 0])
    top = jnp.concatenate([r[:, :, 0], z], axis=-1)
    bot = jnp.concatenate([z, r[:, :, 1]], axis=-1)
    return jnp.concatenate([top, bot], axis=-2).reshape(nb * (RET_H // 2), LANES, LANES)


def _ret_state_out(r, nb):
    r = r.reshape(nb, RET_H // 2, LANES, LANES)
    return jnp.stack([r[:, :, :HEAD_DIM, :HEAD_DIM], r[:, :, HEAD_DIM:, HEAD_DIM:]], axis=2).reshape(
        nb, RET_H, HEAD_DIM, HEAD_DIM)


def _layer_weights(l, ln_mix_g, w_in, fox_qn_g, fox_kn_g, fox_f_b, rwkv_mu, rwkv_w0, rwkv_w2, rwkv_a0, rwkv_a2,
                   rwkv_g2, rwkv_k_k, rwkv_k_a, rwkv_r_k, rwkv_lnx_w, rwkv_lnx_b, ret_gn_w, w_out, ln_ffn_g,
                   w_gate, w_up, w_down):
    wi = w_in[l].astype(BF16)
    o_f = 3 * FOX_W
    o_r = o_f + FOX_H
    o_e = o_r + RWKV_PROJ
    pad_rows = lambda w, off: jnp.zeros((LANES, RWKV_W), BF16).at[off:off + w.shape[0]].set(w.astype(BF16))
    wo = w_out[l].astype(BF16)
    return dict(
        ln_mix_g=ln_mix_g[l][None],
        w_in=[wi[:, 0:FOX_W], wi[:, FOX_W:2 * FOX_W], wi[:, 2 * FOX_W:o_f],
              jnp.pad(wi[:, o_f:o_r], ((0, 0), (0, LANES - FOX_H))), wi[:, o_r:o_e], wi[:, o_e:]],
        qg=jnp.tile(fox_qn_g[l], FOX_H)[None], kg=jnp.tile(fox_kn_g[l], FOX_H)[None],
        fb=jnp.pad(fox_f_b[l], (0, LANES - FOX_H))[None],
        mu=rwkv_mu[l][None], w0=rwkv_w0[l][None], a0=rwkv_a0[l][None],
        w2=pad_rows(rwkv_w2[l], 0), a2=pad_rows(rwkv_a2[l], RWKV_LORA_W),
        g2=pad_rows(rwkv_g2[l], RWKV_LORA_W + RWKV_LORA_A),
        k_k=rwkv_k_k[l][None], k_a=rwkv_k_a[l][None], r_k=rwkv_r_k[l].reshape(1, RWKV_W),
        lnx_w=rwkv_lnx_w[l][None], lnx_b=rwkv_lnx_b[l][None], gn=ret_gn_w[l][None],
        wo_f=wo[:FOX_W], wo_r=wo[FOX_W:FOX_W + RWKV_W], wo_e=wo[FOX_W + RWKV_W:],
        ln_ffn_g=ln_ffn_g[l][None],
        w_gate=w_gate[l].astype(BF16), w_up=w_up[l].astype(BF16), w_down=w_down[l].astype(BF16),
    )


def _mix_and_ffn(x2, fo, p_rwkv, shift, s0, proj_ret, ret_tabs, r0, lw, nb, t, tm, bblk, tblk, ret_bs, l_in, l_true):
    r, w, k, v, a, b, g = _rwkvprep(p_rwkv, shift, t, lw['mu'], lw['w0'], lw['w2'], lw['a0'], lw['a2'], lw['g2'],
                                    lw['k_k'], lw['k_a'], tm)
    seq = lambda z: z.reshape(nb, t, RWKV_W)
    y, s_t = _rwkvscan(seq(r), seq(w), seq(k), seq(v), seq(a), seq(b), s0, bblk, tblk)
    cos, sin, lg, lgh = ret_tabs
    eo, r_t = _retention(proj_ret, cos, sin, lg, lgh, lw['gn'], r0, ret_bs, l_in, l_true)
    eo = eo[:, :t].reshape(nb * t, RET_W)
    x2 = _outproj(x2, fo, y.reshape(nb * t, RWKV_W), r, k, v, g, eo, lw['lnx_w'], lw['lnx_b'], lw['r_k'],
                  lw['wo_f'], lw['wo_r'], lw['wo_e'], tm)
    x2 = _ffn(x2, lw['ln_ffn_g'], lw['w_gate'], lw['w_up'], lw['w_down'], tm)
    return x2, s_t, r_t


def kernel(x_prompt, x_sample, cache_fox_k, cache_fox_v, cache_fox_logf, state_rwkv, state_rwkv_shift, state_ret,
           page_table, ln_mix_g, w_in, fox_qn_g, fox_kn_g, fox_f_b, rwkv_mu, rwkv_w0, rwkv_w2, rwkv_a0, rwkv_a2,
           rwkv_g2, rwkv_k_k, rwkv_k_a, rwkv_r_k, rwkv_lnx_w, rwkv_lnx_b, ret_gn_w, w_out, ln_ffn_g, w_gate,
           w_up, w_down):
    nb, t, _ = x_prompt.shape
    db, ts, _ = x_sample.shape
    depth = w_in.shape[0]
    n_pages = page_table.shape[1]
    past_len = n_pages * PAGE_SIZE
    n_pool = cache_fox_k.shape[1]

    half = HEAD_DIM // 2
    inv = ROPE_BASE ** (-jnp.arange(half, dtype=F32) / half)
    inv_lanes = jnp.tile(inv, LANES // half)[None]
    log_gamma = jnp.log1p(-jnp.exp2(-5.0 - jnp.arange(RET_H, dtype=F32)))
    lg = jnp.repeat(log_gamma, HEAD_DIM).reshape(RET_H // 2, 1, LANES)
    lgh = jnp.broadcast_to(log_gamma.reshape(RET_H // 2, 2, 1), (RET_H // 2, 2, LANES))
    cos_p, sin_p = _rope_tables(inv_lanes, t, 0)
    cos_s, sin_s = _rope_tables(inv_lanes, SUBLANES, past_len)

    ckt = jnp.transpose(cache_fox_k, (0, 1, 3, 4, 2))
    cvt = jnp.transpose(cache_fox_v, (0, 1, 3, 4, 2))
    clt = jnp.swapaxes(cache_fox_logf, 2, 3)

    pad8 = lambda z: jnp.pad(z.reshape(db, ts, -1), ((0, 0), (0, SUBLANES - ts), (0, 0)))

    yp = x_prompt.reshape(nb * t, D_MODEL)
    ys = x_sample.reshape(db * ts, D_MODEL)
    outs = [[] for _ in range(12)]
    for l in range(depth):
        lw = _layer_weights(l, ln_mix_g, w_in, fox_qn_g, fox_kn_g, fox_f_b, rwkv_mu, rwkv_w0, rwkv_w2, rwkv_a0,
                            rwkv_a2, rwkv_g2, rwkv_k_k, rwkv_k_a, rwkv_r_k, rwkv_lnx_w, rwkv_lnx_b, ret_gn_w,
                            w_out, ln_ffn_g, w_gate, w_up, w_down)
        fq, fk, fv, fl, p_rwkv, p_ret = _inproj(yp, lw['ln_mix_g'], lw['w_in'], 512)
        kt, vt, lft, qx, kx, vb = _foxprep_prompt(fq, fk, fl, fv, lw['qg'], lw['kg'], lw['fb'], nb, 256)
        fo = _foxattn(qx, kx, vb, nb, 512, 512)
        p3 = p_rwkv.reshape(nb, t, RWKV_PROJ)
        yp, s_t, r_t = _mix_and_ffn(
            yp, fo, p_rwkv, jnp.zeros((nb, RWKV_PROJ), F32), jnp.zeros((2 * nb, HEAD_DIM, LANES), F32),
            p_ret.reshape(nb, t, RET_PROJ), (cos_p, sin_p, lg, lgh), jnp.zeros((2 * nb, LANES, LANES), F32),
            lw, nb, t, 512, nb, 128, 4, RET_CHUNK, RET_CHUNK)
        outs[0].append(kt); outs[1].append(vt); outs[2].append(lft)
        outs[3].append(_rwkv_state_out(s_t, nb)); outs[4].append(p3[:, -1]); outs[5].append(_ret_state_out(r_t, nb))
        fq, fk, fv, fl, p_rwkv, p_ret = _inproj(ys, lw['ln_mix_g'], lw['w_in'], 256)
        qn, kn, lf = _foxprep_sample(fq, fk, fl, lw['qg'], lw['kg'], lw['fb'], 256)
        fo = _foxsample(page_table, l, pad8(qn), pad8(kn), pad8(fv), pad8(lf), ckt, cvt, clt, ts)
        p3 = p_rwkv.reshape(db, ts, RWKV_PROJ)
        ys, s_t, r_t = _mix_and_ffn(
            ys, fo.reshape(db * ts, FOX_W), p_rwkv, state_rwkv_shift[l], _rwkv_state_in(state_rwkv[l]), pad8(p_ret),
            (cos_s, sin_s, lg, lgh), _ret_state_in(state_ret[l]), lw, db, ts, 256, 8, ts, 8, SUBLANES, ts)
        outs[6].append(kn); outs[7].append(fv); outs[8].append(lf[:, :FOX_H])
        outs[9].append(_rwkv_state_out(s_t, db)); outs[10].append(p3[:, -1]); outs[11].append(_ret_state_out(r_t, db))

    n_pp = nb * t // PAGE_SIZE
    st = lambda i: jnp.stack(outs[i])
    page_rows = lambda z: jnp.transpose(z.reshape(depth, n_pp, FOX_H, HEAD_DIM, PAGE_SIZE), (0, 1, 4, 2, 3))
    return (yp.reshape(nb, t, D_MODEL), ys.reshape(db, ts, D_MODEL),
            page_rows(st(0)), page_rows(st(1)), jnp.swapaxes(st(2), 2, 3),
            st(3), st(4), st(5),
            st(6).reshape(depth, db, ts, FOX_H, HEAD_DIM),
            st(7).reshape(depth, db, ts, FOX_H, HEAD_DIM),
            st(8).reshape(depth, db, ts, FOX_H),
            st(9), st(10), st(11))
```

```python
import functools

import jax
import jax.numpy as jnp
import numpy as np
from jax import lax
from jax.experimental import pallas as pl
from jax.experimental.pallas import tpu as pltpu

F32 = jnp.float32
BF16 = jnp.bfloat16

LANES = 128
SUBLANES = 8
VMEM_LIMIT = 56 * 1024 * 1024

D_MODEL = 1024
HEAD_DIM = 64
FOX_H = 8
RWKV_H = 4
RET_H = 4
FOX_W = FOX_H * HEAD_DIM
RWKV_W = RWKV_H * HEAD_DIM
RET_W = RET_H * HEAD_DIM
RWKV_LORA_W = 32
RWKV_LORA_A = 32
RWKV_LORA_G = 64
RWKV_PROJ = 3 * RWKV_W + RWKV_LORA_W + RWKV_LORA_A + RWKV_LORA_G
RET_PROJ = 4 * RET_W
D_FF = 2816
PAGE_SIZE = 128
RET_CHUNK = 128
ROPE_BASE = 10000.0
NORM_EPS = 1e-6
RWKV_GN_EPS = 64e-5
RET_GN_EPS = 1e-5
NEG_BIG = -1e30
HEADS_PER_TILE = LANES // HEAD_DIM


def _cparams(sem):
    return pltpu.CompilerParams(dimension_semantics=sem, vmem_limit_bytes=VMEM_LIMIT)


def _resident(shape):
    nd = len(shape)
    return pl.BlockSpec(shape, lambda *_: (0,) * nd, pipeline_mode=pl.Buffered(1))


def _bdot(a, b):
    return jnp.dot(a.astype(BF16), b.astype(BF16), preferred_element_type=F32)


def _bdot_nt(a, b):
    return lax.dot_general(a.astype(BF16), b.astype(BF16), (((1,), (1,)), ((), ())),
                           preferred_element_type=F32)


def _split2(x):
    hi = x.astype(BF16)
    lo = (x - hi.astype(F32)).astype(BF16)
    return hi, lo


def _split3(x):
    hi = x.astype(BF16)
    r = x - hi.astype(F32)
    mid = r.astype(BF16)
    lo = (r - mid.astype(F32)).astype(BF16)
    return hi, mid, lo


def _iota(shape, axis):
    return lax.broadcasted_iota(jnp.int32, shape, axis)


def _seg_ones():
    return (_iota((LANES, LANES), 0) // HEAD_DIM == _iota((LANES, LANES), 1) // HEAD_DIM).astype(BF16)


def _seg_sum(x, e):
    outs = []
    for c in range(x.shape[-1] // LANES):
        hi, lo = _split2(x[:, c * LANES:(c + 1) * LANES])
        outs.append(jnp.dot(hi, e, preferred_element_type=F32) + jnp.dot(lo, e, preferred_element_type=F32))
    return outs[0] if len(outs) == 1 else jnp.concatenate(outs, axis=-1)


def _exact_dot(x, m01, left):
    parts = _split3(x)
    if left:
        return sum(jnp.dot(m01, p, preferred_element_type=F32) for p in parts)
    return sum(jnp.dot(p, m01, preferred_element_type=F32) for p in parts)


def _sigmoid(x):
    return 1.0 / (1.0 + jnp.exp(-x))


def _softplus(x):
    return jnp.maximum(x, 0.0) + jnp.log1p(jnp.exp(-jnp.abs(x)))


def _inproj_kernel(x_ref, g_ref, wq_ref, wk_ref, wv_ref, wl_ref, wr_ref, we_ref,
                   oq_ref, ok_ref, ov_ref, ol_ref, or_ref, oe_ref):
    x = x_ref[...]
    h = x * lax.rsqrt(jnp.mean(x * x, -1, keepdims=True) + NORM_EPS) * g_ref[...]
    hb = h.astype(BF16)
    for w_ref, o_ref in ((wq_ref, oq_ref), (wk_ref, ok_ref), (wv_ref, ov_ref), (wl_ref, ol_ref),
                         (wr_ref, or_ref), (we_ref, oe_ref)):
        o_ref[...] = jnp.dot(hb, w_ref[...], preferred_element_type=F32)


def _inproj(x, g, ws, tm):
    n = x.shape[0]
    widths = [w.shape[1] for w in ws]
    row = lambda wd: pl.BlockSpec((tm, wd), lambda i: (i, 0))
    return pl.pallas_call(
        _inproj_kernel,
        grid=(n // tm,),
        in_specs=[row(D_MODEL), _resident((1, D_MODEL))] + [_resident(w.shape) for w in ws],
        out_specs=[row(wd) for wd in widths],
        out_shape=[jax.ShapeDtypeStruct((n, wd), F32) for wd in widths],
        compiler_params=_cparams(("parallel",)),
        name="inproj",
    )(x, g, *ws)


def _fox_norms(q_ref, k_ref, fl_ref, qg_ref, kg_ref, fb_ref):
    e = _seg_ones()

    def hnorm(x, g):
        ms = _seg_sum(x * x, e) * (1.0 / HEAD_DIM)
        return x * lax.rsqrt(ms + NORM_EPS) * g

    qn = hnorm(q_ref[...], qg_ref[...]) * (HEAD_DIM ** -0.5)
    kn = hnorm(k_ref[...], kg_ref[...])
    z = fl_ref[...] + fb_ref[...]
    lf = jnp.minimum(z, 0.0) - jnp.log1p(jnp.exp(-jnp.abs(z)))
    return qn, kn, lf


def _foxprep_sample_kernel(q_ref, k_ref, fl_ref, qg_ref, kg_ref, fb_ref, qn_ref, kn_ref, lf_ref):
    qn_ref[...], kn_ref[...], lf_ref[...] = _fox_norms(q_ref, k_ref, fl_ref, qg_ref, kg_ref, fb_ref)


def _foxprep_prompt_kernel(q_ref, k_ref, fl_ref, qg_ref, kg_ref, fb_ref, v_ref,
                           kt_ref, vt_ref, lft_ref, qx_ref, kx_ref, vb_ref, carry_ref, *, tm):
    qn, kn, lf = _fox_norms(q_ref, k_ref, fl_ref, qg_ref, kg_ref, fb_ref)
    v = v_ref[...]
    vb_ref[...] = v.astype(BF16)
    for pg in range(tm // PAGE_SIZE):
        rows = slice(pg * PAGE_SIZE, (pg + 1) * PAGE_SIZE)
        kt_ref[pg] = kn[rows, :].T
        vt_ref[pg] = v[rows, :].T
        lft_ref[pg] = lf[rows, :].T[:FOX_H, :]

    @pl.when(pl.program_id(1) == 0)
    def _():
        carry_ref[...] = jnp.zeros_like(carry_ref)

    tri = (_iota((tm, tm), 0) >= _iota((tm, tm), 1)).astype(BF16)
    c = _exact_dot(lf, tri, left=True) + carry_ref[...]
    carry_ref[...] = c[tm - 1:tm, :]

    lane = _iota((tm, LANES), 1)
    j = lane % HEAD_DIM
    for h in range(FOX_H):
        hp, hh = divmod(h, HEADS_PER_TILE)
        c_hi, c_mid, c_lo = (p.astype(F32) for p in _split3(jnp.broadcast_to(c[:, h:h + 1], (tm, LANES))))
        ext_q = jnp.where(j == 0, c_hi, jnp.where(j == 1, c_mid, jnp.where(j == 2, c_lo, jnp.where(j < 6, 1.0, 0.0))))
        ext_k = jnp.where(j < 3, 1.0, jnp.where(j == 3, -c_hi, jnp.where(j == 4, -c_mid, jnp.where(j == 5, -c_lo, 0.0))))
        own = lane // HEAD_DIM == hh
        pair = slice(hp * LANES, (hp + 1) * LANES)
        tile = slice(h * LANES, (h + 1) * LANES)
        qx_ref[:, tile] = jnp.where(own, qn[:, pair], ext_q).astype(BF16)
        kx_ref[:, tile] = jnp.where(own, kn[:, pair], ext_k).astype(BF16)


def _foxprep_sample(fq, fk, fl, qg, kg, fb, tm):
    n = fq.shape[0]
    row = lambda wd: pl.BlockSpec((tm, wd), lambda i: (i, 0))
    return pl.pallas_call(
        _foxprep_sample_kernel,
        grid=(n // tm,),
        in_specs=[row(FOX_W), row(FOX_W), row(LANES), _resident((1, FOX_W)), _resident((1, FOX_W)),
                  _resident((1, LANES))],
        out_specs=[row(FOX_W), row(FOX_W), row(LANES)],
        out_shape=[jax.ShapeDtypeStruct((n, FOX_W), F32), jax.ShapeDtypeStruct((n, FOX_W), F32),
                   jax.ShapeDtypeStruct((n, LANES), F32)],
        compiler_params=_cparams(("parallel",)),
        name="foxprep_sample",
    )(fq, fk, fl, qg, kg, fb)


def _foxprep_prompt(fq, fk, fl, fv, qg, kg, fb, nseq, tm):
    n = fq.shape[0]
    nt = n // nseq // tm
    ppt = tm // PAGE_SIZE
    n_pp = n // PAGE_SIZE
    row = lambda wd: pl.BlockSpec((tm, wd), lambda b, j: (b * nt + j, 0))
    pages = lambda r: pl.BlockSpec((ppt, r, PAGE_SIZE), lambda b, j: (b * nt + j, 0, 0))
    return pl.pallas_call(
        functools.partial(_foxprep_prompt_kernel, tm=tm),
        grid=(nseq, nt),
        in_specs=[row(FOX_W), row(FOX_W), row(LANES), _resident((1, FOX_W)), _resident((1, FOX_W)),
                  _resident((1, LANES)), row(FOX_W)],
        out_specs=[pages(FOX_W), pages(FOX_W), pages(FOX_H), row(FOX_H * LANES), row(FOX_H * LANES), row(FOX_W)],
        out_shape=[jax.ShapeDtypeStruct((n_pp, FOX_W, PAGE_SIZE), F32), jax.ShapeDtypeStruct((n_pp, FOX_W, PAGE_SIZE), F32),
                   jax.ShapeDtypeStruct((n_pp, FOX_H, PAGE_SIZE), F32),
                   jax.ShapeDtypeStruct((n, FOX_H * LANES), BF16), jax.ShapeDtypeStruct((n, FOX_H * LANES), BF16),
                   jax.ShapeDtypeStruct((n, FOX_W), BF16)],
        scratch_shapes=[pltpu.VMEM((1, LANES), F32)],
        compiler_params=_cparams(("parallel", "arbitrary")),
        name="foxprep_prompt",
    )(fq, fk, fl, qg, kg, fb, fv)


def _foxattn_kernel(qx_ref, kx_ref, vb_ref, o_ref, *, tq, tk):
    qi = pl.program_id(2)
    q = [qx_ref[:, hh * LANES:(hh + 1) * LANES] for hh in range(HEADS_PER_TILE)]
    n_full = (qi * tq) // tk

    def block(off, carry, diagonal):
        vb = vb_ref[pl.ds(off, tk), :]
        new = []
        for hh in range(HEADS_PER_TILE):
            m, l, acc = carry[hh]
            kb = kx_ref[pl.ds(off, tk), hh * LANES:(hh + 1) * LANES]
            s = lax.dot_general(q[hh], kb, (((1,), (1,)), ((), ())), preferred_element_type=F32)
            if diagonal:
                s = jnp.where(off + _iota((tq, tk), 1) <= qi * tq + _iota((tq, tk), 0), s, NEG_BIG)
            m_new = jnp.maximum(m, jnp.max(s, axis=-1, keepdims=True))
            alpha = jnp.exp(m - m_new)
            p = jnp.exp(s - m_new)
            l = alpha * l + jnp.sum(p, axis=-1, keepdims=True)
            acc = alpha * acc + jnp.dot(p.astype(BF16), vb, preferred_element_type=F32)
            new.append((m_new, l, acc))
        return tuple(new)

    init = tuple((jnp.full((tq, 1), NEG_BIG, F32), jnp.zeros((tq, 1), F32), jnp.zeros((tq, LANES), F32))
                 for _ in range(HEADS_PER_TILE))
    carry = lax.fori_loop(0, n_full, lambda j, c: block(pl.multiple_of(j * tk, tk), c, False), init)
    carry = block(pl.multiple_of(n_full * tk, tk), carry, True)
    outs = [acc / l for _, l, acc in carry]
    o_ref[...] = jnp.where(_iota((tq, LANES), 1) // HEAD_DIM == 0, outs[0], outs[1])


def _foxattn(qx, kx, vb, nseq, tq, tk):
    n = qx.shape[0]
    t = n // nseq
    nq = t // tq
    assert tk % tq == 0 and t % tk == 0
    npair = FOX_H // HEADS_PER_TILE
    pair_w = HEADS_PER_TILE * LANES
    return pl.pallas_call(
        functools.partial(_foxattn_kernel, tq=tq, tk=tk),
        grid=(nseq, npair, nq),
        in_specs=[
            pl.BlockSpec((tq, pair_w), lambda b, h, i: (b * nq + i, h)),
            pl.BlockSpec((t, pair_w), lambda b, h, i: (b, h)),
            pl.BlockSpec((t, LANES), lambda b, h, i: (b, h)),
        ],
        out_specs=pl.BlockSpec((tq, LANES), lambda b, h, i: (b * nq + i, h)),
        out_shape=jax.ShapeDtypeStruct((n, FOX_W), F32),
        compiler_params=_cparams(("parallel", "parallel", "arbitrary")),
        name="foxattn",
    )(qx, kx, vb)


def _foxsample_kernel(pt_ref, q_ref, kn_ref, vn_ref, lfn_ref, *rest, n_pages, t_new):
    del pt_ref
    kp = rest[:n_pages]
    vp = rest[n_pages:2 * n_pages]
    lp = rest[2 * n_pages:3 * n_pages]
    o_ref = rest[3 * n_pages]
    nrow = t_new * FOX_H

    rep = lambda x: jnp.concatenate([x] * t_new, axis=0)
    q = q_ref[...]
    hmask = _iota((FOX_H, FOX_W), 1) // HEAD_DIM == _iota((FOX_H, FOX_W), 0)
    qbd = jnp.concatenate(
        [jnp.where(hmask, jnp.broadcast_to(q[t:t + 1, :], (FOX_H, FOX_W)), 0.0) for t in range(t_new)],
        axis=0).astype(BF16)

    lfn = lfn_ref[...]
    diag8 = _iota((FOX_H, LANES), 1) == _iota((FOX_H, LANES), 0)
    cn_cols = []
    run = jnp.zeros((1, LANES), F32)
    for t in range(t_new):
        run = run + lfn[t:t + 1, :]
        cn_cols.append(jnp.sum(jnp.where(diag8, jnp.broadcast_to(run, (FOX_H, LANES)), 0.0), axis=-1, keepdims=True))
    cn = jnp.concatenate(cn_cols, axis=0)

    zeros_tail = jnp.zeros((PAGE_SIZE - SUBLANES, FOX_W), F32)
    key = _iota((nrow, PAGE_SIZE), 1)
    trow = _iota((nrow, PAGE_SIZE), 0) // FOX_H
    ckey = jnp.zeros((nrow, PAGE_SIZE), F32)
    for j in range(t_new):
        ckey = jnp.where(key == j, rep(cn_cols[j]), ckey)
    s_new = _bdot_nt(qbd, jnp.concatenate([kn_ref[...], zeros_tail], axis=0)) + (cn - ckey)
    s_new = jnp.where(key <= trow, s_new, NEG_BIG)

    lf_all = jnp.concatenate([lp[i][...] for i in range(n_pages)], axis=0)
    upper = (_iota((PAGE_SIZE, PAGE_SIZE), 0) <= _iota((PAGE_SIZE, PAGE_SIZE), 1)).astype(BF16)
    cp_all = _exact_dot(lf_all, upper, left=False)
    before = [jnp.zeros((FOX_H, 1), F32)]
    for i in range(n_pages):
        before.append(before[-1] + cp_all[i * FOX_H:(i + 1) * FOX_H, PAGE_SIZE - 1:PAGE_SIZE])
    cq_abs = rep(before[n_pages]) + cn
    s_past = []
    for i in range(n_pages):
        cp = rep(cp_all[i * FOX_H:(i + 1) * FOX_H, :] + before[i])
        s_past.append(_bdot(qbd, kp[i][...].reshape(FOX_W, PAGE_SIZE)) + (cq_abs - cp))

    m = jnp.max(s_new, axis=-1, keepdims=True)
    for s in s_past:
        m = jnp.maximum(m, jnp.max(s, axis=-1, keepdims=True))
    p = jnp.exp(s_new - m)
    l = jnp.sum(p, axis=-1, keepdims=True)
    o = _bdot(p, jnp.concatenate([vn_ref[...], zeros_tail], axis=0))
    for i in range(n_pages):
        p = jnp.exp(s_past[i] - m)
        l = l + jnp.sum(p, axis=-1, keepdims=True)
        o = o + _bdot_nt(p, vp[i][...].reshape(FOX_W, PAGE_SIZE))
    o = o / l
    omask = _iota((nrow, FOX_W), 1) // HEAD_DIM == _iota((nrow, FOX_W), 0) % FOX_H
    o_ref[...] = jnp.sum(jnp.where(omask, o, 0.0).reshape(t_new, FOX_H, FOX_W), axis=1)


def _foxsample(page_table, layer, qn, kn, vn, lfn, cache_kt, cache_vt, cache_lft, t_new):
    db, n_pages = page_table.shape
    pt = page_table.reshape(-1)
    new = lambda wd: pl.BlockSpec((None, SUBLANES, wd), lambda b, pt: (b, 0, 0))

    def page(shape, i):
        nd = len(shape)
        return pl.BlockSpec((None, None) + shape, lambda b, pt: (layer, pt[b * n_pages + i]) + (0,) * nd)

    in_specs = [new(FOX_W), new(FOX_W), new(FOX_W), new(LANES)]
    in_specs += [page((FOX_H, HEAD_DIM, PAGE_SIZE), i) for i in range(n_pages)]
    in_specs += [page((FOX_H, HEAD_DIM, PAGE_SIZE), i) for i in range(n_pages)]
    in_specs += [page((FOX_H, PAGE_SIZE), i) for i in range(n_pages)]
    return pl.pallas_call(
        functools.partial(_foxsample_kernel, n_pages=n_pages, t_new=t_new),
        grid_spec=pltpu.PrefetchScalarGridSpec(
            num_scalar_prefetch=1,
            grid=(db,),
            in_specs=in_specs,
            out_specs=pl.BlockSpec((None, t_new, FOX_W), lambda b, pt: (b, 0, 0)),
        ),
        out_shape=jax.ShapeDtypeStruct((db, t_new, FOX_W), F32),
        compiler_params=_cparams(("arbitrary",)),
        name="foxsample",
    )(pt, qn, kn, vn, lfn, *([cache_kt] * n_pages), *([cache_vt] * n_pages), *([cache_lft] * n_pages))


def _rwkvprep_kernel(p_ref, init_ref, mu_ref, w0_ref, w2_ref, a0_ref, a2_ref, g2_ref, kk_ref, ka_ref,
                     r_out, w_out, k_out, v_out, a_out, b_out, g_out, *carry, tm, period):
    e = _seg_ones()
    p = p_ref[...]
    rolled = pltpu.roll(p, 1, 0)
    row = _iota((tm, RWKV_PROJ), 0)
    if period >= tm:
        carry_ref, = carry

        @pl.when(pl.program_id(1) == 0)
        def _():
            carry_ref[...] = init_ref[...]

        pp = jnp.where(row == 0, carry_ref[...], rolled)
        carry_ref[...] = p[tm - 1:tm, :]
    else:
        pp = jnp.where(row % period == 0, init_ref[...], rolled)
    xs = p + (pp - p) * mu_ref[...]
    r = xs[:, 0:RWKV_W]
    k = xs[:, RWKV_W:2 * RWKV_W]
    v = xs[:, 2 * RWKV_W:3 * RWKV_W]
    lo = xs[:, 3 * RWKV_W:]
    w = -_softplus(-(w0_ref[...] + _bdot(jnp.tanh(lo), w2_ref[...]))) - 0.5
    a = _sigmoid(a0_ref[...] + _bdot(lo, a2_ref[...]))
    kk = k * kk_ref[...]
    nrm = jnp.sqrt(_seg_sum(kk * kk, e))
    kk = kk / jnp.maximum(nrm, 1e-12)
    r_out[...] = r
    w_out[...] = jnp.exp(-jnp.exp(w))
    k_out[...] = k * (1.0 + (a - 1.0) * ka_ref[...])
    v_out[...] = v
    a_out[...] = -kk
    b_out[...] = kk * a
    g_out[...] = _bdot(_sigmoid(lo), g2_ref[...])


def _rwkvprep(p, shift, period, mu, w0, w2p, a0, a2p, g2p, k_k, k_a, tm):
    n = p.shape[0]
    nseq = n // period
    vec = _resident((1, RWKV_W))
    lora = _resident((LANES, RWKV_W))
    if period >= tm:
        nt = period // tm
        grid = (nseq, nt)
        row = lambda wd: pl.BlockSpec((tm, wd), lambda b, j: (b * nt + j, 0))
        init = shift[:, None, :]
        init_spec = pl.BlockSpec((None, 1, RWKV_PROJ), lambda b, j: (b, 0, 0))
        scratch = [pltpu.VMEM((1, RWKV_PROJ), F32)]
        sem = ("parallel", "arbitrary")
    else:
        grid = (n // tm,)
        row = lambda wd: pl.BlockSpec((tm, wd), lambda i: (i, 0))
        init = jnp.repeat(shift, period, axis=0)
        init_spec = row(RWKV_PROJ)
        scratch = []
        sem = ("parallel",)
    return pl.pallas_call(
        functools.partial(_rwkvprep_kernel, tm=tm, period=period),
        grid=grid,
        in_specs=[row(RWKV_PROJ), init_spec, _resident((1, RWKV_PROJ)), vec, lora, vec, lora, lora, vec, vec],
        out_specs=[row(RWKV_W)] * 7,
        out_shape=[jax.ShapeDtypeStruct((n, RWKV_W), F32)] * 7,
        scratch_shapes=scratch,
        compiler_params=_cparams(sem),
        name="rwkvprep",
    )(p, init, mu, w0, w2p, a0, a2p, g2p, k_k, k_a)


def _rwkvscan_kernel(r_ref, w_ref, k_ref, v_ref, a_ref, b_ref, s0_ref, y_ref, st_ref, s_scr, *, bblk, tblk):
    tb = pl.program_id(1)
    ng = HEADS_PER_TILE * bblk

    @pl.when(tb == 0)
    def _():
        s_scr[...] = s0_ref[...]

    e = _seg_ones()
    diag = (_iota((HEAD_DIM, LANES), 1) % HEAD_DIM == _iota((HEAD_DIM, LANES), 0)).astype(BF16)[None]
    half = (_iota((SUBLANES, LANES), 1) // HEAD_DIM == _iota((SUBLANES, LANES), 0)).astype(BF16)
    first = _iota((1, LANES), 1) < HEAD_DIM

    sub = min(SUBLANES, tblk)

    def chunk(c, carry):
        off = pl.multiple_of(c * sub, sub)
        tiles = [[ref[b, pl.ds(off, sub), :] for b in range(bblk)] for ref in (r_ref, w_ref, k_ref, v_ref, a_ref, b_ref)]

        def rows(kind, i):
            return jnp.concatenate(
                [tiles[kind][b][i:i + 1, hp * LANES:(hp + 1) * LANES][None] for b in range(bblk) for hp in range(2)],
                axis=0)

        def seg(x):
            return jnp.dot(x.reshape(ng * HEAD_DIM, LANES).astype(BF16), e,
                           preferred_element_type=F32).reshape(ng, HEAD_DIM, LANES)

        s = s_scr[...]
        ytiles = [[] for _ in range(bblk)]
        for i in range(sub):
            sa = seg(s * rows(4, i))
            vdiag = (rows(3, i).astype(BF16) * diag).reshape(ng * HEAD_DIM, LANES)
            vcol = jnp.dot(vdiag, e, preferred_element_type=F32).reshape(ng, HEAD_DIM, LANES)
            s = s * rows(1, i) + sa * rows(5, i) + vcol * rows(2, i)
            sr = (s * rows(0, i)).astype(BF16)
            for b in range(bblk):
                out = lax.dot_general(half, sr[2 * b:2 * b + 2].reshape(2 * HEAD_DIM, LANES),
                                      (((1,), (1,)), ((), ())), preferred_element_type=F32)
                h0, h1 = out[0:1, :], out[1:2, :]
                ytiles[b].append(jnp.concatenate(
                    [jnp.where(first, h0, pltpu.roll(h1, HEAD_DIM, 1)),
                     jnp.where(first, pltpu.roll(h0, HEAD_DIM, 1), h1)], axis=1))
        s_scr[...] = s
        for b in range(bblk):
            y_ref[b, pl.ds(off, sub), :] = jnp.concatenate(ytiles[b], axis=0)
        return carry

    lax.fori_loop(0, tblk // sub, chunk, 0)

    @pl.when(tb == pl.num_programs(1) - 1)
    def _():
        st_ref[...] = s_scr[...]


def _rwkvscan(r, w, k, v, a, b, s0, bblk, tblk):
    nb, t, _ = r.shape
    ng = HEADS_PER_TILE * bblk
    seq = pl.BlockSpec((bblk, tblk, RWKV_W), lambda i, j: (i, j, 0))
    st = pl.BlockSpec((ng, HEAD_DIM, LANES), lambda i, j: (i, 0, 0))
    return pl.pallas_call(
        functools.partial(_rwkvscan_kernel, bblk=bblk, tblk=tblk),
        grid=(nb // bblk, t // tblk),
        in_specs=[seq] * 6 + [st],
        out_specs=[seq, st],
        out_shape=[jax.ShapeDtypeStruct((nb, t, RWKV_W), F32), jax.ShapeDtypeStruct(s0.shape, F32)],
        scratch_shapes=[pltpu.VMEM((ng, HEAD_DIM, LANES), F32)],
        compiler_params=_cparams(("parallel", "arbitrary")),
        name="rwkvscan",
    )(r, w, k, v, a, b, s0)


def _rwkv_short_kernel(z_ref, s0_ref, y_ref, st_ref):
    nt = z_ref.shape[0]
    R, W, K, V, A, B = range(6)

    def value_rows(i, carry):
        base = pl.multiple_of(i * SUBLANES, SUBLANES)
        vt = [z_ref[t, V, pl.ds(base, SUBLANES), :] for t in range(nt)]
        yrows = [[] for _ in range(nt)]
        for j in range(SUBLANES):
            s = s0_ref[base + j]
            for t in range(nt):
                sa = jnp.sum(s * z_ref[t, A], axis=0, keepdims=True)
                s = s * z_ref[t, W] + sa * z_ref[t, B] + vt[t][j:j + 1, :] * z_ref[t, K]
                yrows[t].append(jnp.sum(s * z_ref[t, R], axis=0, keepdims=True))
            st_ref[base + j] = s
        for t in range(nt):
            y_ref[t, pl.ds(base, SUBLANES), :] = jnp.concatenate(yrows[t], axis=0)
        return carry

    lax.fori_loop(0, HEAD_DIM // SUBLANES, value_rows, 0)


def _rwkv_short(zt, s0):
    nt, _, nh, _, nb = zt.shape
    assert nb == LANES and nt <= SUBLANES
    st = pl.BlockSpec((None, HEAD_DIM, HEAD_DIM, LANES), lambda h: (h, 0, 0, 0))
    return pl.pallas_call(
        _rwkv_short_kernel,
        grid=(nh,),
        in_specs=[pl.BlockSpec((nt, 6, None, HEAD_DIM, LANES), lambda h: (0, 0, h, 0, 0)), st],
        out_specs=[pl.BlockSpec((nt, None, HEAD_DIM, LANES), lambda h: (0, h, 0, 0)), st],
        out_shape=[jax.ShapeDtypeStruct((nt, nh, HEAD_DIM, LANES), F32), jax.ShapeDtypeStruct(s0.shape, F32)],
        compiler_params=_cparams(("parallel",)),
        name="rwkv_short",
    )(zt, s0)


def _rope_kernel(inv_ref, cos_ref, sin_ref, *, pos0):
    t = cos_ref.shape[0]
    pos = (pos0 + _iota((t, LANES), 0)).astype(F32)
    ang = pos * inv_ref[...]
    first_half = _iota((t, LANES), 1) % HEAD_DIM < HEAD_DIM // 2
    cos_ref[...] = jnp.cos(ang)
    sin_ref[...] = jnp.where(first_half, -jnp.sin(ang), jnp.sin(ang))


def _rope_tables(inv_lanes, t, pos0):
    return pl.pallas_call(
        functools.partial(_rope_kernel, pos0=pos0),
        out_shape=[jax.ShapeDtypeStruct((t, LANES), F32)] * 2,
        name="rope",
    )(inv_lanes)


def _ret_kernel(x_ref, cos_ref, sin_ref, lg_ref, lgh_ref, gn_ref, r0_ref, o_ref, rt_ref, r_scr,
                *, bs, l_in, l_true):
    lp = RET_CHUNK
    npair = RET_H // HEADS_PER_TILE
    c = pl.program_id(1)

    @pl.when(c == 0)
    def _():
        r_scr[...] = r0_ref[...]

    def pad(x):
        if l_in == lp:
            return x
        return jnp.concatenate([x, jnp.zeros((lp - l_in, x.shape[1]), F32)], axis=0)

    lane = _iota((lp, LANES), 1)
    first_half = lane % HEAD_DIM < HEAD_DIM // 2
    cos = pad(cos_ref[...])
    sin = pad(sin_ref[...])

    def rope(x):
        swapped = jnp.where(first_half, pltpu.roll(x, LANES - HEAD_DIM // 2, 1), pltpu.roll(x, HEAD_DIM // 2, 1))
        return x * cos + swapped * sin

    idx = _iota((lp, 1), 0).astype(F32)
    diff = (_iota((lp, lp), 0) - _iota((lp, lp), 1)).astype(F32)
    same_head = _iota((LANES, LANES), 0) // HEAD_DIM == _iota((LANES, LANES), 1) // HEAD_DIM
    e = _seg_ones()
    dmask, cross, kdec, cdec = [], [], [], []
    for hp in range(npair):
        lg = lg_ref[hp]
        dmask.append([jnp.where(diff >= 0.0, jnp.exp(lgh_ref[hp, hh:hh + 1, :] * jnp.maximum(diff, 0.0)), 0.0)
                      for hh in range(HEADS_PER_TILE)])
        cross.append(jnp.exp(lg * (idx + 1.0)))
        kdec.append(jnp.exp(lg * jnp.maximum(l_true - 1.0 - idx, 0.0)))
        cdec.append(jnp.exp(lg * float(l_true)))

    for s in range(bs):
        for hp in range(npair):
            col = lambda j: x_ref[s, :, (j * npair + hp) * LANES:(j * npair + hp + 1) * LANES]
            q = rope(pad(col(0)))
            k = rope(pad(col(1))) * (HEAD_DIM ** -0.5)
            v = pad(col(2))
            g = pad(col(3))
            vb = v.astype(BF16)
            kb = k.astype(BF16)
            outs = []
            for hh in range(HEADS_PER_TILE):
                qh = jnp.where(lane // HEAD_DIM == hh, q, 0.0).astype(BF16)
                sc = lax.dot_general(qh, kb, (((1,), (1,)), ((), ())), preferred_element_type=F32) * dmask[hp][hh]
                outs.append(jnp.dot(sc.astype(BF16), vb, preferred_element_type=F32))
            o = jnp.where(lane // HEAD_DIM == 0, outs[0], outs[1])
            r = r_scr[s * npair + hp]
            o = o + _bdot(q, r) * cross[hp]
            upd = jnp.dot((k * kdec[hp]).T.astype(BF16), vb, preferred_element_type=F32)
            r_new = r * cdec[hp] + jnp.where(same_head, upd, 0.0)
            r_scr[s * npair + hp] = r_new
            mean = _seg_sum(o, e) * (1.0 / HEAD_DIM)
            xc = o - mean
            var = _seg_sum(xc * xc, e) * (1.0 / HEAD_DIM)
            y = xc * lax.rsqrt(var + RET_GN_EPS) * gn_ref[:, hp * LANES:(hp + 1) * LANES] * (g * _sigmoid(g))
            o_ref[s, :, hp * LANES:(hp + 1) * LANES] = y[:l_in, :]

    @pl.when(c == pl.num_programs(1) - 1)
    def _():
        rt_ref[...] = r_scr[...]


def _retention(proj, cos, sin, lg, lgh, gn, r0, bs, l_in, l_true):
    nb, t, _ = proj.shape
    nc = t // l_in
    npair = RET_H // HEADS_PER_TILE
    st = pl.BlockSpec((bs * npair, LANES, LANES), lambda b, c: (b, 0, 0))
    tab = pl.BlockSpec((l_in, LANES), lambda b, c: (c, 0))
    return pl.pallas_call(
        functools.partial(_ret_kernel, bs=bs, l_in=l_in, l_true=l_true),
        grid=(nb // bs, nc),
        in_specs=[pl.BlockSpec((bs, l_in, 4 * RET_W), lambda b, c: (b, c, 0)), tab, tab,
                  _resident(lg.shape), _resident(lgh.shape), _resident(gn.shape), st],
        out_specs=[pl.BlockSpec((bs, l_in, RET_W), lambda b, c: (b, c, 0)), st],
        out_shape=[jax.ShapeDtypeStruct((nb, t, RET_W), F32), jax.ShapeDtypeStruct(r0.shape, F32)],
        scratch_shapes=[pltpu.VMEM((bs * npair, LANES, LANES), F32)],
        compiler_params=_cparams(("parallel", "arbitrary")),
        name="retention",
    )(proj, cos, sin, lg, lgh, gn, r0)


def _rope_cols_kernel(inv_ref, cos_ref, sin_ref, *, pos0):
    for t in range(cos_ref.shape[0]):
        ang = float(pos0 + t) * inv_ref[...]
        cos_ref[t] = jnp.cos(ang)
        sin_ref[t] = jnp.sin(ang)


def _rope_cols(inv_cols, t, pos0):
    return pl.pallas_call(
        functools.partial(_rope_cols_kernel, pos0=pos0),
        out_shape=[jax.ShapeDtypeStruct((t, HEAD_DIM // 2, LANES), F32)] * 2,
        name="rope_cols",
    )(inv_cols)


def _ret_short_kernel(x_ref, cos_ref, sin_ref, lg_ref, gn_ref, r0_ref, o_ref, rt_ref, q_scr, k_scr):
    nt = x_ref.shape[0]
    half = HEAD_DIM // 2
    lg = lg_ref[0:1, :]
    gam = lambda n: jnp.exp(lg * float(n))

    def rope(x, t):
        x1, x2 = x[:half], x[half:]
        c, s = cos_ref[t], sin_ref[t]
        return jnp.concatenate([x1 * c - x2 * s, x2 * c + x1 * s], axis=0)

    q = [rope(x_ref[t, 0], t) for t in range(nt)]
    k = [rope(x_ref[t, 1], t) * (HEAD_DIM ** -0.5) for t in range(nt)]
    v = [x_ref[t, 2] for t in range(nt)]

    intra = []
    for t in range(nt):
        acc = None
        for t2 in range(t + 1):
            term = (jnp.sum(q[t] * k[t2], axis=0, keepdims=True) * gam(t - t2)) * v[t2]
            acc = term if acc is None else acc + term
        intra.append(acc)

    for t in range(nt):
        q_scr[t] = q[t]
        k_scr[t] = k[t] * gam(nt - 1 - t)
    decay_all = gam(nt)

    def key_rows(i, acc):
        base = pl.multiple_of(i * SUBLANES, SUBLANES)
        qb = [q_scr[t, pl.ds(base, SUBLANES), :] for t in range(nt)]
        kb = [k_scr[t, pl.ds(base, SUBLANES), :] for t in range(nt)]
        acc = list(acc)
        for j in range(SUBLANES):
            r = r0_ref[base + j]
            upd = r * decay_all
            for t in range(nt):
                acc[t] = acc[t] + qb[t][j:j + 1, :] * r
                upd = upd + kb[t][j:j + 1, :] * v[t]
            rt_ref[base + j] = upd
        return tuple(acc)

    cross = lax.fori_loop(0, HEAD_DIM // SUBLANES, key_rows,
                          tuple(jnp.zeros((HEAD_DIM, LANES), F32) for _ in range(nt)))

    for t in range(nt):
        o = intra[t] + cross[t] * gam(t + 1)
        mean = jnp.mean(o, axis=0, keepdims=True)
        xc = o - mean
        var = jnp.mean(xc * xc, axis=0, keepdims=True)
        g = x_ref[t, 3]
        o_ref[t] = xc * lax.rsqrt(var + RET_GN_EPS) * gn_ref[...] * (g * _sigmoid(g))


def _retention_short(xt, cos, sin, lg_rows, gn_cols, r0):
    nt, _, nh, _, nb = xt.shape
    assert nb == LANES and nt <= SUBLANES
    tab = _resident(cos.shape)
    return pl.pallas_call(
        _ret_short_kernel,
        grid=(nh,),
        in_specs=[pl.BlockSpec((nt, 4, None, HEAD_DIM, LANES), lambda h: (0, 0, h, 0, 0)), tab, tab,
                  pl.BlockSpec((None, SUBLANES, LANES), lambda h: (h, 0, 0)),
                  pl.BlockSpec((None, HEAD_DIM, LANES), lambda h: (h, 0, 0)),
                  pl.BlockSpec((None, HEAD_DIM, HEAD_DIM, LANES), lambda h: (h, 0, 0, 0))],
        out_specs=[pl.BlockSpec((nt, None, HEAD_DIM, LANES), lambda h: (0, h, 0, 0)),
                   pl.BlockSpec((None, HEAD_DIM, HEAD_DIM, LANES), lambda h: (h, 0, 0, 0))],
        out_shape=[jax.ShapeDtypeStruct((nt, nh, HEAD_DIM, LANES), F32), jax.ShapeDtypeStruct(r0.shape, F32)],
        scratch_shapes=[pltpu.VMEM((nt, HEAD_DIM, LANES), F32), pltpu.VMEM((nt, HEAD_DIM, LANES), F32)],
        compiler_params=_cparams(("parallel",)),
        name="retention_short",
    )(xt, cos, sin, lg_rows, gn_cols, r0)


def _outproj_kernel(x_ref, fo_ref, y_ref, r_ref, k_ref, v_ref, g_ref, eo_ref, lw_ref, lb_ref, rk_ref,
                    wf_ref, wr_ref, we_ref, o_ref):
    e = _seg_ones()
    y = y_ref[...]
    mean = _seg_sum(y, e) * (1.0 / HEAD_DIM)
    yc = y - mean
    var = _seg_sum(yc * yc, e) * (1.0 / HEAD_DIM)
    yn = yc * lax.rsqrt(var + RWKV_GN_EPS) * lw_ref[...] + lb_ref[...]
    v = v_ref[...]
    bonus = _seg_sum(r_ref[...] * k_ref[...] * rk_ref[...], e) * v
    ro = (yn + bonus) * g_ref[...]
    o_ref[...] = (x_ref[...] + _bdot(fo_ref[...], wf_ref[...]) + _bdot(ro, wr_ref[...])
                  + _bdot(eo_ref[...], we_ref[...]))


def _outproj(x, fo, y, r, k, v, g, eo, lw, lb, rk, wf, wr, we, tm):
    n = x.shape[0]
    row = lambda wd: pl.BlockSpec((tm, wd), lambda i: (i, 0))
    vec = _resident((1, RWKV_W))
    return pl.pallas_call(
        _outproj_kernel,
        grid=(n // tm,),
        in_specs=[row(D_MODEL), row(FOX_W)] + [row(RWKV_W)] * 6 + [vec, vec, vec,
                  _resident(wf.shape), _resident(wr.shape), _resident(we.shape)],
        out_specs=row(D_MODEL),
        out_shape=jax.ShapeDtypeStruct((n, D_MODEL), F32),
        compiler_params=_cparams(("parallel",)),
        name="outproj",
    )(x, fo, y, r, k, v, g, eo, lw, lb, rk, wf, wr, we)


def _ffn_kernel(x_ref, g_ref, wg_ref, wu_ref, wd_ref, o_ref):
    x = x_ref[...]
    h = (x * lax.rsqrt(jnp.mean(x * x, -1, keepdims=True) + NORM_EPS) * g_ref[...]).astype(BF16)
    gate = jnp.dot(h, wg_ref[...], preferred_element_type=F32)
    up = jnp.dot(h, wu_ref[...], preferred_element_type=F32)
    act = gate * _sigmoid(gate) * up
    o_ref[...] = x + _bdot(act, wd_ref[...])


def _ffn(x, g, wg, wu, wd, tm):
    n = x.shape[0]
    row = pl.BlockSpec((tm, D_MODEL), lambda i: (i, 0))
    return pl.pallas_call(
        _ffn_kernel,
        grid=(n // tm,),
        in_specs=[row, _resident((1, D_MODEL)), _resident(wg.shape), _resident(wu.shape), _resident(wd.shape)],
        out_specs=row,
        out_shape=jax.ShapeDtypeStruct((n, D_MODEL), F32),
        compiler_params=_cparams(("parallel",)),
        name="ffn",
    )(x, g, wg, wu, wd)


def _rwkv_state_out(s, nb):
    s = s.reshape(nb, RWKV_H // 2, HEAD_DIM, 2, HEAD_DIM).transpose(0, 1, 3, 2, 4)
    return s.reshape(nb, RWKV_H, HEAD_DIM, HEAD_DIM)


def _ret_state_out(r, nb):
    r = r.reshape(nb, RET_H // 2, LANES, LANES)
    return jnp.stack([r[:, :, :HEAD_DIM, :HEAD_DIM], r[:, :, HEAD_DIM:, HEAD_DIM:]], axis=2).reshape(
        nb, RET_H, HEAD_DIM, HEAD_DIM)


def _layer_weights(l, ln_mix_g, w_in, fox_qn_g, fox_kn_g, fox_f_b, rwkv_mu, rwkv_w0, rwkv_w2, rwkv_a0, rwkv_a2,
                   rwkv_g2, rwkv_k_k, rwkv_k_a, rwkv_r_k, rwkv_lnx_w, rwkv_lnx_b, ret_gn_w, w_out, ln_ffn_g,
                   w_gate, w_up, w_down):
    wi = w_in[l].astype(BF16)
    o_f = 3 * FOX_W
    o_r = o_f + FOX_H
    o_e = o_r + RWKV_PROJ
    pad_rows = lambda w, off: jnp.zeros((LANES, RWKV_W), BF16).at[off:off + w.shape[0]].set(w.astype(BF16))
    wo = w_out[l].astype(BF16)
    return dict(
        ln_mix_g=ln_mix_g[l][None],
        w_in=[wi[:, 0:FOX_W], wi[:, FOX_W:2 * FOX_W], wi[:, 2 * FOX_W:o_f],
              jnp.pad(wi[:, o_f:o_r], ((0, 0), (0, LANES - FOX_H))), wi[:, o_r:o_e], wi[:, o_e:]],
        qg=jnp.tile(fox_qn_g[l], FOX_H)[None], kg=jnp.tile(fox_kn_g[l], FOX_H)[None],
        fb=jnp.pad(fox_f_b[l], (0, LANES - FOX_H))[None],
        mu=rwkv_mu[l][None], w0=rwkv_w0[l][None], a0=rwkv_a0[l][None],
        w2=pad_rows(rwkv_w2[l], 0), a2=pad_rows(rwkv_a2[l], RWKV_LORA_W),
        g2=pad_rows(rwkv_g2[l], RWKV_LORA_W + RWKV_LORA_A),
        k_k=rwkv_k_k[l][None], k_a=rwkv_k_a[l][None], r_k=rwkv_r_k[l].reshape(1, RWKV_W),
        lnx_w=rwkv_lnx_w[l][None], lnx_b=rwkv_lnx_b[l][None], gn=ret_gn_w[l][None],
        wo_f=wo[:FOX_W], wo_r=wo[FOX_W:FOX_W + RWKV_W], wo_e=wo[FOX_W + RWKV_W:],
        ln_ffn_g=ln_ffn_g[l][None],
        w_gate=w_gate[l].astype(BF16), w_up=w_up[l].astype(BF16), w_down=w_down[l].astype(BF16),
    )


def _mix_and_ffn(x2, fo, p_rwkv, shift, s0, eo, lw, nb, t, tm, bblk, tblk):
    r, w, k, v, a, b, g = _rwkvprep(p_rwkv, shift, t, lw['mu'], lw['w0'], lw['w2'], lw['a0'], lw['a2'], lw['g2'],
                                    lw['k_k'], lw['k_a'], tm)
    if s0.ndim == 4:
        zt = jnp.stack([r, w, k, v, a, b], axis=1).reshape(nb, t, 6, RWKV_H, HEAD_DIM)
        y, s_t = _rwkv_short(jnp.transpose(zt, (1, 2, 3, 4, 0)), s0)
        y = jnp.transpose(y, (3, 0, 1, 2))
    else:
        seq = lambda z: z.reshape(nb, t, RWKV_W)
        y, s_t = _rwkvscan(seq(r), seq(w), seq(k), seq(v), seq(a), seq(b), s0, bblk, tblk)
    x2 = _outproj(x2, fo, y.reshape(nb * t, RWKV_W), r, k, v, g, eo, lw['lnx_w'], lw['lnx_b'], lw['r_k'],
                  lw['wo_f'], lw['wo_r'], lw['wo_e'], tm)
    x2 = _ffn(x2, lw['ln_ffn_g'], lw['w_gate'], lw['w_up'], lw['w_down'], tm)
    return x2, s_t


def kernel(x_prompt, x_sample, cache_fox_k, cache_fox_v, cache_fox_logf, state_rwkv, state_rwkv_shift, state_ret,
           page_table, ln_mix_g, w_in, fox_qn_g, fox_kn_g, fox_f_b, rwkv_mu, rwkv_w0, rwkv_w2, rwkv_a0, rwkv_a2,
           rwkv_g2, rwkv_k_k, rwkv_k_a, rwkv_r_k, rwkv_lnx_w, rwkv_lnx_b, ret_gn_w, w_out, ln_ffn_g, w_gate,
           w_up, w_down):
    nb, t, _ = x_prompt.shape
    db, ts, _ = x_sample.shape
    depth = w_in.shape[0]
    n_pages = page_table.shape[1]
    past_len = n_pages * PAGE_SIZE
    n_pool = cache_fox_k.shape[1]

    half = HEAD_DIM // 2
    inv = ROPE_BASE ** (-jnp.arange(half, dtype=F32) / half)
    inv_lanes = jnp.tile(inv, LANES // half)[None]
    log_gamma = jnp.log1p(-jnp.exp2(-5.0 - jnp.arange(RET_H, dtype=F32)))
    lg = jnp.repeat(log_gamma, HEAD_DIM).reshape(RET_H // 2, 1, LANES)
    lgh = jnp.broadcast_to(log_gamma.reshape(RET_H // 2, 2, 1), (RET_H // 2, 2, LANES))
    cos_p, sin_p = _rope_tables(inv_lanes, t, 0)
    cos_s, sin_s = _rope_cols(jnp.broadcast_to(inv[:, None], (half, LANES)), ts, past_len)
    lg_rows = jnp.broadcast_to(log_gamma[:, None, None], (RET_H, SUBLANES, LANES))

    ckt = jnp.transpose(cache_fox_k, (0, 1, 3, 4, 2))
    cvt = jnp.transpose(cache_fox_v, (0, 1, 3, 4, 2))
    clt = jnp.swapaxes(cache_fox_logf, 2, 3)

    pad8 = lambda z: jnp.pad(z.reshape(db, ts, -1), ((0, 0), (0, SUBLANES - ts), (0, 0)))

    yp = x_prompt.reshape(nb * t, D_MODEL)
    ys = x_sample.reshape(db * ts, D_MODEL)
    outs = [[] for _ in range(12)]
    for l in range(depth):
        lw = _layer_weights(l, ln_mix_g, w_in, fox_qn_g, fox_kn_g, fox_f_b, rwkv_mu, rwkv_w0, rwkv_w2, rwkv_a0,
                            rwkv_a2, rwkv_g2, rwkv_k_k, rwkv_k_a, rwkv_r_k, rwkv_lnx_w, rwkv_lnx_b, ret_gn_w,
                            w_out, ln_ffn_g, w_gate, w_up, w_down)
        fq, fk, fv, fl, p_rwkv, p_ret = _inproj(yp, lw['ln_mix_g'], lw['w_in'], 512)
        kt, vt, lft, qx, kx, vb = _foxprep_prompt(fq, fk, fl, fv, lw['qg'], lw['kg'], lw['fb'], nb, 256)
        fo = _foxattn(qx, kx, vb, nb, 512, 512)
        p3 = p_rwkv.reshape(nb, t, RWKV_PROJ)
        eo, r_t = _retention(p_ret.reshape(nb, t, RET_PROJ), cos_p, sin_p, lg, lgh, lw['gn'],
                             jnp.zeros((2 * nb, LANES, LANES), F32), 4, RET_CHUNK, RET_CHUNK)
        yp, s_t = _mix_and_ffn(
            yp, fo, p_rwkv, jnp.zeros((nb, RWKV_PROJ), F32), jnp.zeros((2 * nb, HEAD_DIM, LANES), F32),
            eo.reshape(nb * t, RET_W), lw, nb, t, 512, nb, 128)
        outs[0].append(kt); outs[1].append(vt); outs[2].append(lft)
        outs[3].append(_rwkv_state_out(s_t, nb)); outs[4].append(p3[:, -1]); outs[5].append(_ret_state_out(r_t, nb))
        fq, fk, fv, fl, p_rwkv, p_ret = _inproj(ys, lw['ln_mix_g'], lw['w_in'], 256)
        qn, kn, lf = _foxprep_sample(fq, fk, fl, lw['qg'], lw['kg'], lw['fb'], 256)
        fo = _foxsample(page_table, l, pad8(qn), pad8(kn), pad8(fv), pad8(lf), ckt, cvt, clt, ts)
        p3 = p_rwkv.reshape(db, ts, RWKV_PROJ)
        xt = jnp.transpose(p_ret.reshape(db, ts, 4, RET_H, HEAD_DIM), (1, 2, 3, 4, 0))
        gn_cols = jnp.broadcast_to(lw['gn'].reshape(RET_H, HEAD_DIM, 1), (RET_H, HEAD_DIM, LANES))
        eo, r_t = _retention_short(xt, cos_s, sin_s, lg_rows, gn_cols, jnp.transpose(state_ret[l], (1, 2, 3, 0)))
        eo = jnp.transpose(eo, (3, 0, 1, 2)).reshape(db * ts, RET_W)
        ys, s_t = _mix_and_ffn(
            ys, fo.reshape(db * ts, FOX_W), p_rwkv, state_rwkv_shift[l], jnp.transpose(state_rwkv[l], (1, 2, 3, 0)),
            eo, lw, db, ts, 256, None, None)
        outs[6].append(kn); outs[7].append(fv); outs[8].append(lf[:, :FOX_H])
        outs[9].append(jnp.transpose(s_t, (3, 0, 1, 2))); outs[10].append(p3[:, -1])
        outs[11].append(jnp.transpose(r_t, (3, 0, 1, 2)))

    n_pp = nb * t // PAGE_SIZE
    st = lambda i: jnp.stack(outs[i])
    page_rows = lambda z: jnp.transpose(z.reshape(depth, n_pp, FOX_H, HEAD_DIM, PAGE_SIZE), (0, 1, 4, 2, 3))
    return (yp.reshape(nb, t, D_MODEL), ys.reshape(db, ts, D_MODEL),
            page_rows(st(0)), page_rows(st(1)), jnp.swapaxes(st(2), 2, 3),
            st(3), st(4), st(5),
            st(6).reshape(depth, db, ts, FOX_H, HEAD_DIM),
            st(7).reshape(depth, db, ts, FOX_H, HEAD_DIM),
            st(8).reshape(depth, db, ts, FOX_H),
            st(9), st(10), st(11))
```

```python
import functools

import jax
import jax.numpy as jnp
import numpy as np
from jax import lax
from jax.experimental import pallas as pl
from jax.experimental.pallas import tpu as pltpu

F32 = jnp.float32
BF16 = jnp.bfloat16

LANES = 128
SUBLANES = 8
VMEM_LIMIT = 56 * 1024 * 1024

D_MODEL = 1024
HEAD_DIM = 64
FOX_H = 8
RWKV_H = 4
RET_H = 4
FOX_W = FOX_H * HEAD_DIM
RWKV_W = RWKV_H * HEAD_DIM
RET_W = RET_H * HEAD_DIM
RWKV_LORA_W = 32
RWKV_LORA_A = 32
RWKV_LORA_G = 64
RWKV_PROJ = 3 * RWKV_W + RWKV_LORA_W + RWKV_LORA_A + RWKV_LORA_G
RET_PROJ = 4 * RET_W
D_FF = 2816
PAGE_SIZE = 128
RET_CHUNK = 128
ROPE_BASE = 10000.0
NORM_EPS = 1e-6
RWKV_GN_EPS = 64e-5
RET_GN_EPS = 1e-5
NEG_BIG = -1e30
HEADS_PER_TILE = LANES // HEAD_DIM


def _cparams(sem):
    return pltpu.CompilerParams(dimension_semantics=sem, vmem_limit_bytes=VMEM_LIMIT)


def _resident(shape):
    nd = len(shape)
    return pl.BlockSpec(shape, lambda *_: (0,) * nd, pipeline_mode=pl.Buffered(1))


def _bdot(a, b):
    return jnp.dot(a.astype(BF16), b.astype(BF16), preferred_element_type=F32)


def _bdot_nt(a, b):
    return lax.dot_general(a.astype(BF16), b.astype(BF16), (((1,), (1,)), ((), ())),
                           preferred_element_type=F32)


def _split2(x):
    hi = x.astype(BF16)
    lo = (x - hi.astype(F32)).astype(BF16)
    return hi, lo


def _split3(x):
    hi = x.astype(BF16)
    r = x - hi.astype(F32)
    mid = r.astype(BF16)
    lo = (r - mid.astype(F32)).astype(BF16)
    return hi, mid, lo


def _iota(shape, axis):
    return lax.broadcasted_iota(jnp.int32, shape, axis)


def _seg_ones():
    return (_iota((LANES, LANES), 0) // HEAD_DIM == _iota((LANES, LANES), 1) // HEAD_DIM).astype(BF16)


def _seg_sum(x, e):
    n, nblk = x.shape[0], x.shape[-1] // LANES
    stacked = x if nblk == 1 else jnp.concatenate([x[:, c * LANES:(c + 1) * LANES] for c in range(nblk)], axis=0)
    hi, lo = _split2(stacked)
    r = jnp.dot(jnp.concatenate([hi, lo], axis=0), e, preferred_element_type=F32)
    r = r[:n * nblk] + r[n * nblk:]
    return r if nblk == 1 else jnp.concatenate([r[c * n:(c + 1) * n] for c in range(nblk)], axis=-1)


def _exact_dot(x, m01, left):
    parts = _split3(x)
    n, w = x.shape
    if left:
        r = jnp.dot(m01, jnp.concatenate(parts, axis=1), preferred_element_type=F32)
        return r[:, :w] + r[:, w:2 * w] + r[:, 2 * w:]
    r = jnp.dot(jnp.concatenate(parts, axis=0), m01, preferred_element_type=F32)
    return r[:n] + r[n:2 * n] + r[2 * n:]


def _sigmoid(x):
    return 1.0 / (1.0 + jnp.exp(-x))


def _softplus(x):
    return jnp.maximum(x, 0.0) + jnp.log1p(jnp.exp(-jnp.abs(x)))


def _inproj_kernel(x_ref, g_ref, wq_ref, wk_ref, wv_ref, wl_ref, wr_ref, we_ref,
                   oq_ref, ok_ref, ov_ref, ol_ref, or_ref, oe_ref):
    x = x_ref[...]
    h = x * lax.rsqrt(jnp.mean(x * x, -1, keepdims=True) + NORM_EPS) * g_ref[...]
    hb = h.astype(BF16)
    for w_ref, o_ref in ((wq_ref, oq_ref), (wk_ref, ok_ref), (wv_ref, ov_ref), (wl_ref, ol_ref),
                         (wr_ref, or_ref), (we_ref, oe_ref)):
        o_ref[...] = jnp.dot(hb, w_ref[...], preferred_element_type=F32)


def _inproj(x, g, ws, tm):
    n = x.shape[0]
    widths = [w.shape[1] for w in ws]
    row = lambda wd: pl.BlockSpec((tm, wd), lambda i: (i, 0))
    return pl.pallas_call(
        _inproj_kernel,
        grid=(n // tm,),
        in_specs=[row(D_MODEL), _resident((1, D_MODEL))] + [_resident(w.shape) for w in ws],
        out_specs=[row(wd) for wd in widths],
        out_shape=[jax.ShapeDtypeStruct((n, wd), F32) for wd in widths],
        compiler_params=_cparams(("parallel",)),
        name="inproj",
    )(x, g, *ws)


def _fox_norms(q_ref, k_ref, fl_ref, qg_ref, kg_ref, fb_ref):
    e = _seg_ones()

    def hnorm(x, g):
        ms = _seg_sum(x * x, e) * (1.0 / HEAD_DIM)
        return x * lax.rsqrt(ms + NORM_EPS) * g

    qn = hnorm(q_ref[...], qg_ref[...]) * (HEAD_DIM ** -0.5)
    kn = hnorm(k_ref[...], kg_ref[...])
    z = fl_ref[...] + fb_ref[...]
    lf = jnp.minimum(z, 0.0) - jnp.log1p(jnp.exp(-jnp.abs(z)))
    return qn, kn, lf


def _foxprep_sample_kernel(q_ref, k_ref, fl_ref, qg_ref, kg_ref, fb_ref, qn_ref, kn_ref, lf_ref):
    qn_ref[...], kn_ref[...], lf_ref[...] = _fox_norms(q_ref, k_ref, fl_ref, qg_ref, kg_ref, fb_ref)


def _foxprep_prompt_kernel(q_ref, k_ref, fl_ref, qg_ref, kg_ref, fb_ref, v_ref,
                           kt_ref, vt_ref, lft_ref, qx_ref, kx_ref, vb_ref, carry_ref, *, tm):
    qn, kn, lf = _fox_norms(q_ref, k_ref, fl_ref, qg_ref, kg_ref, fb_ref)
    v = v_ref[...]
    vb_ref[...] = v.astype(BF16)
    for pg in range(tm // PAGE_SIZE):
        rows = slice(pg * PAGE_SIZE, (pg + 1) * PAGE_SIZE)
        kt_ref[pg] = kn[rows, :].T
        vt_ref[pg] = v[rows, :].T
        lft_ref[pg] = lf[rows, :].T[:FOX_H, :]

    @pl.when(pl.program_id(1) == 0)
    def _():
        carry_ref[...] = jnp.zeros_like(carry_ref)

    tri = (_iota((tm, tm), 0) >= _iota((tm, tm), 1)).astype(BF16)
    c = _exact_dot(lf, tri, left=True) + carry_ref[...]
    carry_ref[...] = c[tm - 1:tm, :]

    lane = _iota((tm, LANES), 1)
    j = lane % HEAD_DIM
    for h in range(FOX_H):
        hp, hh = divmod(h, HEADS_PER_TILE)
        c_hi, c_mid, c_lo = (p.astype(F32) for p in _split3(jnp.broadcast_to(c[:, h:h + 1], (tm, LANES))))
        ext_q = jnp.where(j == 0, c_hi, jnp.where(j == 1, c_mid, jnp.where(j == 2, c_lo, jnp.where(j < 6, 1.0, 0.0))))
        ext_k = jnp.where(j < 3, 1.0, jnp.where(j == 3, -c_hi, jnp.where(j == 4, -c_mid, jnp.where(j == 5, -c_lo, 0.0))))
        own = lane // HEAD_DIM == hh
        pair = slice(hp * LANES, (hp + 1) * LANES)
        tile = slice(h * LANES, (h + 1) * LANES)
        qx_ref[:, tile] = jnp.where(own, qn[:, pair], ext_q).astype(BF16)
        kx_ref[:, tile] = jnp.where(own, kn[:, pair], ext_k).astype(BF16)


def _foxprep_sample(fq, fk, fl, qg, kg, fb, tm):
    n = fq.shape[0]
    row = lambda wd: pl.BlockSpec((tm, wd), lambda i: (i, 0))
    return pl.pallas_call(
        _foxprep_sample_kernel,
        grid=(n // tm,),
        in_specs=[row(FOX_W), row(FOX_W), row(LANES), _resident((1, FOX_W)), _resident((1, FOX_W)),
                  _resident((1, LANES))],
        out_specs=[row(FOX_W), row(FOX_W), row(LANES)],
        out_shape=[jax.ShapeDtypeStruct((n, FOX_W), F32), jax.ShapeDtypeStruct((n, FOX_W), F32),
                   jax.ShapeDtypeStruct((n, LANES), F32)],
        compiler_params=_cparams(("parallel",)),
        name="foxprep_sample",
    )(fq, fk, fl, qg, kg, fb)


def _foxprep_prompt(fq, fk, fl, fv, qg, kg, fb, nseq, tm):
    n = fq.shape[0]
    nt = n // nseq // tm
    ppt = tm // PAGE_SIZE
    n_pp = n // PAGE_SIZE
    row = lambda wd: pl.BlockSpec((tm, wd), lambda b, j: (b * nt + j, 0))
    pages = lambda r: pl.BlockSpec((ppt, r, PAGE_SIZE), lambda b, j: (b * nt + j, 0, 0))
    return pl.pallas_call(
        functools.partial(_foxprep_prompt_kernel, tm=tm),
        grid=(nseq, nt),
        in_specs=[row(FOX_W), row(FOX_W), row(LANES), _resident((1, FOX_W)), _resident((1, FOX_W)),
                  _resident((1, LANES)), row(FOX_W)],
        out_specs=[pages(FOX_W), pages(FOX_W), pages(FOX_H), row(FOX_H * LANES), row(FOX_H * LANES), row(FOX_W)],
        out_shape=[jax.ShapeDtypeStruct((n_pp, FOX_W, PAGE_SIZE), F32), jax.ShapeDtypeStruct((n_pp, FOX_W, PAGE_SIZE), F32),
                   jax.ShapeDtypeStruct((n_pp, FOX_H, PAGE_SIZE), F32),
                   jax.ShapeDtypeStruct((n, FOX_H * LANES), BF16), jax.ShapeDtypeStruct((n, FOX_H * LANES), BF16),
                   jax.ShapeDtypeStruct((n, FOX_W), BF16)],
        scratch_shapes=[pltpu.VMEM((1, LANES), F32)],
        compiler_params=_cparams(("parallel", "arbitrary")),
        name="foxprep_prompt",
    )(fq, fk, fl, qg, kg, fb, fv)


def _foxattn_kernel(qx_ref, kx_ref, vb_ref, o_ref, *, tq):
    qi = pl.program_id(2)
    q = [qx_ref[:, hh * LANES:(hh + 1) * LANES] for hh in range(HEADS_PER_TILE)]

    def block(off, carry, diagonal):
        vb = vb_ref[pl.ds(off, tq), :]
        stats, ps = [], []
        for hh in range(HEADS_PER_TILE):
            m, l, _ = carry[hh]
            kb = kx_ref[pl.ds(off, tq), hh * LANES:(hh + 1) * LANES]
            s = lax.dot_general(q[hh], kb, (((1,), (1,)), ((), ())), preferred_element_type=F32)
            if diagonal:
                s = jnp.where(_iota((tq, tq), 1) <= _iota((tq, tq), 0), s, NEG_BIG)
            m_new = jnp.maximum(m, jnp.max(s, axis=-1, keepdims=True))
            alpha = jnp.exp(m - m_new)
            p = jnp.exp(s - m_new)
            stats.append((m_new, alpha, alpha * l + jnp.sum(p, axis=-1, keepdims=True)))
            ps.append(p.astype(BF16))
        pv = jnp.dot(jnp.concatenate(ps, axis=0), vb, preferred_element_type=F32)
        return tuple((m_new, l, alpha * carry[hh][2] + pv[hh * tq:(hh + 1) * tq])
                     for hh, (m_new, alpha, l) in enumerate(stats))

    init = tuple((jnp.full((tq, 1), NEG_BIG, F32), jnp.zeros((tq, 1), F32), jnp.zeros((tq, LANES), F32))
                 for _ in range(HEADS_PER_TILE))
    carry = lax.fori_loop(0, qi, lambda j, c: block(pl.multiple_of(j * tq, tq), c, False), init)
    carry = block(pl.multiple_of(qi * tq, tq), carry, True)
    outs = [acc / l for _, l, acc in carry]
    o_ref[...] = jnp.where(_iota((tq, LANES), 1) // HEAD_DIM == 0, outs[0], outs[1])


def _foxattn(qx, kx, vb, nseq, tq):
    n = qx.shape[0]
    t = n // nseq
    nq = t // tq
    npair = FOX_H // HEADS_PER_TILE
    pair_w = HEADS_PER_TILE * LANES
    return pl.pallas_call(
        functools.partial(_foxattn_kernel, tq=tq),
        grid=(nseq, npair, nq),
        in_specs=[
            pl.BlockSpec((tq, pair_w), lambda b, h, i: (b * nq + i, h)),
            pl.BlockSpec((t, pair_w), lambda b, h, i: (b, h)),
            pl.BlockSpec((t, LANES), lambda b, h, i: (b, h)),
        ],
        out_specs=pl.BlockSpec((tq, LANES), lambda b, h, i: (b * nq + i, h)),
        out_shape=jax.ShapeDtypeStruct((n, FOX_W), F32),
        compiler_params=_cparams(("parallel", "parallel", "arbitrary")),
        name="foxattn",
    )(qx, kx, vb)


def _foxsample_kernel(pt_ref, q_ref, kn_ref, vn_ref, lfn_ref, *rest, n_pages, t_new):
    del pt_ref
    kp = rest[:n_pages]
    vp = rest[n_pages:2 * n_pages]
    lp = rest[2 * n_pages:3 * n_pages]
    o_ref = rest[3 * n_pages]
    nrow = t_new * FOX_H

    rep = lambda x: jnp.concatenate([x] * t_new, axis=0)
    q = q_ref[...]
    hmask = _iota((FOX_H, FOX_W), 1) // HEAD_DIM == _iota((FOX_H, FOX_W), 0)
    qbd = jnp.concatenate(
        [jnp.where(hmask, jnp.broadcast_to(q[t:t + 1, :], (FOX_H, FOX_W)), 0.0) for t in range(t_new)],
        axis=0).astype(BF16)

    lfn = lfn_ref[...]
    diag8 = _iota((FOX_H, LANES), 1) == _iota((FOX_H, LANES), 0)
    cn_cols = []
    run = jnp.zeros((1, LANES), F32)
    for t in range(t_new):
        run = run + lfn[t:t + 1, :]
        cn_cols.append(jnp.sum(jnp.where(diag8, jnp.broadcast_to(run, (FOX_H, LANES)), 0.0), axis=-1, keepdims=True))
    cn = jnp.concatenate(cn_cols, axis=0)

    zeros_tail = jnp.zeros((PAGE_SIZE - SUBLANES, FOX_W), F32)
    key = _iota((nrow, PAGE_SIZE), 1)
    trow = _iota((nrow, PAGE_SIZE), 0) // FOX_H
    ckey = jnp.zeros((nrow, PAGE_SIZE), F32)
    for j in range(t_new):
        ckey = jnp.where(key == j, rep(cn_cols[j]), ckey)
    s_new = _bdot_nt(qbd, jnp.concatenate([kn_ref[...], zeros_tail], axis=0)) + (cn - ckey)
    s_new = jnp.where(key <= trow, s_new, NEG_BIG)

    lf_all = jnp.concatenate([lp[i][...] for i in range(n_pages)], axis=0)
    upper = (_iota((PAGE_SIZE, PAGE_SIZE), 0) <= _iota((PAGE_SIZE, PAGE_SIZE), 1)).astype(BF16)
    cp_all = _exact_dot(lf_all, upper, left=False)
    before = [jnp.zeros((FOX_H, 1), F32)]
    for i in range(n_pages):
        before.append(before[-1] + cp_all[i * FOX_H:(i + 1) * FOX_H, PAGE_SIZE - 1:PAGE_SIZE])
    cq_abs = rep(before[n_pages]) + cn
    s_past = []
    for i in range(n_pages):
        cp = rep(cp_all[i * FOX_H:(i + 1) * FOX_H, :] + before[i])
        s_past.append(_bdot(qbd, kp[i][...].reshape(FOX_W, PAGE_SIZE)) + (cq_abs - cp))

    m = jnp.max(s_new, axis=-1, keepdims=True)
    for s in s_past:
        m = jnp.maximum(m, jnp.max(s, axis=-1, keepdims=True))
    p = jnp.exp(s_new - m)
    l = jnp.sum(p, axis=-1, keepdims=True)
    o = _bdot(p, jnp.concatenate([vn_ref[...], zeros_tail], axis=0))
    for i in range(n_pages):
        p = jnp.exp(s_past[i] - m)
        l = l + jnp.sum(p, axis=-1, keepdims=True)
        o = o + _bdot_nt(p, vp[i][...].reshape(FOX_W, PAGE_SIZE))
    o = o / l
    omask = _iota((nrow, FOX_W), 1) // HEAD_DIM == _iota((nrow, FOX_W), 0) % FOX_H
    o_ref[...] = jnp.sum(jnp.where(omask, o, 0.0).reshape(t_new, FOX_H, FOX_W), axis=1)


def _foxsample(page_table, layer, qn, kn, vn, lfn, cache_kt, cache_vt, cache_lft, t_new):
    db, n_pages = page_table.shape
    pt = page_table.reshape(-1)
    new = lambda wd: pl.BlockSpec((None, SUBLANES, wd), lambda b, pt: (b, 0, 0))

    def page(shape, i):
        nd = len(shape)
        return pl.BlockSpec((None, None) + shape, lambda b, pt: (layer, pt[b * n_pages + i]) + (0,) * nd)

    in_specs = [new(FOX_W), new(FOX_W), new(FOX_W), new(LANES)]
    in_specs += [page((FOX_H, HEAD_DIM, PAGE_SIZE), i) for i in range(n_pages)]
    in_specs += [page((FOX_H, HEAD_DIM, PAGE_SIZE), i) for i in range(n_pages)]
    in_specs += [page((FOX_H, PAGE_SIZE), i) for i in range(n_pages)]
    return pl.pallas_call(
        functools.partial(_foxsample_kernel, n_pages=n_pages, t_new=t_new),
        grid_spec=pltpu.PrefetchScalarGridSpec(
            num_scalar_prefetch=1,
            grid=(db,),
            in_specs=in_specs,
            out_specs=pl.BlockSpec((None, t_new, FOX_W), lambda b, pt: (b, 0, 0)),
        ),
        out_shape=jax.ShapeDtypeStruct((db, t_new, FOX_W), F32),
        compiler_params=_cparams(("arbitrary",)),
        name="foxsample",
    )(pt, qn, kn, vn, lfn, *([cache_kt] * n_pages), *([cache_vt] * n_pages), *([cache_lft] * n_pages))


def _rwkvprep_kernel(p_ref, init_ref, mu_ref, w0_ref, w2_ref, a0_ref, a2_ref, g2_ref, kk_ref, ka_ref,
                     r_out, w_out, k_out, v_out, a_out, b_out, g_out, *carry, tm, period):
    e = _seg_ones()
    p = p_ref[...]
    rolled = pltpu.roll(p, 1, 0)
    row = _iota((tm, RWKV_PROJ), 0)
    if period >= tm:
        carry_ref, = carry

        @pl.when(pl.program_id(1) == 0)
        def _():
            carry_ref[...] = init_ref[...]

        pp = jnp.where(row == 0, carry_ref[...], rolled)
        carry_ref[...] = p[tm - 1:tm, :]
    else:
        pp = jnp.where(row % period == 0, init_ref[...], rolled)
    xs = p + (pp - p) * mu_ref[...]
    r = xs[:, 0:RWKV_W]
    k = xs[:, RWKV_W:2 * RWKV_W]
    v = xs[:, 2 * RWKV_W:3 * RWKV_W]
    lo = xs[:, 3 * RWKV_W:]
    w = -_softplus(-(w0_ref[...] + _bdot(jnp.tanh(lo), w2_ref[...]))) - 0.5
    a = _sigmoid(a0_ref[...] + _bdot(lo, a2_ref[...]))
    kk = k * kk_ref[...]
    nrm = jnp.sqrt(_seg_sum(kk * kk, e))
    kk = kk / jnp.maximum(nrm, 1e-12)
    r_out[...] = r
    w_out[...] = jnp.exp(-jnp.exp(w))
    k_out[...] = k * (1.0 + (a - 1.0) * ka_ref[...])
    v_out[...] = v
    a_out[...] = -kk
    b_out[...] = kk * a
    g_out[...] = _bdot(_sigmoid(lo), g2_ref[...])


def _rwkvprep(p, shift, period, mu, w0, w2p, a0, a2p, g2p, k_k, k_a, tm):
    n = p.shape[0]
    nseq = n // period
    vec = _resident((1, RWKV_W))
    lora = _resident((LANES, RWKV_W))
    if period >= tm:
        nt = period // tm
        grid = (nseq, nt)
        row = lambda wd: pl.BlockSpec((tm, wd), lambda b, j: (b * nt + j, 0))
        init = shift[:, None, :]
        init_spec = pl.BlockSpec((None, 1, RWKV_PROJ), lambda b, j: (b, 0, 0))
        scratch = [pltpu.VMEM((1, RWKV_PROJ), F32)]
        sem = ("parallel", "arbitrary")
    else:
        grid = (n // tm,)
        row = lambda wd: pl.BlockSpec((tm, wd), lambda i: (i, 0))
        init = jnp.repeat(shift, period, axis=0)
        init_spec = row(RWKV_PROJ)
        scratch = []
        sem = ("parallel",)
    return pl.pallas_call(
        functools.partial(_rwkvprep_kernel, tm=tm, period=period),
        grid=grid,
        in_specs=[row(RWKV_PROJ), init_spec, _resident((1, RWKV_PROJ)), vec, lora, vec, lora, lora, vec, vec],
        out_specs=[row(RWKV_W)] * 7,
        out_shape=[jax.ShapeDtypeStruct((n, RWKV_W), F32)] * 7,
        scratch_shapes=scratch,
        compiler_params=_cparams(sem),
        name="rwkvprep",
    )(p, init, mu, w0, w2p, a0, a2p, g2p, k_k, k_a)


def _rwkvscan_kernel(r_ref, w_ref, k_ref, v_ref, a_ref, b_ref, s0_ref, y_ref, st_ref, s_scr, *, bblk, tblk):
    tb = pl.program_id(1)
    ng = HEADS_PER_TILE * bblk

    @pl.when(tb == 0)
    def _():
        s_scr[...] = s0_ref[...]

    e = _seg_ones()
    diag = (_iota((HEAD_DIM, LANES), 1) % HEAD_DIM == _iota((HEAD_DIM, LANES), 0)).astype(BF16)[None]
    half = (_iota((SUBLANES, LANES), 1) // HEAD_DIM == _iota((SUBLANES, LANES), 0)).astype(BF16)
    first = _iota((1, LANES), 1) < HEAD_DIM

    sub = min(SUBLANES, tblk)

    def chunk(c, carry):
        off = pl.multiple_of(c * sub, sub)
        tiles = [[ref[b, pl.ds(off, sub), :] for b in range(bblk)] for ref in (r_ref, w_ref, k_ref, v_ref, a_ref, b_ref)]

        def rows(kind, i):
            return jnp.concatenate(
                [tiles[kind][b][i:i + 1, hp * LANES:(hp + 1) * LANES][None] for b in range(bblk) for hp in range(2)],
                axis=0)

        vdiag = jnp.concatenate([rows(3, i).astype(BF16) * diag for i in range(sub)], axis=0)
        vcol = jnp.dot(vdiag.reshape(sub * ng * HEAD_DIM, LANES), e,
                       preferred_element_type=F32).reshape(sub * ng, HEAD_DIM, LANES)

        s = s_scr[...]
        srs = []
        for i in range(sub):
            sa = jnp.dot((s * rows(4, i)).reshape(ng * HEAD_DIM, LANES).astype(BF16), e,
                         preferred_element_type=F32).reshape(ng, HEAD_DIM, LANES)
            s = s * rows(1, i) + sa * rows(5, i) + vcol[i * ng:(i + 1) * ng] * rows(2, i)
            srs.append((s * rows(0, i)).astype(BF16))
        s_scr[...] = s

        sr = jnp.concatenate(srs, axis=0).reshape(sub * ng * HEAD_DIM, LANES)
        out = lax.dot_general(half, sr, (((1,), (1,)), ((), ())), preferred_element_type=F32)
        for b in range(bblk):
            ytile = []
            for i in range(sub):
                lanes = slice((i * bblk + b) * 2 * HEAD_DIM, (i * bblk + b + 1) * 2 * HEAD_DIM)
                h0, h1 = out[0:1, lanes], out[1:2, lanes]
                ytile.append(jnp.concatenate(
                    [jnp.where(first, h0, pltpu.roll(h1, HEAD_DIM, 1)),
                     jnp.where(first, pltpu.roll(h0, HEAD_DIM, 1), h1)], axis=1))
            y_ref[b, pl.ds(off, sub), :] = jnp.concatenate(ytile, axis=0)
        return carry

    lax.fori_loop(0, tblk // sub, chunk, 0)

    @pl.when(tb == pl.num_programs(1) - 1)
    def _():
        st_ref[...] = s_scr[...]


def _rwkvscan(r, w, k, v, a, b, s0, bblk, tblk):
    nb, t, _ = r.shape
    ng = HEADS_PER_TILE * bblk
    seq = pl.BlockSpec((bblk, tblk, RWKV_W), lambda i, j: (i, j, 0))
    st = pl.BlockSpec((ng, HEAD_DIM, LANES), lambda i, j: (i, 0, 0))
    return pl.pallas_call(
        functools.partial(_rwkvscan_kernel, bblk=bblk, tblk=tblk),
        grid=(nb // bblk, t // tblk),
        in_specs=[seq] * 6 + [st],
        out_specs=[seq, st],
        out_shape=[jax.ShapeDtypeStruct((nb, t, RWKV_W), F32), jax.ShapeDtypeStruct(s0.shape, F32)],
        scratch_shapes=[pltpu.VMEM((ng, HEAD_DIM, LANES), F32)],
        compiler_params=_cparams(("parallel", "arbitrary")),
        name="rwkvscan",
    )(r, w, k, v, a, b, s0)


def _rwkv_short_kernel(z_ref, s0_ref, y_ref, st_ref):
    nt = z_ref.shape[0]
    R, W, K, V, A, B = range(6)

    def value_rows(i, carry):
        base = pl.multiple_of(i * SUBLANES, SUBLANES)
        vt = [z_ref[t, V, pl.ds(base, SUBLANES), :] for t in range(nt)]
        yrows = [[] for _ in range(nt)]
        for j in range(SUBLANES):
            s = s0_ref[base + j]
            for t in range(nt):
                sa = jnp.sum(s * z_ref[t, A], axis=0, keepdims=True)
                s = s * z_ref[t, W] + sa * z_ref[t, B] + vt[t][j:j + 1, :] * z_ref[t, K]
                yrows[t].append(jnp.sum(s * z_ref[t, R], axis=0, keepdims=True))
            st_ref[base + j] = s
        for t in range(nt):
            y_ref[t, pl.ds(base, SUBLANES), :] = jnp.concatenate(yrows[t], axis=0)
        return carry

    lax.fori_loop(0, HEAD_DIM // SUBLANES, value_rows, 0)


def _rwkv_short(zt, s0):
    nt, _, nh, _, nb = zt.shape
    assert nb == LANES and nt <= SUBLANES
    st = pl.BlockSpec((None, HEAD_DIM, HEAD_DIM, LANES), lambda h: (h, 0, 0, 0))
    return pl.pallas_call(
        _rwkv_short_kernel,
        grid=(nh,),
        in_specs=[pl.BlockSpec((nt, 6, None, HEAD_DIM, LANES), lambda h: (0, 0, h, 0, 0)), st],
        out_specs=[pl.BlockSpec((nt, None, HEAD_DIM, LANES), lambda h: (0, h, 0, 0)), st],
        out_shape=[jax.ShapeDtypeStruct((nt, nh, HEAD_DIM, LANES), F32), jax.ShapeDtypeStruct(s0.shape, F32)],
        compiler_params=_cparams(("parallel",)),
        name="rwkv_short",
    )(zt, s0)


def _rope_kernel(inv_ref, cos_ref, sin_ref, *, pos0):
    t = cos_ref.shape[0]
    pos = (pos0 + _iota((t, LANES), 0)).astype(F32)
    ang = pos * inv_ref[...]
    first_half = _iota((t, LANES), 1) % HEAD_DIM < HEAD_DIM // 2
    cos_ref[...] = jnp.cos(ang)
    sin_ref[...] = jnp.where(first_half, -jnp.sin(ang), jnp.sin(ang))


def _rope_tables(inv_lanes, t, pos0):
    return pl.pallas_call(
        functools.partial(_rope_kernel, pos0=pos0),
        out_shape=[jax.ShapeDtypeStruct((t, LANES), F32)] * 2,
        name="rope",
    )(inv_lanes)


def _ret_kernel(x_ref, cos_ref, sin_ref, lg_ref, lgh_ref, gn_ref, r0_ref, o_ref, rt_ref, r_scr,
                *, bs):
    lp = RET_CHUNK
    npair = RET_H // HEADS_PER_TILE
    c = pl.program_id(1)

    @pl.when(c == 0)
    def _():
        r_scr[...] = r0_ref[...]

    lane = _iota((lp, LANES), 1)
    first_half = lane % HEAD_DIM < HEAD_DIM // 2
    cos = cos_ref[...]
    sin = sin_ref[...]

    def rope(x):
        swapped = jnp.where(first_half, pltpu.roll(x, LANES - HEAD_DIM // 2, 1), pltpu.roll(x, HEAD_DIM // 2, 1))
        return x * cos + swapped * sin

    idx = _iota((lp, 1), 0).astype(F32)
    diff = (_iota((lp, lp), 0) - _iota((lp, lp), 1)).astype(F32)
    same_head = _iota((LANES, LANES), 0) // HEAD_DIM == _iota((LANES, LANES), 1) // HEAD_DIM
    e = _seg_ones()
    dmask, cross, kdec, cdec = [], [], [], []
    for hp in range(npair):
        lg = lg_ref[hp]
        dmask.append(jnp.concatenate(
            [jnp.where(diff >= 0.0, jnp.exp(lgh_ref[hp, hh:hh + 1, :] * jnp.maximum(diff, 0.0)), 0.0)
             for hh in range(HEADS_PER_TILE)], axis=0))
        cross.append(jnp.exp(lg * (idx + 1.0)))
        kdec.append(jnp.exp(lg * (lp - 1.0 - idx)))
        cdec.append(jnp.exp(lg * float(lp)))

    chains = [(s, hp) for s in range(bs) for hp in range(npair)]
    os_ = []
    for s, hp in chains:
        col = lambda j: x_ref[s, :, (j * npair + hp) * LANES:(j * npair + hp + 1) * LANES]
        q = rope(col(0))
        k = rope(col(1)) * (HEAD_DIM ** -0.5)
        vb = col(2).astype(BF16)
        qh = jnp.concatenate([jnp.where(lane // HEAD_DIM == hh, q, 0.0) for hh in range(HEADS_PER_TILE)], axis=0)
        sc = _bdot_nt(qh, k) * dmask[hp]
        ov = jnp.dot(sc.astype(BF16), vb, preferred_element_type=F32)
        o = jnp.where(lane // HEAD_DIM == 0, ov[:lp], ov[lp:])
        r = r_scr[s * npair + hp]
        os_.append(o + _bdot(q, r) * cross[hp])
        upd = jnp.dot((k * kdec[hp]).T.astype(BF16), vb, preferred_element_type=F32)
        r_scr[s * npair + hp] = r * cdec[hp] + jnp.where(same_head, upd, 0.0)

    o_all = jnp.concatenate(os_, axis=0)
    xc = o_all - _seg_sum(o_all, e) * (1.0 / HEAD_DIM)
    yn = xc * lax.rsqrt(_seg_sum(xc * xc, e) * (1.0 / HEAD_DIM) + RET_GN_EPS)
    for i, (s, hp) in enumerate(chains):
        g = x_ref[s, :, (3 * npair + hp) * LANES:(3 * npair + hp + 1) * LANES]
        o_ref[s, :, hp * LANES:(hp + 1) * LANES] = (
            yn[i * lp:(i + 1) * lp] * gn_ref[:, hp * LANES:(hp + 1) * LANES] * (g * _sigmoid(g)))

    @pl.when(c == pl.num_programs(1) - 1)
    def _():
        rt_ref[...] = r_scr[...]


def _retention(proj, cos, sin, lg, lgh, gn, r0, bs):
    nb, t, _ = proj.shape
    l_in = RET_CHUNK
    nc = t // l_in
    npair = RET_H // HEADS_PER_TILE
    st = pl.BlockSpec((bs * npair, LANES, LANES), lambda b, c: (b, 0, 0))
    tab = pl.BlockSpec((l_in, LANES), lambda b, c: (c, 0))
    return pl.pallas_call(
        functools.partial(_ret_kernel, bs=bs),
        grid=(nb // bs, nc),
        in_specs=[pl.BlockSpec((bs, l_in, 4 * RET_W), lambda b, c: (b, c, 0)), tab, tab,
                  _resident(lg.shape), _resident(lgh.shape), _resident(gn.shape), st],
        out_specs=[pl.BlockSpec((bs, l_in, RET_W), lambda b, c: (b, c, 0)), st],
        out_shape=[jax.ShapeDtypeStruct((nb, t, RET_W), F32), jax.ShapeDtypeStruct(r0.shape, F32)],
        scratch_shapes=[pltpu.VMEM((bs * npair, LANES, LANES), F32)],
        compiler_params=_cparams(("parallel", "arbitrary")),
        name="retention",
    )(proj, cos, sin, lg, lgh, gn, r0)


def _rope_cols_kernel(inv_ref, cos_ref, sin_ref, *, pos0):
    for t in range(cos_ref.shape[0]):
        ang = float(pos0 + t) * inv_ref[...]
        cos_ref[t] = jnp.cos(ang)
        sin_ref[t] = jnp.sin(ang)


def _rope_cols(inv_cols, t, pos0):
    return pl.pallas_call(
        functools.partial(_rope_cols_kernel, pos0=pos0),
        out_shape=[jax.ShapeDtypeStruct((t, HEAD_DIM // 2, LANES), F32)] * 2,
        name="rope_cols",
    )(inv_cols)


def _ret_short_kernel(x_ref, cos_ref, sin_ref, lg_ref, gn_ref, r0_ref, o_ref, rt_ref, q_scr, k_scr):
    nt = x_ref.shape[0]
    half = HEAD_DIM // 2
    lg = lg_ref[0:1, :]
    gam = lambda n: jnp.exp(lg * float(n))

    def rope(x, t):
        x1, x2 = x[:half], x[half:]
        c, s = cos_ref[t], sin_ref[t]
        return jnp.concatenate([x1 * c - x2 * s, x2 * c + x1 * s], axis=0)

    q = [rope(x_ref[t, 0], t) for t in range(nt)]
    k = [rope(x_ref[t, 1], t) * (HEAD_DIM ** -0.5) for t in range(nt)]
    v = [x_ref[t, 2] for t in range(nt)]

    intra = []
    for t in range(nt):
        acc = None
        for t2 in range(t + 1):
            term = (jnp.sum(q[t] * k[t2], axis=0, keepdims=True) * gam(t - t2)) * v[t2]
            acc = term if acc is None else acc + term
        intra.append(acc)

    for t in range(nt):
        q_scr[t] = q[t]
        k_scr[t] = k[t] * gam(nt - 1 - t)
    decay_all = gam(nt)

    def key_rows(i, acc):
        base = pl.multiple_of(i * SUBLANES, SUBLANES)
        qb = [q_scr[t, pl.ds(base, SUBLANES), :] for t in range(nt)]
        kb = [k_scr[t, pl.ds(base, SUBLANES), :] for t in range(nt)]
        acc = list(acc)
        for j in range(SUBLANES):
            r = r0_ref[base + j]
            upd = r * decay_all
            for t in range(nt):
                acc[t] = acc[t] + qb[t][j:j + 1, :] * r
                upd = upd + kb[t][j:j + 1, :] * v[t]
            rt_ref[base + j] = upd
        return tuple(acc)

    cross = lax.fori_loop(0, HEAD_DIM // SUBLANES, key_rows,
                          tuple(jnp.zeros((HEAD_DIM, LANES), F32) for _ in range(nt)))

    for t in range(nt):
        o = intra[t] + cross[t] * gam(t + 1)
        mean = jnp.mean(o, axis=0, keepdims=True)
        xc = o - mean
        var = jnp.mean(xc * xc, axis=0, keepdims=True)
        g = x_ref[t, 3]
        o_ref[t] = xc * lax.rsqrt(var + RET_GN_EPS) * gn_ref[...] * (g * _sigmoid(g))


def _retention_short(xt, cos, sin, lg_rows, gn_cols, r0):
    nt, _, nh, _, nb = xt.shape
    assert nb == LANES and nt <= SUBLANES
    tab = _resident(cos.shape)
    return pl.pallas_call(
        _ret_short_kernel,
        grid=(nh,),
        in_specs=[pl.BlockSpec((nt, 4, None, HEAD_DIM, LANES), lambda h: (0, 0, h, 0, 0)), tab, tab,
                  pl.BlockSpec((None, SUBLANES, LANES), lambda h: (h, 0, 0)),
                  pl.BlockSpec((None, HEAD_DIM, LANES), lambda h: (h, 0, 0)),
                  pl.BlockSpec((None, HEAD_DIM, HEAD_DIM, LANES), lambda h: (h, 0, 0, 0))],
        out_specs=[pl.BlockSpec((nt, None, HEAD_DIM, LANES), lambda h: (0, h, 0, 0)),
                   pl.BlockSpec((None, HEAD_DIM, HEAD_DIM, LANES), lambda h: (h, 0, 0, 0))],
        out_shape=[jax.ShapeDtypeStruct((nt, nh, HEAD_DIM, LANES), F32), jax.ShapeDtypeStruct(r0.shape, F32)],
        scratch_shapes=[pltpu.VMEM((nt, HEAD_DIM, LANES), F32), pltpu.VMEM((nt, HEAD_DIM, LANES), F32)],
        compiler_params=_cparams(("parallel",)),
        name="retention_short",
    )(xt, cos, sin, lg_rows, gn_cols, r0)


def _outproj_kernel(x_ref, fo_ref, y_ref, r_ref, k_ref, v_ref, g_ref, eo_ref, lw_ref, lb_ref, rk_ref,
                    wf_ref, wr_ref, we_ref, o_ref):
    e = _seg_ones()
    y = y_ref[...]
    mean = _seg_sum(y, e) * (1.0 / HEAD_DIM)
    yc = y - mean
    var = _seg_sum(yc * yc, e) * (1.0 / HEAD_DIM)
    yn = yc * lax.rsqrt(var + RWKV_GN_EPS) * lw_ref[...] + lb_ref[...]
    v = v_ref[...]
    bonus = _seg_sum(r_ref[...] * k_ref[...] * rk_ref[...], e) * v
    ro = (yn + bonus) * g_ref[...]
    o_ref[...] = (x_ref[...] + _bdot(fo_ref[...], wf_ref[...]) + _bdot(ro, wr_ref[...])
                  + _bdot(eo_ref[...], we_ref[...]))


def _outproj(x, fo, y, r, k, v, g, eo, lw, lb, rk, wf, wr, we, tm):
    n = x.shape[0]
    row = lambda wd: pl.BlockSpec((tm, wd), lambda i: (i, 0))
    vec = _resident((1, RWKV_W))
    return pl.pallas_call(
        _outproj_kernel,
        grid=(n // tm,),
        in_specs=[row(D_MODEL), row(FOX_W)] + [row(RWKV_W)] * 6 + [vec, vec, vec,
                  _resident(wf.shape), _resident(wr.shape), _resident(we.shape)],
        out_specs=row(D_MODEL),
        out_shape=jax.ShapeDtypeStruct((n, D_MODEL), F32),
        compiler_params=_cparams(("parallel",)),
        name="outproj",
    )(x, fo, y, r, k, v, g, eo, lw, lb, rk, wf, wr, we)


def _ffn_kernel(x_ref, g_ref, wg_ref, wu_ref, wd_ref, o_ref):
    x = x_ref[...]
    h = (x * lax.rsqrt(jnp.mean(x * x, -1, keepdims=True) + NORM_EPS) * g_ref[...]).astype(BF16)
    gate = jnp.dot(h, wg_ref[...], preferred_element_type=F32)
    up = jnp.dot(h, wu_ref[...], preferred_element_type=F32)
    act = gate * _sigmoid(gate) * up
    o_ref[...] = x + _bdot(act, wd_ref[...])


def _ffn(x, g, wg, wu, wd, tm):
    n = x.shape[0]
    row = pl.BlockSpec((tm, D_MODEL), lambda i: (i, 0))
    return pl.pallas_call(
        _ffn_kernel,
        grid=(n // tm,),
        in_specs=[row, _resident((1, D_MODEL)), _resident(wg.shape), _resident(wu.shape), _resident(wd.shape)],
        out_specs=row,
        out_shape=jax.ShapeDtypeStruct((n, D_MODEL), F32),
        compiler_params=_cparams(("parallel",)),
        name="ffn",
    )(x, g, wg, wu, wd)


def _rwkv_state_out(s, nb):
    s = s.reshape(nb, RWKV_H // 2, HEAD_DIM, 2, HEAD_DIM).transpose(0, 1, 3, 2, 4)
    return s.reshape(nb, RWKV_H, HEAD_DIM, HEAD_DIM)


def _ret_state_out(r, nb):
    r = r.reshape(nb, RET_H // 2, LANES, LANES)
    return jnp.stack([r[:, :, :HEAD_DIM, :HEAD_DIM], r[:, :, HEAD_DIM:, HEAD_DIM:]], axis=2).reshape(
        nb, RET_H, HEAD_DIM, HEAD_DIM)


def _layer_weights(l, ln_mix_g, w_in, fox_qn_g, fox_kn_g, fox_f_b, rwkv_mu, rwkv_w0, rwkv_w2, rwkv_a0, rwkv_a2,
                   rwkv_g2, rwkv_k_k, rwkv_k_a, rwkv_r_k, rwkv_lnx_w, rwkv_lnx_b, ret_gn_w, w_out, ln_ffn_g,
                   w_gate, w_up, w_down):
    wi = w_in[l].astype(BF16)
    o_f = 3 * FOX_W
    o_r = o_f + FOX_H
    o_e = o_r + RWKV_PROJ
    pad_rows = lambda w, off: jnp.zeros((LANES, RWKV_W), BF16).at[off:off + w.shape[0]].set(w.astype(BF16))
    wo = w_out[l].astype(BF16)
    return dict(
        ln_mix_g=ln_mix_g[l][None],
        w_in=[wi[:, 0:FOX_W], wi[:, FOX_W:2 * FOX_W], wi[:, 2 * FOX_W:o_f],
              jnp.pad(wi[:, o_f:o_r], ((0, 0), (0, LANES - FOX_H))), wi[:, o_r:o_e], wi[:, o_e:]],
        qg=jnp.tile(fox_qn_g[l], FOX_H)[None], kg=jnp.tile(fox_kn_g[l], FOX_H)[None],
        fb=jnp.pad(fox_f_b[l], (0, LANES - FOX_H))[None],
        mu=rwkv_mu[l][None], w0=rwkv_w0[l][None], a0=rwkv_a0[l][None],
        w2=pad_rows(rwkv_w2[l], 0), a2=pad_rows(rwkv_a2[l], RWKV_LORA_W),
        g2=pad_rows(rwkv_g2[l], RWKV_LORA_W + RWKV_LORA_A),
        k_k=rwkv_k_k[l][None], k_a=rwkv_k_a[l][None], r_k=rwkv_r_k[l].reshape(1, RWKV_W),
        lnx_w=rwkv_lnx_w[l][None], lnx_b=rwkv_lnx_b[l][None], gn=ret_gn_w[l][None],
        wo_f=wo[:FOX_W], wo_r=wo[FOX_W:FOX_W + RWKV_W], wo_e=wo[FOX_W + RWKV_W:],
        ln_ffn_g=ln_ffn_g[l][None],
        w_gate=w_gate[l].astype(BF16), w_up=w_up[l].astype(BF16), w_down=w_down[l].astype(BF16),
    )


def _mix_and_ffn(x2, fo, p_rwkv, shift, s0, eo, lw, nb, t, tm, bblk, tblk):
    r, w, k, v, a, b, g = _rwkvprep(p_rwkv, shift, t, lw['mu'], lw['w0'], lw['w2'], lw['a0'], lw['a2'], lw['g2'],
                                    lw['k_k'], lw['k_a'], tm)
    if s0.ndim == 4:
        zt = jnp.stack([r, w, k, v, a, b], axis=1).reshape(nb, t, 6, RWKV_H, HEAD_DIM)
        y, s_t = _rwkv_short(jnp.transpose(zt, (1, 2, 3, 4, 0)), s0)
        y = jnp.transpose(y, (3, 0, 1, 2))
    else:
        seq = lambda z: z.reshape(nb, t, RWKV_W)
        y, s_t = _rwkvscan(seq(r), seq(w), seq(k), seq(v), seq(a), seq(b), s0, bblk, tblk)
    x2 = _outproj(x2, fo, y.reshape(nb * t, RWKV_W), r, k, v, g, eo, lw['lnx_w'], lw['lnx_b'], lw['r_k'],
                  lw['wo_f'], lw['wo_r'], lw['wo_e'], tm)
    x2 = _ffn(x2, lw['ln_ffn_g'], lw['w_gate'], lw['w_up'], lw['w_down'], tm)
    return x2, s_t


def kernel(x_prompt, x_sample, cache_fox_k, cache_fox_v, cache_fox_logf, state_rwkv, state_rwkv_shift, state_ret,
           page_table, ln_mix_g, w_in, fox_qn_g, fox_kn_g, fox_f_b, rwkv_mu, rwkv_w0, rwkv_w2, rwkv_a0, rwkv_a2,
           rwkv_g2, rwkv_k_k, rwkv_k_a, rwkv_r_k, rwkv_lnx_w, rwkv_lnx_b, ret_gn_w, w_out, ln_ffn_g, w_gate,
           w_up, w_down):
    nb, t, _ = x_prompt.shape
    db, ts, _ = x_sample.shape
    depth = w_in.shape[0]
    n_pages = page_table.shape[1]
    past_len = n_pages * PAGE_SIZE
    n_pool = cache_fox_k.shape[1]

    half = HEAD_DIM // 2
    inv = ROPE_BASE ** (-jnp.arange(half, dtype=F32) / half)
    inv_lanes = jnp.tile(inv, LANES // half)[None]
    log_gamma = jnp.log1p(-jnp.exp2(-5.0 - jnp.arange(RET_H, dtype=F32)))
    lg = jnp.repeat(log_gamma, HEAD_DIM).reshape(RET_H // 2, 1, LANES)
    lgh = jnp.broadcast_to(log_gamma.reshape(RET_H // 2, 2, 1), (RET_H // 2, 2, LANES))
    cos_p, sin_p = _rope_tables(inv_lanes, t, 0)
    cos_s, sin_s = _rope_cols(jnp.broadcast_to(inv[:, None], (half, LANES)), ts, past_len)
    lg_rows = jnp.broadcast_to(log_gamma[:, None, None], (RET_H, SUBLANES, LANES))

    ckt = jnp.transpose(cache_fox_k, (0, 1, 3, 4, 2))
    cvt = jnp.transpose(cache_fox_v, (0, 1, 3, 4, 2))
    clt = jnp.swapaxes(cache_fox_logf, 2, 3)

    pad8 = lambda z: jnp.pad(z.reshape(db, ts, -1), ((0, 0), (0, SUBLANES - ts), (0, 0)))

    yp = x_prompt.reshape(nb * t, D_MODEL)
    ys = x_sample.reshape(db * ts, D_MODEL)
    outs = [[] for _ in range(12)]
    for l in range(depth):
        lw = _layer_weights(l, ln_mix_g, w_in, fox_qn_g, fox_kn_g, fox_f_b, rwkv_mu, rwkv_w0, rwkv_w2, rwkv_a0,
                            rwkv_a2, rwkv_g2, rwkv_k_k, rwkv_k_a, rwkv_r_k, rwkv_lnx_w, rwkv_lnx_b, ret_gn_w,
                            w_out, ln_ffn_g, w_gate, w_up, w_down)
        fq, fk, fv, fl, p_rwkv, p_ret = _inproj(yp, lw['ln_mix_g'], lw['w_in'], 512)
        kt, vt, lft, qx, kx, vb = _foxprep_prompt(fq, fk, fl, fv, lw['qg'], lw['kg'], lw['fb'], nb, 256)
        fo = _foxattn(qx, kx, vb, nb, 512)
        p3 = p_rwkv.reshape(nb, t, RWKV_PROJ)
        eo, r_t = _retention(p_ret.reshape(nb, t, RET_PROJ), cos_p, sin_p, lg, lgh, lw['gn'],
                             jnp.zeros((2 * nb, LANES, LANES), F32), 4)
        yp, s_t = _mix_and_ffn(
            yp, fo, p_rwkv, jnp.zeros((nb, RWKV_PROJ), F32), jnp.zeros((2 * nb, HEAD_DIM, LANES), F32),
            eo.reshape(nb * t, RET_W), lw, nb, t, 512, nb, 128)
        outs[0].append(kt); outs[1].append(vt); outs[2].append(lft)
        outs[3].append(_rwkv_state_out(s_t, nb)); outs[4].append(p3[:, -1]); outs[5].append(_ret_state_out(r_t, nb))
        fq, fk, fv, fl, p_rwkv, p_ret = _inproj(ys, lw['ln_mix_g'], lw['w_in'], 256)
        qn, kn, lf = _foxprep_sample(fq, fk, fl, lw['qg'], lw['kg'], lw['fb'], 256)
        fo = _foxsample(page_table, l, pad8(qn), pad8(kn), pad8(fv), pad8(lf), ckt, cvt, clt, ts)
        p3 = p_rwkv.reshape(db, ts, RWKV_PROJ)
        xt = jnp.transpose(p_ret.reshape(db, ts, 4, RET_H, HEAD_DIM), (1, 2, 3, 4, 0))
        gn_cols = jnp.broadcast_to(lw['gn'].reshape(RET_H, HEAD_DIM, 1), (RET_H, HEAD_DIM, LANES))
        eo, r_t = _retention_short(xt, cos_s, sin_s, lg_rows, gn_cols, jnp.transpose(state_ret[l], (1, 2, 3, 0)))
        eo = jnp.transpose(eo, (3, 0, 1, 2)).reshape(db * ts, RET_W)
        ys, s_t = _mix_and_ffn(
            ys, fo.reshape(db * ts, FOX_W), p_rwkv, state_rwkv_shift[l], jnp.transpose(state_rwkv[l], (1, 2, 3, 0)),
            eo, lw, db, ts, 256, None, None)
        outs[6].append(kn); outs[7].append(fv); outs[8].append(lf[:, :FOX_H])
        outs[9].append(jnp.transpose(s_t, (3, 0, 1, 2))); outs[10].append(p3[:, -1])
        outs[11].append(jnp.transpose(r_t, (3, 0, 1, 2)))

    n_pp = nb * t // PAGE_SIZE
    st = lambda i: jnp.stack(outs[i])
    page_rows = lambda z: jnp.transpose(z.reshape(depth, n_pp, FOX_H, HEAD_DIM, PAGE_SIZE), (0, 1, 4, 2, 3))
    return (yp.reshape(nb, t, D_MODEL), ys.reshape(db, ts, D_MODEL),
            page_rows(st(0)), page_rows(st(1)), jnp.swapaxes(st(2), 2, 3),
            st(3), st(4), st(5),
            st(6).reshape(depth, db, ts, FOX_H, HEAD_DIM),
            st(7).reshape(depth, db, ts, FOX_H, HEAD_DIM),
            st(8).reshape(depth, db, ts, FOX_H),
            st(9), st(10), st(11))
```

```python
import functools

import jax
import jax.numpy as jnp
import numpy as np
from jax import lax
from jax.experimental import pallas as pl
from jax.experimental.pallas import tpu as pltpu

F32 = jnp.float32
BF16 = jnp.bfloat16

LANES = 128
SUBLANES = 8
VMEM_LIMIT = 56 * 1024 * 1024

D_MODEL = 1024
HEAD_DIM = 64
FOX_H = 8
RWKV_H = 4
RET_H = 4
FOX_W = FOX_H * HEAD_DIM
RWKV_W = RWKV_H * HEAD_DIM
RET_W = RET_H * HEAD_DIM
RWKV_LORA_W = 32
RWKV_LORA_A = 32
RWKV_LORA_G = 64
RWKV_PROJ = 3 * RWKV_W + RWKV_LORA_W + RWKV_LORA_A + RWKV_LORA_G
RET_PROJ = 4 * RET_W
D_FF = 2816
PAGE_SIZE = 128
RET_CHUNK = 128
ROPE_BASE = 10000.0
NORM_EPS = 1e-6
RWKV_GN_EPS = 64e-5
RET_GN_EPS = 1e-5
NEG_BIG = -1e30
HEADS_PER_TILE = LANES // HEAD_DIM

ROWS_PROMPT = 512
ROWS_SAMPLE = 256
ROWS_FOXPREP = 256
ATTN_BLOCK = 512
SCAN_STEPS = 128
RET_SEQS = 4


def _cparams(sem):
    return pltpu.CompilerParams(dimension_semantics=sem, vmem_limit_bytes=VMEM_LIMIT)


def _resident(shape):
    nd = len(shape)
    return pl.BlockSpec(shape, lambda *_: (0,) * nd, pipeline_mode=pl.Buffered(1))


def _bdot(a, b):
    return jnp.dot(a.astype(BF16), b.astype(BF16), preferred_element_type=F32)


def _bdot_nt(a, b):
    return lax.dot_general(a.astype(BF16), b.astype(BF16), (((1,), (1,)), ((), ())),
                           preferred_element_type=F32)


def _split2(x):
    hi = x.astype(BF16)
    lo = (x - hi.astype(F32)).astype(BF16)
    return hi, lo


def _split3(x):
    hi = x.astype(BF16)
    r = x - hi.astype(F32)
    mid = r.astype(BF16)
    lo = (r - mid.astype(F32)).astype(BF16)
    return hi, mid, lo


def _iota(shape, axis):
    return lax.broadcasted_iota(jnp.int32, shape, axis)


def _seg_ones():
    return (_iota((LANES, LANES), 0) // HEAD_DIM == _iota((LANES, LANES), 1) // HEAD_DIM).astype(BF16)


def _seg_sum(x, e):
    n, nblk = x.shape[0], x.shape[-1] // LANES
    stacked = x if nblk == 1 else jnp.concatenate([x[:, c * LANES:(c + 1) * LANES] for c in range(nblk)], axis=0)
    hi, lo = _split2(stacked)
    r = jnp.dot(jnp.concatenate([hi, lo], axis=0), e, preferred_element_type=F32)
    r = r[:n * nblk] + r[n * nblk:]
    return r if nblk == 1 else jnp.concatenate([r[c * n:(c + 1) * n] for c in range(nblk)], axis=-1)


def _exact_dot(x, m01, left):
    parts = _split3(x)
    n, w = x.shape
    if left:
        r = jnp.dot(m01, jnp.concatenate(parts, axis=1), preferred_element_type=F32)
        return r[:, :w] + r[:, w:2 * w] + r[:, 2 * w:]
    r = jnp.dot(jnp.concatenate(parts, axis=0), m01, preferred_element_type=F32)
    return r[:n] + r[n:2 * n] + r[2 * n:]


def _sigmoid(x):
    return 1.0 / (1.0 + jnp.exp(-x))


def _softplus(x):
    return jnp.maximum(x, 0.0) + jnp.log1p(jnp.exp(-jnp.abs(x)))


def _inproj_kernel(x_ref, g_ref, wq_ref, wk_ref, wv_ref, wl_ref, wr_ref, we_ref,
                   oq_ref, ok_ref, ov_ref, ol_ref, or_ref, oe_ref):
    x = x_ref[...]
    h = x * lax.rsqrt(jnp.mean(x * x, -1, keepdims=True) + NORM_EPS) * g_ref[...]
    hb = h.astype(BF16)
    for w_ref, o_ref in ((wq_ref, oq_ref), (wk_ref, ok_ref), (wv_ref, ov_ref), (wl_ref, ol_ref),
                         (wr_ref, or_ref), (we_ref, oe_ref)):
        o_ref[...] = jnp.dot(hb, w_ref[...], preferred_element_type=F32)


def _inproj(x, g, ws, tm):
    n = x.shape[0]
    widths = [w.shape[1] for w in ws]
    row = lambda wd: pl.BlockSpec((tm, wd), lambda i: (i, 0))
    return pl.pallas_call(
        _inproj_kernel,
        grid=(n // tm,),
        in_specs=[row(D_MODEL), _resident((1, D_MODEL))] + [_resident(w.shape) for w in ws],
        out_specs=[row(wd) for wd in widths],
        out_shape=[jax.ShapeDtypeStruct((n, wd), F32) for wd in widths],
        compiler_params=_cparams(("parallel",)),
        name="inproj",
    )(x, g, *ws)


def _fox_norms(q_ref, k_ref, fl_ref, qg_ref, kg_ref, fb_ref):
    e = _seg_ones()

    def hnorm(x, g):
        ms = _seg_sum(x * x, e) * (1.0 / HEAD_DIM)
        return x * lax.rsqrt(ms + NORM_EPS) * g

    qn = hnorm(q_ref[...], qg_ref[...]) * (HEAD_DIM ** -0.5)
    kn = hnorm(k_ref[...], kg_ref[...])
    z = fl_ref[...] + fb_ref[...]
    lf = jnp.minimum(z, 0.0) - jnp.log1p(jnp.exp(-jnp.abs(z)))
    return qn, kn, lf


def _foxprep_sample_kernel(q_ref, k_ref, fl_ref, qg_ref, kg_ref, fb_ref, qn_ref, kn_ref, lf_ref):
    qn_ref[...], kn_ref[...], lf_ref[...] = _fox_norms(q_ref, k_ref, fl_ref, qg_ref, kg_ref, fb_ref)


def _bias_placement():
    m = np.zeros((LANES, 2 * FOX_H * LANES), np.float32)
    one = 3 * FOX_H
    for h in range(FOX_H):
        spare = h * LANES + HEAD_DIM * (1 - h % HEADS_PER_TILE)
        kspare = FOX_H * LANES + spare
        for term in range(3):
            m[term * FOX_H + h, spare + term] = 1.0
            m[one, spare + 3 + term] = 1.0
            m[one, kspare + term] = 1.0
            m[term * FOX_H + h, kspare + 3 + term] = -1.0
    return jnp.asarray(m, BF16)


def _foxprep_prompt_kernel(q_ref, k_ref, fl_ref, qg_ref, kg_ref, fb_ref, v_ref, place_ref,
                           kt_ref, vt_ref, lft_ref, qx_ref, kx_ref, vb_ref, carry_ref, *, tm):
    qn, kn, lf = _fox_norms(q_ref, k_ref, fl_ref, qg_ref, kg_ref, fb_ref)
    v = v_ref[...]
    vb_ref[...] = v.astype(BF16)
    for pg in range(tm // PAGE_SIZE):
        rows = slice(pg * PAGE_SIZE, (pg + 1) * PAGE_SIZE)
        kt_ref[pg] = kn[rows, :].T
        vt_ref[pg] = v[rows, :].T
        lft_ref[pg] = lf[rows, :].T[:FOX_H, :]

    @pl.when(pl.program_id(1) == 0)
    def _():
        carry_ref[...] = jnp.zeros_like(carry_ref)

    tri = (_iota((tm, tm), 0) >= _iota((tm, tm), 1)).astype(BF16)
    c = _exact_dot(lf, tri, left=True) + carry_ref[...]
    carry_ref[...] = c[tm - 1:tm, :]

    lane = _iota((tm, LANES), 1)
    c_hi, c_mid, c_lo = (p.astype(F32) for p in _split3(c))
    packed = jnp.where(lane < FOX_H, c_hi,
                       jnp.where(lane < 2 * FOX_H, pltpu.roll(c_mid, FOX_H, 1),
                                 jnp.where(lane < 3 * FOX_H, pltpu.roll(c_lo, 2 * FOX_H, 1),
                                           jnp.where(lane == 3 * FOX_H, 1.0, 0.0))))
    ext = jnp.dot(packed.astype(BF16), place_ref[...], preferred_element_type=F32)
    for h in range(FOX_H):
        hp, hh = divmod(h, HEADS_PER_TILE)
        own = lane // HEAD_DIM == hh
        pair = slice(hp * LANES, (hp + 1) * LANES)
        tile = slice(h * LANES, (h + 1) * LANES)
        ktile = slice((FOX_H + h) * LANES, (FOX_H + h + 1) * LANES)
        qx_ref[:, tile] = jnp.where(own, qn[:, pair], ext[:, tile]).astype(BF16)
        kx_ref[:, tile] = jnp.where(own, kn[:, pair], ext[:, ktile]).astype(BF16)


def _foxprep_sample(fq, fk, fl, qg, kg, fb, tm):
    n = fq.shape[0]
    row = lambda wd: pl.BlockSpec((tm, wd), lambda i: (i, 0))
    return pl.pallas_call(
        _foxprep_sample_kernel,
        grid=(n // tm,),
        in_specs=[row(FOX_W), row(FOX_W), row(LANES), _resident((1, FOX_W)), _resident((1, FOX_W)),
                  _resident((1, LANES))],
        out_specs=[row(FOX_W), row(FOX_W), row(LANES)],
        out_shape=[jax.ShapeDtypeStruct((n, FOX_W), F32), jax.ShapeDtypeStruct((n, FOX_W), F32),
                   jax.ShapeDtypeStruct((n, LANES), F32)],
        compiler_params=_cparams(("parallel",)),
        name="foxprep_sample",
    )(fq, fk, fl, qg, kg, fb)


def _foxprep_prompt(fq, fk, fl, fv, qg, kg, fb, nseq, tm):
    n = fq.shape[0]
    nt = n // nseq // tm
    ppt = tm // PAGE_SIZE
    n_pp = n // PAGE_SIZE
    place = _bias_placement()
    row = lambda wd: pl.BlockSpec((tm, wd), lambda b, j: (b * nt + j, 0))
    pages = lambda r: pl.BlockSpec((ppt, r, PAGE_SIZE), lambda b, j: (b * nt + j, 0, 0))
    return pl.pallas_call(
        functools.partial(_foxprep_prompt_kernel, tm=tm),
        grid=(nseq, nt),
        in_specs=[row(FOX_W), row(FOX_W), row(LANES), _resident((1, FOX_W)), _resident((1, FOX_W)),
                  _resident((1, LANES)), row(FOX_W), _resident(place.shape)],
        out_specs=[pages(FOX_W), pages(FOX_W), pages(FOX_H), row(FOX_H * LANES), row(FOX_H * LANES), row(FOX_W)],
        out_shape=[jax.ShapeDtypeStruct((n_pp, FOX_W, PAGE_SIZE), F32), jax.ShapeDtypeStruct((n_pp, FOX_W, PAGE_SIZE), F32),
                   jax.ShapeDtypeStruct((n_pp, FOX_H, PAGE_SIZE), F32),
                   jax.ShapeDtypeStruct((n, FOX_H * LANES), BF16), jax.ShapeDtypeStruct((n, FOX_H * LANES), BF16),
                   jax.ShapeDtypeStruct((n, FOX_W), BF16)],
        scratch_shapes=[pltpu.VMEM((1, LANES), F32)],
        compiler_params=_cparams(("parallel", "arbitrary")),
        name="foxprep_prompt",
    )(fq, fk, fl, qg, kg, fb, fv, place)


def _foxattn_kernel(qx_ref, kx_ref, vb_ref, o_ref, *, tq):
    qi = pl.program_id(2)
    q = [qx_ref[:, hh * LANES:(hh + 1) * LANES] for hh in range(HEADS_PER_TILE)]

    def block(off, carry, diagonal):
        vb = vb_ref[pl.ds(off, tq), :]
        stats, ps = [], []
        for hh in range(HEADS_PER_TILE):
            m, l, _ = carry[hh]
            kb = kx_ref[pl.ds(off, tq), hh * LANES:(hh + 1) * LANES]
            s = lax.dot_general(q[hh], kb, (((1,), (1,)), ((), ())), preferred_element_type=F32)
            if diagonal:
                s = jnp.where(_iota((tq, tq), 1) <= _iota((tq, tq), 0), s, NEG_BIG)
            m_new = jnp.maximum(m, jnp.max(s, axis=-1, keepdims=True))
            alpha = jnp.exp(m - m_new)
            p = jnp.exp(s - m_new)
            stats.append((m_new, alpha, alpha * l + jnp.sum(p, axis=-1, keepdims=True)))
            ps.append(p.astype(BF16))
        pv = jnp.dot(jnp.concatenate(ps, axis=0), vb, preferred_element_type=F32)
        return tuple((m_new, l, alpha * carry[hh][2] + pv[hh * tq:(hh + 1) * tq])
                     for hh, (m_new, alpha, l) in enumerate(stats))

    init = tuple((jnp.full((tq, 1), NEG_BIG, F32), jnp.zeros((tq, 1), F32), jnp.zeros((tq, LANES), F32))
                 for _ in range(HEADS_PER_TILE))
    carry = lax.fori_loop(0, qi, lambda j, c: block(pl.multiple_of(j * tq, tq), c, False), init)
    carry = block(pl.multiple_of(qi * tq, tq), carry, True)
    outs = [acc / l for _, l, acc in carry]
    o_ref[...] = jnp.where(_iota((tq, LANES), 1) // HEAD_DIM == 0, outs[0], outs[1])


def _foxattn(qx, kx, vb, nseq, tq):
    n = qx.shape[0]
    t = n // nseq
    nq = t // tq
    npair = FOX_H // HEADS_PER_TILE
    pair_w = HEADS_PER_TILE * LANES
    return pl.pallas_call(
        functools.partial(_foxattn_kernel, tq=tq),
        grid=(nseq, npair, nq),
        in_specs=[
            pl.BlockSpec((tq, pair_w), lambda b, h, i: (b * nq + i, h)),
            pl.BlockSpec((t, pair_w), lambda b, h, i: (b, h)),
            pl.BlockSpec((t, LANES), lambda b, h, i: (b, h)),
        ],
        out_specs=pl.BlockSpec((tq, LANES), lambda b, h, i: (b * nq + i, h)),
        out_shape=jax.ShapeDtypeStruct((n, FOX_W), F32),
        compiler_params=_cparams(("parallel", "parallel", "arbitrary")),
        name="foxattn",
    )(qx, kx, vb)


def _foxsample_kernel(pt_ref, q_ref, kn_ref, vn_ref, lfn_ref, *rest, n_pages, t_new):
    del pt_ref
    kp = rest[:n_pages]
    vp = rest[n_pages:2 * n_pages]
    lp = rest[2 * n_pages:3 * n_pages]
    o_ref = rest[3 * n_pages]
    nrow = t_new * FOX_H

    rep = lambda x: jnp.concatenate([x] * t_new, axis=0)
    q = q_ref[...]
    hmask = _iota((FOX_H, FOX_W), 1) // HEAD_DIM == _iota((FOX_H, FOX_W), 0)
    qbd = jnp.concatenate(
        [jnp.where(hmask, jnp.broadcast_to(q[t:t + 1, :], (FOX_H, FOX_W)), 0.0) for t in range(t_new)],
        axis=0).astype(BF16)

    lfn = lfn_ref[...]
    diag8 = _iota((FOX_H, LANES), 1) == _iota((FOX_H, LANES), 0)
    cn_cols = []
    run = jnp.zeros((1, LANES), F32)
    for t in range(t_new):
        run = run + lfn[t:t + 1, :]
        cn_cols.append(jnp.sum(jnp.where(diag8, jnp.broadcast_to(run, (FOX_H, LANES)), 0.0), axis=-1, keepdims=True))
    cn = jnp.concatenate(cn_cols, axis=0)

    zeros_tail = jnp.zeros((PAGE_SIZE - SUBLANES, FOX_W), F32)
    key = _iota((nrow, PAGE_SIZE), 1)
    trow = _iota((nrow, PAGE_SIZE), 0) // FOX_H
    ckey = jnp.zeros((nrow, PAGE_SIZE), F32)
    for j in range(t_new):
        ckey = jnp.where(key == j, rep(cn_cols[j]), ckey)
    s_new = _bdot_nt(qbd, jnp.concatenate([kn_ref[...], zeros_tail], axis=0)) + (cn - ckey)
    s_new = jnp.where(key <= trow, s_new, NEG_BIG)

    lf_all = jnp.concatenate([lp[i][...] for i in range(n_pages)], axis=0)
    upper = (_iota((PAGE_SIZE, PAGE_SIZE), 0) <= _iota((PAGE_SIZE, PAGE_SIZE), 1)).astype(BF16)
    cp_all = _exact_dot(lf_all, upper, left=False)
    before = [jnp.zeros((FOX_H, 1), F32)]
    for i in range(n_pages):
        before.append(before[-1] + cp_all[i * FOX_H:(i + 1) * FOX_H, PAGE_SIZE - 1:PAGE_SIZE])
    cq_abs = rep(before[n_pages]) + cn
    s_past = []
    for i in range(n_pages):
        cp = rep(cp_all[i * FOX_H:(i + 1) * FOX_H, :] + before[i])
        s_past.append(_bdot(qbd, kp[i][...].reshape(FOX_W, PAGE_SIZE)) + (cq_abs - cp))

    m = jnp.max(s_new, axis=-1, keepdims=True)
    for s in s_past:
        m = jnp.maximum(m, jnp.max(s, axis=-1, keepdims=True))
    p = jnp.exp(s_new - m)
    l = jnp.sum(p, axis=-1, keepdims=True)
    o = _bdot(p, jnp.concatenate([vn_ref[...], zeros_tail], axis=0))
    for i in range(n_pages):
        p = jnp.exp(s_past[i] - m)
        l = l + jnp.sum(p, axis=-1, keepdims=True)
        o = o + _bdot_nt(p, vp[i][...].reshape(FOX_W, PAGE_SIZE))
    o = o / l
    omask = _iota((nrow, FOX_W), 1) // HEAD_DIM == _iota((nrow, FOX_W), 0) % FOX_H
    o_ref[...] = jnp.sum(jnp.where(omask, o, 0.0).reshape(t_new, FOX_H, FOX_W), axis=1)


def _foxsample(page_table, layer, qn, kn, vn, lfn, cache_kt, cache_vt, cache_lft, t_new):
    db, n_pages = page_table.shape
    pt = page_table.reshape(-1)
    new = lambda wd: pl.BlockSpec((None, SUBLANES, wd), lambda b, pt: (b, 0, 0))

    def page(shape, i):
        nd = len(shape)
        return pl.BlockSpec((None, None) + shape, lambda b, pt: (layer, pt[b * n_pages + i]) + (0,) * nd)

    in_specs = [new(FOX_W), new(FOX_W), new(FOX_W), new(LANES)]
    in_specs += [page((FOX_H, HEAD_DIM, PAGE_SIZE), i) for i in range(n_pages)]
    in_specs += [page((FOX_H, HEAD_DIM, PAGE_SIZE), i) for i in range(n_pages)]
    in_specs += [page((FOX_H, PAGE_SIZE), i) for i in range(n_pages)]
    return pl.pallas_call(
        functools.partial(_foxsample_kernel, n_pages=n_pages, t_new=t_new),
        grid_spec=pltpu.PrefetchScalarGridSpec(
            num_scalar_prefetch=1,
            grid=(db,),
            in_specs=in_specs,
            out_specs=pl.BlockSpec((None, t_new, FOX_W), lambda b, pt: (b, 0, 0)),
        ),
        out_shape=jax.ShapeDtypeStruct((db, t_new, FOX_W), F32),
        compiler_params=_cparams(("arbitrary",)),
        name="foxsample",
    )(pt, qn, kn, vn, lfn, *([cache_kt] * n_pages), *([cache_vt] * n_pages), *([cache_lft] * n_pages))


def _rwkvprep_kernel(p_ref, init_ref, mu_ref, w0_ref, w2_ref, a0_ref, a2_ref, g2_ref, kk_ref, ka_ref,
                     r_out, w_out, k_out, v_out, a_out, b_out, g_out, *carry, tm, period):
    e = _seg_ones()
    p = p_ref[...]
    rolled = pltpu.roll(p, 1, 0)
    row = _iota((tm, RWKV_PROJ), 0)
    if period >= tm:
        carry_ref, = carry

        @pl.when(pl.program_id(1) == 0)
        def _():
            carry_ref[...] = init_ref[...]

        pp = jnp.where(row == 0, carry_ref[...], rolled)
        carry_ref[...] = p[tm - 1:tm, :]
    else:
        pp = jnp.where(row % period == 0, init_ref[...], rolled)
    xs = p + (pp - p) * mu_ref[...]
    r = xs[:, 0:RWKV_W]
    k = xs[:, RWKV_W:2 * RWKV_W]
    v = xs[:, 2 * RWKV_W:3 * RWKV_W]
    lo = xs[:, 3 * RWKV_W:]
    w = -_softplus(-(w0_ref[...] + _bdot(jnp.tanh(lo), w2_ref[...]))) - 0.5
    a = _sigmoid(a0_ref[...] + _bdot(lo, a2_ref[...]))
    kk = k * kk_ref[...]
    nrm = jnp.sqrt(_seg_sum(kk * kk, e))
    kk = kk / jnp.maximum(nrm, 1e-12)
    r_out[...] = r
    w_out[...] = jnp.exp(-jnp.exp(w))
    k_out[...] = k * (1.0 + (a - 1.0) * ka_ref[...])
    v_out[...] = v
    a_out[...] = -kk
    b_out[...] = kk * a
    g_out[...] = _bdot(_sigmoid(lo), g2_ref[...])


def _rwkvprep(p, shift, period, mu, w0, w2p, a0, a2p, g2p, k_k, k_a, tm):
    n = p.shape[0]
    nseq = n // period
    vec = _resident((1, RWKV_W))
    lora = _resident((LANES, RWKV_W))
    if period >= tm:
        nt = period // tm
        grid = (nseq, nt)
        row = lambda wd: pl.BlockSpec((tm, wd), lambda b, j: (b * nt + j, 0))
        init = shift[:, None, :]
        init_spec = pl.BlockSpec((None, 1, RWKV_PROJ), lambda b, j: (b, 0, 0))
        scratch = [pltpu.VMEM((1, RWKV_PROJ), F32)]
        sem = ("parallel", "arbitrary")
    else:
        grid = (n // tm,)
        row = lambda wd: pl.BlockSpec((tm, wd), lambda i: (i, 0))
        init = jnp.repeat(shift, period, axis=0)
        init_spec = row(RWKV_PROJ)
        scratch = []
        sem = ("parallel",)
    return pl.pallas_call(
        functools.partial(_rwkvprep_kernel, tm=tm, period=period),
        grid=grid,
        in_specs=[row(RWKV_PROJ), init_spec, _resident((1, RWKV_PROJ)), vec, lora, vec, lora, lora, vec, vec],
        out_specs=[row(RWKV_W)] * 7,
        out_shape=[jax.ShapeDtypeStruct((n, RWKV_W), F32)] * 7,
        scratch_shapes=scratch,
        compiler_params=_cparams(sem),
        name="rwkvprep",
    )(p, init, mu, w0, w2p, a0, a2p, g2p, k_k, k_a)


def _rwkvscan_kernel(r_ref, w_ref, k_ref, v_ref, a_ref, b_ref, s0_ref, y_ref, st_ref, s_scr, *, bblk, tblk):
    tb = pl.program_id(1)
    ng = HEADS_PER_TILE * bblk

    @pl.when(tb == 0)
    def _():
        s_scr[...] = s0_ref[...]

    e = _seg_ones()
    diag = (_iota((HEAD_DIM, LANES), 1) % HEAD_DIM == _iota((HEAD_DIM, LANES), 0)).astype(BF16)[None]
    half = (_iota((SUBLANES, LANES), 1) // HEAD_DIM == _iota((SUBLANES, LANES), 0)).astype(BF16)
    first = _iota((1, LANES), 1) < HEAD_DIM

    sub = min(SUBLANES, tblk)

    def chunk(c, carry):
        off = pl.multiple_of(c * sub, sub)
        tiles = [[ref[b, pl.ds(off, sub), :] for b in range(bblk)] for ref in (r_ref, w_ref, k_ref, v_ref, a_ref, b_ref)]

        def rows(kind, i):
            return jnp.concatenate(
                [tiles[kind][b][i:i + 1, hp * LANES:(hp + 1) * LANES][None] for b in range(bblk) for hp in range(2)],
                axis=0)

        vdiag = jnp.concatenate([rows(3, i).astype(BF16) * diag for i in range(sub)], axis=0)
        vcol = jnp.dot(vdiag.reshape(sub * ng * HEAD_DIM, LANES), e,
                       preferred_element_type=F32).reshape(sub * ng, HEAD_DIM, LANES)

        s = s_scr[...]
        srs = []
        for i in range(sub):
            sa = jnp.dot((s * rows(4, i)).reshape(ng * HEAD_DIM, LANES).astype(BF16), e,
                         preferred_element_type=F32).reshape(ng, HEAD_DIM, LANES)
            s = s * rows(1, i) + sa * rows(5, i) + vcol[i * ng:(i + 1) * ng] * rows(2, i)
            srs.append((s * rows(0, i)).astype(BF16))
        s_scr[...] = s

        sr = jnp.concatenate(srs, axis=0).reshape(sub * ng * HEAD_DIM, LANES)
        out = lax.dot_general(half, sr, (((1,), (1,)), ((), ())), preferred_element_type=F32)
        for b in range(bblk):
            ytile = []
            for i in range(sub):
                lanes = slice((i * bblk + b) * 2 * HEAD_DIM, (i * bblk + b + 1) * 2 * HEAD_DIM)
                h0, h1 = out[0:1, lanes], out[1:2, lanes]
                ytile.append(jnp.concatenate(
                    [jnp.where(first, h0, pltpu.roll(h1, HEAD_DIM, 1)),
                     jnp.where(first, pltpu.roll(h0, HEAD_DIM, 1), h1)], axis=1))
            y_ref[b, pl.ds(off, sub), :] = jnp.concatenate(ytile, axis=0)
        return carry

    lax.fori_loop(0, tblk // sub, chunk, 0)

    @pl.when(tb == pl.num_programs(1) - 1)
    def _():
        st_ref[...] = s_scr[...]


def _rwkvscan(r, w, k, v, a, b, s0, bblk, tblk):
    nb, t, _ = r.shape
    ng = HEADS_PER_TILE * bblk
    seq = pl.BlockSpec((bblk, tblk, RWKV_W), lambda i, j: (i, j, 0))
    st = pl.BlockSpec((ng, HEAD_DIM, LANES), lambda i, j: (i, 0, 0))
    return pl.pallas_call(
        functools.partial(_rwkvscan_kernel, bblk=bblk, tblk=tblk),
        grid=(nb // bblk, t // tblk),
        in_specs=[seq] * 6 + [st],
        out_specs=[seq, st],
        out_shape=[jax.ShapeDtypeStruct((nb, t, RWKV_W), F32), jax.ShapeDtypeStruct(s0.shape, F32)],
        scratch_shapes=[pltpu.VMEM((ng, HEAD_DIM, LANES), F32)],
        compiler_params=_cparams(("parallel", "arbitrary")),
        name="rwkvscan",
    )(r, w, k, v, a, b, s0)


def _rwkv_short_kernel(z_ref, s0_ref, y_ref, st_ref):
    nt = z_ref.shape[0]
    R, W, K, V, A, B = range(6)

    def value_rows(i, carry):
        base = pl.multiple_of(i * SUBLANES, SUBLANES)
        vt = [z_ref[t, V, pl.ds(base, SUBLANES), :] for t in range(nt)]
        yrows = [[] for _ in range(nt)]
        for j in range(SUBLANES):
            s = s0_ref[base + j]
            for t in range(nt):
                sa = jnp.sum(s * z_ref[t, A], axis=0, keepdims=True)
                s = s * z_ref[t, W] + sa * z_ref[t, B] + vt[t][j:j + 1, :] * z_ref[t, K]
                yrows[t].append(jnp.sum(s * z_ref[t, R], axis=0, keepdims=True))
            st_ref[base + j] = s
        for t in range(nt):
            y_ref[t, pl.ds(base, SUBLANES), :] = jnp.concatenate(yrows[t], axis=0)
        return carry

    lax.fori_loop(0, HEAD_DIM // SUBLANES, value_rows, 0)


def _rwkv_short(zt, s0):
    nt, _, nh, _, nb = zt.shape
    assert nb == LANES and nt <= SUBLANES
    st = pl.BlockSpec((None, HEAD_DIM, HEAD_DIM, LANES), lambda h: (h, 0, 0, 0))
    return pl.pallas_call(
        _rwkv_short_kernel,
        grid=(nh,),
        in_specs=[pl.BlockSpec((nt, 6, None, HEAD_DIM, LANES), lambda h: (0, 0, h, 0, 0)), st],
        out_specs=[pl.BlockSpec((nt, None, HEAD_DIM, LANES), lambda h: (0, h, 0, 0)), st],
        out_shape=[jax.ShapeDtypeStruct((nt, nh, HEAD_DIM, LANES), F32), jax.ShapeDtypeStruct(s0.shape, F32)],
        compiler_params=_cparams(("parallel",)),
        name="rwkv_short",
    )(zt, s0)


def _rope_kernel(inv_ref, cos_ref, sin_ref, *, pos0):
    t = cos_ref.shape[0]
    pos = (pos0 + _iota((t, LANES), 0)).astype(F32)
    ang = pos * inv_ref[...]
    first_half = _iota((t, LANES), 1) % HEAD_DIM < HEAD_DIM // 2
    cos_ref[...] = jnp.cos(ang)
    sin_ref[...] = jnp.where(first_half, -jnp.sin(ang), jnp.sin(ang))


def _rope_tables(inv_lanes, t, pos0):
    return pl.pallas_call(
        functools.partial(_rope_kernel, pos0=pos0),
        out_shape=[jax.ShapeDtypeStruct((t, LANES), F32)] * 2,
        name="rope",
    )(inv_lanes)


def _ret_kernel(x_ref, cos_ref, sin_ref, lg_ref, lgh_ref, gn_ref, r0_ref, o_ref, rt_ref, r_scr,
                *, bs):
    lp = RET_CHUNK
    npair = RET_H // HEADS_PER_TILE
    c = pl.program_id(1)

    @pl.when(c == 0)
    def _():
        r_scr[...] = r0_ref[...]

    lane = _iota((lp, LANES), 1)
    first_half = lane % HEAD_DIM < HEAD_DIM // 2
    cos = cos_ref[...]
    sin = sin_ref[...]

    def rope(x):
        swapped = jnp.where(first_half, pltpu.roll(x, LANES - HEAD_DIM // 2, 1), pltpu.roll(x, HEAD_DIM // 2, 1))
        return x * cos + swapped * sin

    idx = _iota((lp, 1), 0).astype(F32)
    diff = (_iota((lp, lp), 0) - _iota((lp, lp), 1)).astype(F32)
    same_head = _iota((LANES, LANES), 0) // HEAD_DIM == _iota((LANES, LANES), 1) // HEAD_DIM
    e = _seg_ones()
    dmask, cross, kdec, cdec = [], [], [], []
    for hp in range(npair):
        lg = lg_ref[hp]
        dmask.append(jnp.concatenate(
            [jnp.where(diff >= 0.0, jnp.exp(lgh_ref[hp, hh:hh + 1, :] * jnp.maximum(diff, 0.0)), 0.0)
             for hh in range(HEADS_PER_TILE)], axis=0))
        cross.append(jnp.exp(lg * (idx + 1.0)))
        kdec.append(jnp.exp(lg * (lp - 1.0 - idx)))
        cdec.append(jnp.exp(lg * float(lp)))

    chains = [(s, hp) for s in range(bs) for hp in range(npair)]
    os_ = []
    for s, hp in chains:
        col = lambda j: x_ref[s, :, (j * npair + hp) * LANES:(j * npair + hp + 1) * LANES]
        q = rope(col(0))
        k = rope(col(1)) * (HEAD_DIM ** -0.5)
        vb = col(2).astype(BF16)
        qh = jnp.concatenate([jnp.where(lane // HEAD_DIM == hh, q, 0.0) for hh in range(HEADS_PER_TILE)], axis=0)
        sc = _bdot_nt(qh, k) * dmask[hp]
        ov = jnp.dot(sc.astype(BF16), vb, preferred_element_type=F32)
        o = jnp.where(lane // HEAD_DIM == 0, ov[:lp], ov[lp:])
        r = r_scr[s * npair + hp]
        os_.append(o + _bdot(q, r) * cross[hp])
        upd = jnp.dot((k * kdec[hp]).T.astype(BF16), vb, preferred_element_type=F32)
        r_scr[s * npair + hp] = r * cdec[hp] + jnp.where(same_head, upd, 0.0)

    o_all = jnp.concatenate(os_, axis=0)
    xc = o_all - _seg_sum(o_all, e) * (1.0 / HEAD_DIM)
    yn = xc * lax.rsqrt(_seg_sum(xc * xc, e) * (1.0 / HEAD_DIM) + RET_GN_EPS)
    for i, (s, hp) in enumerate(chains):
        g = x_ref[s, :, (3 * npair + hp) * LANES:(3 * npair + hp + 1) * LANES]
        o_ref[s, :, hp * LANES:(hp + 1) * LANES] = (
            yn[i * lp:(i + 1) * lp] * gn_ref[:, hp * LANES:(hp + 1) * LANES] * (g * _sigmoid(g)))

    @pl.when(c == pl.num_programs(1) - 1)
    def _():
        rt_ref[...] = r_scr[...]


def _retention(proj, cos, sin, lg, lgh, gn, r0, bs):
    nb, t, _ = proj.shape
    l_in = RET_CHUNK
    nc = t // l_in
    npair = RET_H // HEADS_PER_TILE
    st = pl.BlockSpec((bs * npair, LANES, LANES), lambda b, c: (b, 0, 0))
    tab = pl.BlockSpec((l_in, LANES), lambda b, c: (c, 0))
    return pl.pallas_call(
        functools.partial(_ret_kernel, bs=bs),
        grid=(nb // bs, nc),
        in_specs=[pl.BlockSpec((bs, l_in, 4 * RET_W), lambda b, c: (b, c, 0)), tab, tab,
                  _resident(lg.shape), _resident(lgh.shape), _resident(gn.shape), st],
        out_specs=[pl.BlockSpec((bs, l_in, RET_W), lambda b, c: (b, c, 0)), st],
        out_shape=[jax.ShapeDtypeStruct((nb, t, RET_W), F32), jax.ShapeDtypeStruct(r0.shape, F32)],
        scratch_shapes=[pltpu.VMEM((bs * npair, LANES, LANES), F32)],
        compiler_params=_cparams(("parallel", "arbitrary")),
        name="retention",
    )(proj, cos, sin, lg, lgh, gn, r0)


def _rope_cols_kernel(inv_ref, cos_ref, sin_ref, *, pos0):
    for t in range(cos_ref.shape[0]):
        ang = float(pos0 + t) * inv_ref[...]
        cos_ref[t] = jnp.cos(ang)
        sin_ref[t] = jnp.sin(ang)


def _rope_cols(inv_cols, t, pos0):
    return pl.pallas_call(
        functools.partial(_rope_cols_kernel, pos0=pos0),
        out_shape=[jax.ShapeDtypeStruct((t, HEAD_DIM // 2, LANES), F32)] * 2,
        name="rope_cols",
    )(inv_cols)


def _ret_short_kernel(x_ref, cos_ref, sin_ref, lg_ref, gn_ref, r0_ref, o_ref, rt_ref, q_scr, k_scr):
    nt = x_ref.shape[0]
    half = HEAD_DIM // 2
    lg = lg_ref[0:1, :]
    gam = lambda n: jnp.exp(lg * float(n))

    def rope(x, t):
        x1, x2 = x[:half], x[half:]
        c, s = cos_ref[t], sin_ref[t]
        return jnp.concatenate([x1 * c - x2 * s, x2 * c + x1 * s], axis=0)

    q = [rope(x_ref[t, 0], t) for t in range(nt)]
    k = [rope(x_ref[t, 1], t) * (HEAD_DIM ** -0.5) for t in range(nt)]
    v = [x_ref[t, 2] for t in range(nt)]

    intra = []
    for t in range(nt):
        acc = None
        for t2 in range(t + 1):
            term = (jnp.sum(q[t] * k[t2], axis=0, keepdims=True) * gam(t - t2)) * v[t2]
            acc = term if acc is None else acc + term
        intra.append(acc)

    for t in range(nt):
        q_scr[t] = q[t]
        k_scr[t] = k[t] * gam(nt - 1 - t)
    decay_all = gam(nt)

    def key_rows(i, acc):
        base = pl.multiple_of(i * SUBLANES, SUBLANES)
        qb = [q_scr[t, pl.ds(base, SUBLANES), :] for t in range(nt)]
        kb = [k_scr[t, pl.ds(base, SUBLANES), :] for t in range(nt)]
        acc = list(acc)
        for j in range(SUBLANES):
            r = r0_ref[base + j]
            upd = r * decay_all
            for t in range(nt):
                acc[t] = acc[t] + qb[t][j:j + 1, :] * r
                upd = upd + kb[t][j:j + 1, :] * v[t]
            rt_ref[base + j] = upd
        return tuple(acc)

    cross = lax.fori_loop(0, HEAD_DIM // SUBLANES, key_rows,
                          tuple(jnp.zeros((HEAD_DIM, LANES), F32) for _ in range(nt)))

    for t in range(nt):
        o = intra[t] + cross[t] * gam(t + 1)
        mean = jnp.mean(o, axis=0, keepdims=True)
        xc = o - mean
        var = jnp.mean(xc * xc, axis=0, keepdims=True)
        g = x_ref[t, 3]
        o_ref[t] = xc * lax.rsqrt(var + RET_GN_EPS) * gn_ref[...] * (g * _sigmoid(g))


def _retention_short(xt, cos, sin, lg_rows, gn_cols, r0):
    nt, _, nh, _, nb = xt.shape
    assert nb == LANES and nt <= SUBLANES
    tab = _resident(cos.shape)
    return pl.pallas_call(
        _ret_short_kernel,
        grid=(nh,),
        in_specs=[pl.BlockSpec((nt, 4, None, HEAD_DIM, LANES), lambda h: (0, 0, h, 0, 0)), tab, tab,
                  pl.BlockSpec((None, SUBLANES, LANES), lambda h: (h, 0, 0)),
                  pl.BlockSpec((None, HEAD_DIM, LANES), lambda h: (h, 0, 0)),
                  pl.BlockSpec((None, HEAD_DIM, HEAD_DIM, LANES), lambda h: (h, 0, 0, 0))],
        out_specs=[pl.BlockSpec((nt, None, HEAD_DIM, LANES), lambda h: (0, h, 0, 0)),
                   pl.BlockSpec((None, HEAD_DIM, HEAD_DIM, LANES), lambda h: (h, 0, 0, 0))],
        out_shape=[jax.ShapeDtypeStruct((nt, nh, HEAD_DIM, LANES), F32), jax.ShapeDtypeStruct(r0.shape, F32)],
        scratch_shapes=[pltpu.VMEM((nt, HEAD_DIM, LANES), F32), pltpu.VMEM((nt, HEAD_DIM, LANES), F32)],
        compiler_params=_cparams(("parallel",)),
        name="retention_short",
    )(xt, cos, sin, lg_rows, gn_cols, r0)


def _outproj_kernel(x_ref, fo_ref, y_ref, r_ref, k_ref, v_ref, g_ref, eo_ref, lw_ref, lb_ref, rk_ref,
                    wf_ref, wr_ref, we_ref, o_ref):
    e = _seg_ones()
    y = y_ref[...]
    mean = _seg_sum(y, e) * (1.0 / HEAD_DIM)
    yc = y - mean
    var = _seg_sum(yc * yc, e) * (1.0 / HEAD_DIM)
    yn = yc * lax.rsqrt(var + RWKV_GN_EPS) * lw_ref[...] + lb_ref[...]
    v = v_ref[...]
    bonus = _seg_sum(r_ref[...] * k_ref[...] * rk_ref[...], e) * v
    ro = (yn + bonus) * g_ref[...]
    o_ref[...] = (x_ref[...] + _bdot(fo_ref[...], wf_ref[...]) + _bdot(ro, wr_ref[...])
                  + _bdot(eo_ref[...], we_ref[...]))


def _outproj(x, fo, y, r, k, v, g, eo, lw, lb, rk, wf, wr, we, tm):
    n = x.shape[0]
    row = lambda wd: pl.BlockSpec((tm, wd), lambda i: (i, 0))
    vec = _resident((1, RWKV_W))
    return pl.pallas_call(
        _outproj_kernel,
        grid=(n // tm,),
        in_specs=[row(D_MODEL), row(FOX_W)] + [row(RWKV_W)] * 6 + [vec, vec, vec,
                  _resident(wf.shape), _resident(wr.shape), _resident(we.shape)],
        out_specs=row(D_MODEL),
        out_shape=jax.ShapeDtypeStruct((n, D_MODEL), F32),
        compiler_params=_cparams(("parallel",)),
        name="outproj",
    )(x, fo, y, r, k, v, g, eo, lw, lb, rk, wf, wr, we)


def _ffn_kernel(x_ref, g_ref, wg_ref, wu_ref, wd_ref, o_ref):
    x = x_ref[...]
    h = (x * lax.rsqrt(jnp.mean(x * x, -1, keepdims=True) + NORM_EPS) * g_ref[...]).astype(BF16)
    gate = jnp.dot(h, wg_ref[...], preferred_element_type=F32)
    up = jnp.dot(h, wu_ref[...], preferred_element_type=F32)
    act = gate * _sigmoid(gate) * up
    o_ref[...] = x + _bdot(act, wd_ref[...])


def _ffn(x, g, wg, wu, wd, tm):
    n = x.shape[0]
    row = pl.BlockSpec((tm, D_MODEL), lambda i: (i, 0))
    return pl.pallas_call(
        _ffn_kernel,
        grid=(n // tm,),
        in_specs=[row, _resident((1, D_MODEL)), _resident(wg.shape), _resident(wu.shape), _resident(wd.shape)],
        out_specs=row,
        out_shape=jax.ShapeDtypeStruct((n, D_MODEL), F32),
        compiler_params=_cparams(("parallel",)),
        name="ffn",
    )(x, g, wg, wu, wd)


def _rwkv_state_out(s, nb):
    s = s.reshape(nb, RWKV_H // 2, HEAD_DIM, 2, HEAD_DIM).transpose(0, 1, 3, 2, 4)
    return s.reshape(nb, RWKV_H, HEAD_DIM, HEAD_DIM)


def _ret_state_out(r, nb):
    r = r.reshape(nb, RET_H // 2, LANES, LANES)
    return jnp.stack([r[:, :, :HEAD_DIM, :HEAD_DIM], r[:, :, HEAD_DIM:, HEAD_DIM:]], axis=2).reshape(
        nb, RET_H, HEAD_DIM, HEAD_DIM)


def _layer_weights(l, ln_mix_g, w_in, fox_qn_g, fox_kn_g, fox_f_b, rwkv_mu, rwkv_w0, rwkv_w2, rwkv_a0, rwkv_a2,
                   rwkv_g2, rwkv_k_k, rwkv_k_a, rwkv_r_k, rwkv_lnx_w, rwkv_lnx_b, ret_gn_w, w_out, ln_ffn_g,
                   w_gate, w_up, w_down):
    wi = w_in[l].astype(BF16)
    o_f = 3 * FOX_W
    o_r = o_f + FOX_H
    o_e = o_r + RWKV_PROJ
    pad_rows = lambda w, off: jnp.zeros((LANES, RWKV_W), BF16).at[off:off + w.shape[0]].set(w.astype(BF16))
    wo = w_out[l].astype(BF16)
    return dict(
        ln_mix_g=ln_mix_g[l][None],
        w_in=[wi[:, 0:FOX_W], wi[:, FOX_W:2 * FOX_W], wi[:, 2 * FOX_W:o_f],
              jnp.pad(wi[:, o_f:o_r], ((0, 0), (0, LANES - FOX_H))), wi[:, o_r:o_e], wi[:, o_e:]],
        qg=jnp.tile(fox_qn_g[l], FOX_H)[None], kg=jnp.tile(fox_kn_g[l], FOX_H)[None],
        fb=jnp.pad(fox_f_b[l], (0, LANES - FOX_H))[None],
        mu=rwkv_mu[l][None], w0=rwkv_w0[l][None], a0=rwkv_a0[l][None],
        w2=pad_rows(rwkv_w2[l], 0), a2=pad_rows(rwkv_a2[l], RWKV_LORA_W),
        g2=pad_rows(rwkv_g2[l], RWKV_LORA_W + RWKV_LORA_A),
        k_k=rwkv_k_k[l][None], k_a=rwkv_k_a[l][None], r_k=rwkv_r_k[l].reshape(1, RWKV_W),
        lnx_w=rwkv_lnx_w[l][None], lnx_b=rwkv_lnx_b[l][None], gn=ret_gn_w[l][None],
        wo_f=wo[:FOX_W], wo_r=wo[FOX_W:FOX_W + RWKV_W], wo_e=wo[FOX_W + RWKV_W:],
        ln_ffn_g=ln_ffn_g[l][None],
        w_gate=w_gate[l].astype(BF16), w_up=w_up[l].astype(BF16), w_down=w_down[l].astype(BF16),
    )


def _mix_and_ffn(x2, fo, p_rwkv, shift, s0, eo, lw, nb, t, tm, bblk, tblk):
    r, w, k, v, a, b, g = _rwkvprep(p_rwkv, shift, t, lw['mu'], lw['w0'], lw['w2'], lw['a0'], lw['a2'], lw['g2'],
                                    lw['k_k'], lw['k_a'], tm)
    if s0.ndim == 4:
        zt = jnp.stack([r, w, k, v, a, b], axis=1).reshape(nb, t, 6, RWKV_H, HEAD_DIM)
        y, s_t = _rwkv_short(jnp.transpose(zt, (1, 2, 3, 4, 0)), s0)
        y = jnp.transpose(y, (3, 0, 1, 2))
    else:
        seq = lambda z: z.reshape(nb, t, RWKV_W)
        y, s_t = _rwkvscan(seq(r), seq(w), seq(k), seq(v), seq(a), seq(b), s0, bblk, tblk)
    x2 = _outproj(x2, fo, y.reshape(nb * t, RWKV_W), r, k, v, g, eo, lw['lnx_w'], lw['lnx_b'], lw['r_k'],
                  lw['wo_f'], lw['wo_r'], lw['wo_e'], tm)
    x2 = _ffn(x2, lw['ln_ffn_g'], lw['w_gate'], lw['w_up'], lw['w_down'], tm)
    return x2, s_t


def kernel(x_prompt, x_sample, cache_fox_k, cache_fox_v, cache_fox_logf, state_rwkv, state_rwkv_shift, state_ret,
           page_table, ln_mix_g, w_in, fox_qn_g, fox_kn_g, fox_f_b, rwkv_mu, rwkv_w0, rwkv_w2, rwkv_a0, rwkv_a2,
           rwkv_g2, rwkv_k_k, rwkv_k_a, rwkv_r_k, rwkv_lnx_w, rwkv_lnx_b, ret_gn_w, w_out, ln_ffn_g, w_gate,
           w_up, w_down):
    nb, t, _ = x_prompt.shape
    db, ts, _ = x_sample.shape
    depth = w_in.shape[0]
    n_pages = page_table.shape[1]
    past_len = n_pages * PAGE_SIZE
    n_pool = cache_fox_k.shape[1]

    half = HEAD_DIM // 2
    inv = ROPE_BASE ** (-jnp.arange(half, dtype=F32) / half)
    inv_lanes = jnp.tile(inv, LANES // half)[None]
    log_gamma = jnp.log1p(-jnp.exp2(-5.0 - jnp.arange(RET_H, dtype=F32)))
    lg = jnp.repeat(log_gamma, HEAD_DIM).reshape(RET_H // 2, 1, LANES)
    lgh = jnp.broadcast_to(log_gamma.reshape(RET_H // 2, 2, 1), (RET_H // 2, 2, LANES))
    cos_p, sin_p = _rope_tables(inv_lanes, t, 0)
    cos_s, sin_s = _rope_cols(jnp.broadcast_to(inv[:, None], (half, LANES)), ts, past_len)
    lg_rows = jnp.broadcast_to(log_gamma[:, None, None], (RET_H, SUBLANES, LANES))

    ckt = jnp.transpose(cache_fox_k, (0, 1, 3, 4, 2))
    cvt = jnp.transpose(cache_fox_v, (0, 1, 3, 4, 2))
    clt = jnp.swapaxes(cache_fox_logf, 2, 3)

    pad8 = lambda z: jnp.pad(z.reshape(db, ts, -1), ((0, 0), (0, SUBLANES - ts), (0, 0)))

    yp = x_prompt.reshape(nb * t, D_MODEL)
    ys = x_sample.reshape(db * ts, D_MODEL)
    outs = [[] for _ in range(12)]
    for l in range(depth):
        lw = _layer_weights(l, ln_mix_g, w_in, fox_qn_g, fox_kn_g, fox_f_b, rwkv_mu, rwkv_w0, rwkv_w2, rwkv_a0,
                            rwkv_a2, rwkv_g2, rwkv_k_k, rwkv_k_a, rwkv_r_k, rwkv_lnx_w, rwkv_lnx_b, ret_gn_w,
                            w_out, ln_ffn_g, w_gate, w_up, w_down)
        fq, fk, fv, fl, p_rwkv, p_ret = _inproj(yp, lw['ln_mix_g'], lw['w_in'], ROWS_PROMPT)
        kt, vt, lft, qx, kx, vb = _foxprep_prompt(fq, fk, fl, fv, lw['qg'], lw['kg'], lw['fb'], nb, ROWS_FOXPREP)
        fo = _foxattn(qx, kx, vb, nb, ATTN_BLOCK)
        p3 = p_rwkv.reshape(nb, t, RWKV_PROJ)
        eo, r_t = _retention(p_ret.reshape(nb, t, RET_PROJ), cos_p, sin_p, lg, lgh, lw['gn'],
                             jnp.zeros((2 * nb, LANES, LANES), F32), RET_SEQS)
        yp, s_t = _mix_and_ffn(
            yp, fo, p_rwkv, jnp.zeros((nb, RWKV_PROJ), F32), jnp.zeros((2 * nb, HEAD_DIM, LANES), F32),
            eo.reshape(nb * t, RET_W), lw, nb, t, ROWS_PROMPT, nb, SCAN_STEPS)
        outs[0].append(kt); outs[1].append(vt); outs[2].append(lft)
        outs[3].append(_rwkv_state_out(s_t, nb)); outs[4].append(p3[:, -1]); outs[5].append(_ret_state_out(r_t, nb))
        fq, fk, fv, fl, p_rwkv, p_ret = _inproj(ys, lw['ln_mix_g'], lw['w_in'], ROWS_SAMPLE)
        qn, kn, lf = _foxprep_sample(fq, fk, fl, lw['qg'], lw['kg'], lw['fb'], ROWS_SAMPLE)
        fo = _foxsample(page_table, l, pad8(qn), pad8(kn), pad8(fv), pad8(lf), ckt, cvt, clt, ts)
        p3 = p_rwkv.reshape(db, ts, RWKV_PROJ)
        xt = jnp.transpose(p_ret.reshape(db, ts, 4, RET_H, HEAD_DIM), (1, 2, 3, 4, 0))
        gn_cols = jnp.broadcast_to(lw['gn'].reshape(RET_H, HEAD_DIM, 1), (RET_H, HEAD_DIM, LANES))
        eo, r_t = _retention_short(xt, cos_s, sin_s, lg_rows, gn_cols, jnp.transpose(state_ret[l], (1, 2, 3, 0)))
        eo = jnp.transpose(eo, (3, 0, 1, 2)).reshape(db * ts, RET_W)
        ys, s_t = _mix_and_ffn(
            ys, fo.reshape(db * ts, FOX_W), p_rwkv, state_rwkv_shift[l], jnp.transpose(state_rwkv[l], (1, 2, 3, 0)),
            eo, lw, db, ts, ROWS_SAMPLE, None, None)
        outs[6].append(kn); outs[7].append(fv); outs[8].append(lf[:, :FOX_H])
        outs[9].append(jnp.transpose(s_t, (3, 0, 1, 2))); outs[10].append(p3[:, -1])
        outs[11].append(jnp.transpose(r_t, (3, 0, 1, 2)))

    n_pp = nb * t // PAGE_SIZE
    st = lambda i: jnp.stack(outs[i])
    page_rows = lambda z: jnp.transpose(z.reshape(depth, n_pp, FOX_H, HEAD_DIM, PAGE_SIZE), (0, 1, 4, 2, 3))
    return (yp.reshape(nb, t, D_MODEL), ys.reshape(db, ts, D_MODEL),
            page_rows(st(0)), page_rows(st(1)), jnp.swapaxes(st(2), 2, 3),
            st(3), st(4), st(5),
            st(6).reshape(depth, db, ts, FOX_H, HEAD_DIM),
            st(7).reshape(depth, db, ts, FOX_H, HEAD_DIM),
            st(8).reshape(depth, db, ts, FOX_H),
            st(9), st(10), st(11))
```

```python
import functools

import jax
import jax.numpy as jnp
import numpy as np
from jax import lax
from jax.experimental import pallas as pl
from jax.experimental.pallas import tpu as pltpu

F32 = jnp.float32
BF16 = jnp.bfloat16

LANES = 128
SUBLANES = 8
VMEM_LIMIT = 56 * 1024 * 1024

D_MODEL = 1024
HEAD_DIM = 64
FOX_H = 8
RWKV_H = 4
RET_H = 4
FOX_W = FOX_H * HEAD_DIM
RWKV_W = RWKV_H * HEAD_DIM
RET_W = RET_H * HEAD_DIM
RWKV_LORA_W = 32
RWKV_LORA_A = 32
RWKV_LORA_G = 64
RWKV_PROJ = 3 * RWKV_W + RWKV_LORA_W + RWKV_LORA_A + RWKV_LORA_G
RET_PROJ = 4 * RET_W
D_FF = 2816
PAGE_SIZE = 128
RET_CHUNK = 128
ROPE_BASE = 10000.0
NORM_EPS = 1e-6
RWKV_GN_EPS = 64e-5
RET_GN_EPS = 1e-5
NEG_BIG = -1e30
HEADS_PER_TILE = LANES // HEAD_DIM

ROWS_PROMPT = 512
ROWS_SAMPLE = 256
ROWS_FOXPREP = 256
ATTN_BLOCK = 512
SCAN_STEPS = 128
SCAN_UNROLL = 16
RET_SEQS = 4


def _cparams(sem):
    return pltpu.CompilerParams(dimension_semantics=sem, vmem_limit_bytes=VMEM_LIMIT)


def _resident(shape):
    nd = len(shape)
    return pl.BlockSpec(shape, lambda *_: (0,) * nd, pipeline_mode=pl.Buffered(1))


def _bdot(a, b):
    return jnp.dot(a.astype(BF16), b.astype(BF16), preferred_element_type=F32)


def _bdot_nt(a, b):
    return lax.dot_general(a.astype(BF16), b.astype(BF16), (((1,), (1,)), ((), ())),
                           preferred_element_type=F32)


def _split2(x):
    hi = x.astype(BF16)
    lo = (x - hi.astype(F32)).astype(BF16)
    return hi, lo


def _split3(x):
    hi = x.astype(BF16)
    r = x - hi.astype(F32)
    mid = r.astype(BF16)
    lo = (r - mid.astype(F32)).astype(BF16)
    return hi, mid, lo


def _iota(shape, axis):
    return lax.broadcasted_iota(jnp.int32, shape, axis)


def _seg_ones():
    return (_iota((LANES, LANES), 0) // HEAD_DIM == _iota((LANES, LANES), 1) // HEAD_DIM).astype(BF16)


def _seg_sum(x, e):
    n, nblk = x.shape[0], x.shape[-1] // LANES
    stacked = x if nblk == 1 else jnp.concatenate([x[:, c * LANES:(c + 1) * LANES] for c in range(nblk)], axis=0)
    hi, lo = _split2(stacked)
    r = jnp.dot(jnp.concatenate([hi, lo], axis=0), e, preferred_element_type=F32)
    r = r[:n * nblk] + r[n * nblk:]
    return r if nblk == 1 else jnp.concatenate([r[c * n:(c + 1) * n] for c in range(nblk)], axis=-1)


def _exact_dot(x, m01, left):
    parts = _split3(x)
    n, w = x.shape
    if left:
        r = jnp.dot(m01, jnp.concatenate(parts, axis=1), preferred_element_type=F32)
        return r[:, :w] + r[:, w:2 * w] + r[:, 2 * w:]
    r = jnp.dot(jnp.concatenate(parts, axis=0), m01, preferred_element_type=F32)
    return r[:n] + r[n:2 * n] + r[2 * n:]


def _sigmoid(x):
    return 1.0 / (1.0 + jnp.exp(-x))


def _softplus(x):
    return jnp.maximum(x, 0.0) + jnp.log1p(jnp.exp(-jnp.abs(x)))


def _inproj_kernel(x_ref, g_ref, wq_ref, wk_ref, wv_ref, wl_ref, wr_ref, we_ref,
                   oq_ref, ok_ref, ov_ref, ol_ref, or_ref, oe_ref):
    x = x_ref[...]
    h = x * lax.rsqrt(jnp.mean(x * x, -1, keepdims=True) + NORM_EPS) * g_ref[...]
    hb = h.astype(BF16)
    for w_ref, o_ref in ((wq_ref, oq_ref), (wk_ref, ok_ref), (wv_ref, ov_ref), (wl_ref, ol_ref),
                         (wr_ref, or_ref), (we_ref, oe_ref)):
        o_ref[...] = jnp.dot(hb, w_ref[...], preferred_element_type=F32)


def _inproj(x, g, ws, tm):
    n = x.shape[0]
    widths = [w.shape[1] for w in ws]
    row = lambda wd: pl.BlockSpec((tm, wd), lambda i: (i, 0))
    return pl.pallas_call(
        _inproj_kernel,
        grid=(n // tm,),
        in_specs=[row(D_MODEL), _resident((1, D_MODEL))] + [_resident(w.shape) for w in ws],
        out_specs=[row(wd) for wd in widths],
        out_shape=[jax.ShapeDtypeStruct((n, wd), F32) for wd in widths],
        compiler_params=_cparams(("parallel",)),
        name="inproj",
    )(x, g, *ws)


def _fox_norms(q_ref, k_ref, fl_ref, qg_ref, kg_ref, fb_ref):
    e = _seg_ones()

    def hnorm(x, g):
        ms = _seg_sum(x * x, e) * (1.0 / HEAD_DIM)
        return x * lax.rsqrt(ms + NORM_EPS) * g

    qn = hnorm(q_ref[...], qg_ref[...]) * (HEAD_DIM ** -0.5)
    kn = hnorm(k_ref[...], kg_ref[...])
    z = fl_ref[...] + fb_ref[...]
    lf = jnp.minimum(z, 0.0) - jnp.log1p(jnp.exp(-jnp.abs(z)))
    return qn, kn, lf


def _foxprep_sample_kernel(q_ref, k_ref, fl_ref, qg_ref, kg_ref, fb_ref, qn_ref, kn_ref, lf_ref):
    qn_ref[...], kn_ref[...], lf_ref[...] = _fox_norms(q_ref, k_ref, fl_ref, qg_ref, kg_ref, fb_ref)


def _bias_placement():
    m = np.zeros((LANES, 2 * FOX_H * LANES), np.float32)
    one = 3 * FOX_H
    for h in range(FOX_H):
        spare = h * LANES + HEAD_DIM * (1 - h % HEADS_PER_TILE)
        kspare = FOX_H * LANES + spare
        for term in range(3):
            m[term * FOX_H + h, spare + term] = 1.0
            m[one, spare + 3 + term] = 1.0
            m[one, kspare + term] = 1.0
            m[term * FOX_H + h, kspare + 3 + term] = -1.0
    return jnp.asarray(m, BF16)


def _foxprep_prompt_kernel(q_ref, k_ref, fl_ref, qg_ref, kg_ref, fb_ref, v_ref, place_ref,
                           kt_ref, vt_ref, lft_ref, qx_ref, kx_ref, vb_ref, carry_ref, *, tm):
    qn, kn, lf = _fox_norms(q_ref, k_ref, fl_ref, qg_ref, kg_ref, fb_ref)
    v = v_ref[...]
    vb_ref[...] = v.astype(BF16)
    for pg in range(tm // PAGE_SIZE):
        rows = slice(pg * PAGE_SIZE, (pg + 1) * PAGE_SIZE)
        kt_ref[pg] = kn[rows, :].T
        vt_ref[pg] = v[rows, :].T
        lft_ref[pg] = lf[rows, :].T[:FOX_H, :]

    @pl.when(pl.program_id(1) == 0)
    def _():
        carry_ref[...] = jnp.zeros_like(carry_ref)

    tri = (_iota((tm, tm), 0) >= _iota((tm, tm), 1)).astype(BF16)
    c = _exact_dot(lf, tri, left=True) + carry_ref[...]
    carry_ref[...] = c[tm - 1:tm, :]

    lane = _iota((tm, LANES), 1)
    c_hi, c_mid, c_lo = (p.astype(F32) for p in _split3(c))
    packed = jnp.where(lane < FOX_H, c_hi,
                       jnp.where(lane < 2 * FOX_H, pltpu.roll(c_mid, FOX_H, 1),
                                 jnp.where(lane < 3 * FOX_H, pltpu.roll(c_lo, 2 * FOX_H, 1),
                                           jnp.where(lane == 3 * FOX_H, 1.0, 0.0))))
    ext = jnp.dot(packed.astype(BF16), place_ref[...], preferred_element_type=F32)
    for h in range(FOX_H):
        hp, hh = divmod(h, HEADS_PER_TILE)
        own = lane // HEAD_DIM == hh
        pair = slice(hp * LANES, (hp + 1) * LANES)
        tile = slice(h * LANES, (h + 1) * LANES)
        ktile = slice((FOX_H + h) * LANES, (FOX_H + h + 1) * LANES)
        qx_ref[:, tile] = jnp.where(own, qn[:, pair], ext[:, tile]).astype(BF16)
        kx_ref[:, tile] = jnp.where(own, kn[:, pair], ext[:, ktile]).astype(BF16)


def _foxprep_sample(fq, fk, fl, qg, kg, fb, tm):
    n = fq.shape[0]
    row = lambda wd: pl.BlockSpec((tm, wd), lambda i: (i, 0))
    return pl.pallas_call(
        _foxprep_sample_kernel,
        grid=(n // tm,),
        in_specs=[row(FOX_W), row(FOX_W), row(LANES), _resident((1, FOX_W)), _resident((1, FOX_W)),
                  _resident((1, LANES))],
        out_specs=[row(FOX_W), row(FOX_W), row(LANES)],
        out_shape=[jax.ShapeDtypeStruct((n, FOX_W), F32), jax.ShapeDtypeStruct((n, FOX_W), F32),
                   jax.ShapeDtypeStruct((n, LANES), F32)],
        compiler_params=_cparams(("parallel",)),
        name="foxprep_sample",
    )(fq, fk, fl, qg, kg, fb)


def _foxprep_prompt(fq, fk, fl, fv, qg, kg, fb, nseq, tm):
    n = fq.shape[0]
    nt = n // nseq // tm
    ppt = tm // PAGE_SIZE
    n_pp = n // PAGE_SIZE
    place = _bias_placement()
    row = lambda wd: pl.BlockSpec((tm, wd), lambda b, j: (b * nt + j, 0))
    pages = lambda r: pl.BlockSpec((ppt, r, PAGE_SIZE), lambda b, j: (b * nt + j, 0, 0))
    return pl.pallas_call(
        functools.partial(_foxprep_prompt_kernel, tm=tm),
        grid=(nseq, nt),
        in_specs=[row(FOX_W), row(FOX_W), row(LANES), _resident((1, FOX_W)), _resident((1, FOX_W)),
                  _resident((1, LANES)), row(FOX_W), _resident(place.shape)],
        out_specs=[pages(FOX_W), pages(FOX_W), pages(FOX_H), row(FOX_H * LANES), row(FOX_H * LANES), row(FOX_W)],
        out_shape=[jax.ShapeDtypeStruct((n_pp, FOX_W, PAGE_SIZE), F32), jax.ShapeDtypeStruct((n_pp, FOX_W, PAGE_SIZE), F32),
                   jax.ShapeDtypeStruct((n_pp, FOX_H, PAGE_SIZE), F32),
                   jax.ShapeDtypeStruct((n, FOX_H * LANES), BF16), jax.ShapeDtypeStruct((n, FOX_H * LANES), BF16),
                   jax.ShapeDtypeStruct((n, FOX_W), BF16)],
        scratch_shapes=[pltpu.VMEM((1, LANES), F32)],
        compiler_params=_cparams(("parallel", "arbitrary")),
        name="foxprep_prompt",
    )(fq, fk, fl, qg, kg, fb, fv, place)


def _foxattn_kernel(qx_ref, kx_ref, vb_ref, o_ref, *, tq):
    qi = pl.program_id(2)
    q = [qx_ref[:, hh * LANES:(hh + 1) * LANES] for hh in range(HEADS_PER_TILE)]

    def block(off, carry, diagonal):
        vb = vb_ref[pl.ds(off, tq), :]
        stats, ps = [], []
        for hh in range(HEADS_PER_TILE):
            m, l, _ = carry[hh]
            kb = kx_ref[pl.ds(off, tq), hh * LANES:(hh + 1) * LANES]
            s = lax.dot_general(q[hh], kb, (((1,), (1,)), ((), ())), preferred_element_type=F32)
            if diagonal:
                s = jnp.where(_iota((tq, tq), 1) <= _iota((tq, tq), 0), s, NEG_BIG)
            m_new = jnp.maximum(m, jnp.max(s, axis=-1, keepdims=True))
            alpha = jnp.exp(m - m_new)
            p = jnp.exp(s - m_new)
            stats.append((m_new, alpha, alpha * l + jnp.sum(p, axis=-1, keepdims=True)))
            ps.append(p.astype(BF16))
        pv = jnp.dot(jnp.concatenate(ps, axis=0), vb, preferred_element_type=F32)
        return tuple((m_new, l, alpha * carry[hh][2] + pv[hh * tq:(hh + 1) * tq])
                     for hh, (m_new, alpha, l) in enumerate(stats))

    init = tuple((jnp.full((tq, 1), NEG_BIG, F32), jnp.zeros((tq, 1), F32), jnp.zeros((tq, LANES), F32))
                 for _ in range(HEADS_PER_TILE))
    carry = lax.fori_loop(0, qi, lambda j, c: block(pl.multiple_of(j * tq, tq), c, False), init)
    carry = block(pl.multiple_of(qi * tq, tq), carry, True)
    outs = [acc / l for _, l, acc in carry]
    o_ref[...] = jnp.where(_iota((tq, LANES), 1) // HEAD_DIM == 0, outs[0], outs[1])


def _foxattn(qx, kx, vb, nseq, tq):
    n = qx.shape[0]
    t = n // nseq
    nq = t // tq
    npair = FOX_H // HEADS_PER_TILE
    pair_w = HEADS_PER_TILE * LANES
    return pl.pallas_call(
        functools.partial(_foxattn_kernel, tq=tq),
        grid=(nseq, npair, nq),
        in_specs=[
            pl.BlockSpec((tq, pair_w), lambda b, h, i: (b * nq + i, h)),
            pl.BlockSpec((t, pair_w), lambda b, h, i: (b, h)),
            pl.BlockSpec((t, LANES), lambda b, h, i: (b, h)),
        ],
        out_specs=pl.BlockSpec((tq, LANES), lambda b, h, i: (b * nq + i, h)),
        out_shape=jax.ShapeDtypeStruct((n, FOX_W), F32),
        compiler_params=_cparams(("parallel", "parallel", "arbitrary")),
        name="foxattn",
    )(qx, kx, vb)


def _foxsample_kernel(pt_ref, q_ref, kn_ref, vn_ref, lfn_ref, *rest, n_pages, t_new):
    del pt_ref
    kp = rest[:n_pages]
    vp = rest[n_pages:2 * n_pages]
    lp = rest[2 * n_pages:3 * n_pages]
    o_ref = rest[3 * n_pages]
    nrow = t_new * FOX_H

    rep = lambda x: jnp.concatenate([x] * t_new, axis=0)
    q = q_ref[...]
    hmask = _iota((FOX_H, FOX_W), 1) // HEAD_DIM == _iota((FOX_H, FOX_W), 0)
    qbd = jnp.concatenate(
        [jnp.where(hmask, jnp.broadcast_to(q[t:t + 1, :], (FOX_H, FOX_W)), 0.0) for t in range(t_new)],
        axis=0).astype(BF16)

    lfn = lfn_ref[...]
    diag8 = _iota((FOX_H, LANES), 1) == _iota((FOX_H, LANES), 0)
    cn_cols = []
    run = jnp.zeros((1, LANES), F32)
    for t in range(t_new):
        run = run + lfn[t:t + 1, :]
        cn_cols.append(jnp.sum(jnp.where(diag8, jnp.broadcast_to(run, (FOX_H, LANES)), 0.0), axis=-1, keepdims=True))
    cn = jnp.concatenate(cn_cols, axis=0)

    zeros_tail = jnp.zeros((PAGE_SIZE - SUBLANES, FOX_W), F32)
    key = _iota((nrow, PAGE_SIZE), 1)
    trow = _iota((nrow, PAGE_SIZE), 0) // FOX_H
    ckey = jnp.zeros((nrow, PAGE_SIZE), F32)
    for j in range(t_new):
        ckey = jnp.where(key == j, rep(cn_cols[j]), ckey)
    s_new = _bdot_nt(qbd, jnp.concatenate([kn_ref[...], zeros_tail], axis=0)) + (cn - ckey)
    s_new = jnp.where(key <= trow, s_new, NEG_BIG)

    lf_all = jnp.concatenate([lp[i][...] for i in range(n_pages)], axis=0)
    upper = (_iota((PAGE_SIZE, PAGE_SIZE), 0) <= _iota((PAGE_SIZE, PAGE_SIZE), 1)).astype(BF16)
    cp_all = _exact_dot(lf_all, upper, left=False)
    before = [jnp.zeros((FOX_H, 1), F32)]
    for i in range(n_pages):
        before.append(before[-1] + cp_all[i * FOX_H:(i + 1) * FOX_H, PAGE_SIZE - 1:PAGE_SIZE])
    cq_abs = rep(before[n_pages]) + cn
    s_past = []
    for i in range(n_pages):
        cp = rep(cp_all[i * FOX_H:(i + 1) * FOX_H, :] + before[i])
        s_past.append(_bdot(qbd, kp[i][...].reshape(FOX_W, PAGE_SIZE)) + (cq_abs - cp))

    m = jnp.max(s_new, axis=-1, keepdims=True)
    for s in s_past:
        m = jnp.maximum(m, jnp.max(s, axis=-1, keepdims=True))
    p = jnp.exp(s_new - m)
    l = jnp.sum(p, axis=-1, keepdims=True)
    o = _bdot(p, jnp.concatenate([vn_ref[...], zeros_tail], axis=0))
    for i in range(n_pages):
        p = jnp.exp(s_past[i] - m)
        l = l + jnp.sum(p, axis=-1, keepdims=True)
        o = o + _bdot_nt(p, vp[i][...].reshape(FOX_W, PAGE_SIZE))
    o = o / l
    omask = _iota((nrow, FOX_W), 1) // HEAD_DIM == _iota((nrow, FOX_W), 0) % FOX_H
    o_ref[...] = jnp.sum(jnp.where(omask, o, 0.0).reshape(t_new, FOX_H, FOX_W), axis=1)


def _foxsample(page_table, layer, qn, kn, vn, lfn, cache_kt, cache_vt, cache_lft, t_new):
    db, n_pages = page_table.shape
    pt = page_table.reshape(-1)
    new = lambda wd: pl.BlockSpec((None, SUBLANES, wd), lambda b, pt: (b, 0, 0))

    def page(shape, i):
        nd = len(shape)
        return pl.BlockSpec((None, None) + shape, lambda b, pt: (layer, pt[b * n_pages + i]) + (0,) * nd)

    in_specs = [new(FOX_W), new(FOX_W), new(FOX_W), new(LANES)]
    in_specs += [page((FOX_H, HEAD_DIM, PAGE_SIZE), i) for i in range(n_pages)]
    in_specs += [page((FOX_H, HEAD_DIM, PAGE_SIZE), i) for i in range(n_pages)]
    in_specs += [page((FOX_H, PAGE_SIZE), i) for i in range(n_pages)]
    return pl.pallas_call(
        functools.partial(_foxsample_kernel, n_pages=n_pages, t_new=t_new),
        grid_spec=pltpu.PrefetchScalarGridSpec(
            num_scalar_prefetch=1,
            grid=(db,),
            in_specs=in_specs,
            out_specs=pl.BlockSpec((None, t_new, FOX_W), lambda b, pt: (b, 0, 0)),
        ),
        out_shape=jax.ShapeDtypeStruct((db, t_new, FOX_W), F32),
        compiler_params=_cparams(("arbitrary",)),
        name="foxsample",
    )(pt, qn, kn, vn, lfn, *([cache_kt] * n_pages), *([cache_vt] * n_pages), *([cache_lft] * n_pages))


def _rwkvprep_kernel(p_ref, init_ref, mu_ref, w0_ref, w2_ref, a0_ref, a2_ref, g2_ref, kk_ref, ka_ref,
                     r_out, w_out, k_out, v_out, a_out, b_out, g_out, *carry, tm, period):
    e = _seg_ones()
    p = p_ref[...]
    rolled = pltpu.roll(p, 1, 0)
    row = _iota((tm, RWKV_PROJ), 0)
    if period >= tm:
        carry_ref, = carry

        @pl.when(pl.program_id(1) == 0)
        def _():
            carry_ref[...] = init_ref[...]

        pp = jnp.where(row == 0, carry_ref[...], rolled)
        carry_ref[...] = p[tm - 1:tm, :]
    else:
        pp = jnp.where(row % period == 0, init_ref[...], rolled)
    xs = p + (pp - p) * mu_ref[...]
    r = xs[:, 0:RWKV_W]
    k = xs[:, RWKV_W:2 * RWKV_W]
    v = xs[:, 2 * RWKV_W:3 * RWKV_W]
    lo = xs[:, 3 * RWKV_W:]
    w = -_softplus(-(w0_ref[...] + _bdot(jnp.tanh(lo), w2_ref[...]))) - 0.5
    a = _sigmoid(a0_ref[...] + _bdot(lo, a2_ref[...]))
    kk = k * kk_ref[...]
    nrm = jnp.sqrt(_seg_sum(kk * kk, e))
    kk = kk / jnp.maximum(nrm, 1e-12)
    r_out[...] = r
    w_out[...] = jnp.exp(-jnp.exp(w))
    k_out[...] = k * (1.0 + (a - 1.0) * ka_ref[...])
    v_out[...] = v
    a_out[...] = -kk
    b_out[...] = kk * a
    g_out[...] = _bdot(_sigmoid(lo), g2_ref[...])


def _rwkvprep(p, shift, period, mu, w0, w2p, a0, a2p, g2p, k_k, k_a, tm):
    n = p.shape[0]
    nseq = n // period
    vec = _resident((1, RWKV_W))
    lora = _resident((LANES, RWKV_W))
    if period >= tm:
        nt = period // tm
        grid = (nseq, nt)
        row = lambda wd: pl.BlockSpec((tm, wd), lambda b, j: (b * nt + j, 0))
        init = shift[:, None, :]
        init_spec = pl.BlockSpec((None, 1, RWKV_PROJ), lambda b, j: (b, 0, 0))
        scratch = [pltpu.VMEM((1, RWKV_PROJ), F32)]
        sem = ("parallel", "arbitrary")
    else:
        grid = (n // tm,)
        row = lambda wd: pl.BlockSpec((tm, wd), lambda i: (i, 0))
        init = jnp.repeat(shift, period, axis=0)
        init_spec = row(RWKV_PROJ)
        scratch = []
        sem = ("parallel",)
    return pl.pallas_call(
        functools.partial(_rwkvprep_kernel, tm=tm, period=period),
        grid=grid,
        in_specs=[row(RWKV_PROJ), init_spec, _resident((1, RWKV_PROJ)), vec, lora, vec, lora, lora, vec, vec],
        out_specs=[row(RWKV_W)] * 7,
        out_shape=[jax.ShapeDtypeStruct((n, RWKV_W), F32)] * 7,
        scratch_shapes=scratch,
        compiler_params=_cparams(sem),
        name="rwkvprep",
    )(p, init, mu, w0, w2p, a0, a2p, g2p, k_k, k_a)


def _rwkvscan_kernel(r_ref, w_ref, k_ref, v_ref, a_ref, b_ref, s0_ref, y_ref, st_ref, s_scr, *, bblk, tblk):
    tb = pl.program_id(1)
    ng = HEADS_PER_TILE * bblk

    @pl.when(tb == 0)
    def _():
        s_scr[...] = s0_ref[...]

    e = _seg_ones()
    diag = (_iota((HEAD_DIM, LANES), 1) % HEAD_DIM == _iota((HEAD_DIM, LANES), 0)).astype(BF16)[None]
    half = (_iota((SUBLANES, LANES), 1) // HEAD_DIM == _iota((SUBLANES, LANES), 0)).astype(BF16)
    first = _iota((1, LANES), 1) < HEAD_DIM

    sub = min(SCAN_UNROLL, tblk)

    def chunk(c, carry):
        off = pl.multiple_of(c * sub, sub)
        tiles = [[ref[b, pl.ds(off, sub), :] for b in range(bblk)] for ref in (r_ref, w_ref, k_ref, v_ref, a_ref, b_ref)]

        def rows(kind, i):
            return jnp.concatenate(
                [tiles[kind][b][i:i + 1, hp * LANES:(hp + 1) * LANES][None] for b in range(bblk) for hp in range(2)],
                axis=0)

        vdiag = jnp.concatenate([rows(3, i).astype(BF16) * diag for i in range(sub)], axis=0)
        vcol = jnp.dot(vdiag.reshape(sub * ng * HEAD_DIM, LANES), e,
                       preferred_element_type=F32).reshape(sub * ng, HEAD_DIM, LANES)

        s = s_scr[...]
        srs = []
        for i in range(sub):
            sa = jnp.dot((s * rows(4, i)).reshape(ng * HEAD_DIM, LANES).astype(BF16), e,
                         preferred_element_type=F32).reshape(ng, HEAD_DIM, LANES)
            s = s * rows(1, i) + sa * rows(5, i) + vcol[i * ng:(i + 1) * ng] * rows(2, i)
            srs.append((s * rows(0, i)).astype(BF16))
        s_scr[...] = s

        sr = jnp.concatenate(srs, axis=0).reshape(sub * ng * HEAD_DIM, LANES)
        out = lax.dot_general(half, sr, (((1,), (1,)), ((), ())), preferred_element_type=F32)
        for b in range(bblk):
            ytile = []
            for i in range(sub):
                lanes = slice((i * bblk + b) * 2 * HEAD_DIM, (i * bblk + b + 1) * 2 * HEAD_DIM)
                h0, h1 = out[0:1, lanes], out[1:2, lanes]
                ytile.append(jnp.concatenate(
                    [jnp.where(first, h0, pltpu.roll(h1, HEAD_DIM, 1)),
                     jnp.where(first, pltpu.roll(h0, HEAD_DIM, 1), h1)], axis=1))
            y_ref[b, pl.ds(off, sub), :] = jnp.concatenate(ytile, axis=0)
        return carry

    lax.fori_loop(0, tblk // sub, chunk, 0)

    @pl.when(tb == pl.num_programs(1) - 1)
    def _():
        st_ref[...] = s_scr[...]


def _rwkvscan(r, w, k, v, a, b, s0, bblk, tblk):
    nb, t, _ = r.shape
    ng = HEADS_PER_TILE * bblk
    seq = pl.BlockSpec((bblk, tblk, RWKV_W), lambda i, j: (i, j, 0))
    st = pl.BlockSpec((ng, HEAD_DIM, LANES), lambda i, j: (i, 0, 0))
    return pl.pallas_call(
        functools.partial(_rwkvscan_kernel, bblk=bblk, tblk=tblk),
        grid=(nb // bblk, t // tblk),
        in_specs=[seq] * 6 + [st],
        out_specs=[seq, st],
        out_shape=[jax.ShapeDtypeStruct((nb, t, RWKV_W), F32), jax.ShapeDtypeStruct(s0.shape, F32)],
        scratch_shapes=[pltpu.VMEM((ng, HEAD_DIM, LANES), F32)],
        compiler_params=_cparams(("parallel", "arbitrary")),
        name="rwkvscan",
    )(r, w, k, v, a, b, s0)


def _rwkv_short_kernel(z_ref, s0_ref, y_ref, st_ref):
    nt = z_ref.shape[0]
    R, W, K, V, A, B = range(6)

    def value_rows(i, carry):
        base = pl.multiple_of(i * SUBLANES, SUBLANES)
        vt = [z_ref[t, V, pl.ds(base, SUBLANES), :] for t in range(nt)]
        yrows = [[] for _ in range(nt)]
        for j in range(SUBLANES):
            s = s0_ref[base + j]
            for t in range(nt):
                sa = jnp.sum(s * z_ref[t, A], axis=0, keepdims=True)
                s = s * z_ref[t, W] + sa * z_ref[t, B] + vt[t][j:j + 1, :] * z_ref[t, K]
                yrows[t].append(jnp.sum(s * z_ref[t, R], axis=0, keepdims=True))
            st_ref[base + j] = s
        for t in range(nt):
            y_ref[t, pl.ds(base, SUBLANES), :] = jnp.concatenate(yrows[t], axis=0)
        return carry

    lax.fori_loop(0, HEAD_DIM // SUBLANES, value_rows, 0)


def _rwkv_short(zt, s0):
    nt, _, nh, _, nb = zt.shape
    assert nb == LANES and nt <= SUBLANES
    st = pl.BlockSpec((None, HEAD_DIM, HEAD_DIM, LANES), lambda h: (h, 0, 0, 0))
    return pl.pallas_call(
        _rwkv_short_kernel,
        grid=(nh,),
        in_specs=[pl.BlockSpec((nt, 6, None, HEAD_DIM, LANES), lambda h: (0, 0, h, 0, 0)), st],
        out_specs=[pl.BlockSpec((nt, None, HEAD_DIM, LANES), lambda h: (0, h, 0, 0)), st],
        out_shape=[jax.ShapeDtypeStruct((nt, nh, HEAD_DIM, LANES), F32), jax.ShapeDtypeStruct(s0.shape, F32)],
        compiler_params=_cparams(("parallel",)),
        name="rwkv_short",
    )(zt, s0)


def _rope_kernel(inv_ref, cos_ref, sin_ref, *, pos0):
    t = cos_ref.shape[0]
    pos = (pos0 + _iota((t, LANES), 0)).astype(F32)
    ang = pos * inv_ref[...]
    first_half = _iota((t, LANES), 1) % HEAD_DIM < HEAD_DIM // 2
    cos_ref[...] = jnp.cos(ang)
    sin_ref[...] = jnp.where(first_half, -jnp.sin(ang), jnp.sin(ang))


def _rope_tables(inv_lanes, t, pos0):
    return pl.pallas_call(
        functools.partial(_rope_kernel, pos0=pos0),
        out_shape=[jax.ShapeDtypeStruct((t, LANES), F32)] * 2,
        name="rope",
    )(inv_lanes)


def _ret_kernel(x_ref, cos_ref, sin_ref, lg_ref, lgh_ref, gn_ref, r0_ref, o_ref, rt_ref, r_scr,
                *, bs):
    lp = RET_CHUNK
    npair = RET_H // HEADS_PER_TILE
    c = pl.program_id(1)

    @pl.when(c == 0)
    def _():
        r_scr[...] = r0_ref[...]

    lane = _iota((lp, LANES), 1)
    first_half = lane % HEAD_DIM < HEAD_DIM // 2
    cos = cos_ref[...]
    sin = sin_ref[...]

    def rope(x):
        swapped = jnp.where(first_half, pltpu.roll(x, LANES - HEAD_DIM // 2, 1), pltpu.roll(x, HEAD_DIM // 2, 1))
        return x * cos + swapped * sin

    idx = _iota((lp, 1), 0).astype(F32)
    diff = (_iota((lp, lp), 0) - _iota((lp, lp), 1)).astype(F32)
    same_head = _iota((LANES, LANES), 0) // HEAD_DIM == _iota((LANES, LANES), 1) // HEAD_DIM
    e = _seg_ones()
    dmask, cross, kdec, cdec = [], [], [], []
    for hp in range(npair):
        lg = lg_ref[hp]
        dmask.append(jnp.concatenate(
            [jnp.where(diff >= 0.0, jnp.exp(lgh_ref[hp, hh:hh + 1, :] * jnp.maximum(diff, 0.0)), 0.0)
             for hh in range(HEADS_PER_TILE)], axis=0))
        cross.append(jnp.exp(lg * (idx + 1.0)))
        kdec.append(jnp.exp(lg * (lp - 1.0 - idx)))
        cdec.append(jnp.exp(lg * float(lp)))

    chains = [(s, hp) for s in range(bs) for hp in range(npair)]
    os_ = []
    for s, hp in chains:
        col = lambda j: x_ref[s, :, (j * npair + hp) * LANES:(j * npair + hp + 1) * LANES]
        q = rope(col(0))
        k = rope(col(1)) * (HEAD_DIM ** -0.5)
        vb = col(2).astype(BF16)
        qh = jnp.concatenate([jnp.where(lane // HEAD_DIM == hh, q, 0.0) for hh in range(HEADS_PER_TILE)], axis=0)
        sc = _bdot_nt(qh, k) * dmask[hp]
        ov = jnp.dot(sc.astype(BF16), vb, preferred_element_type=F32)
        o = jnp.where(lane // HEAD_DIM == 0, ov[:lp], ov[lp:])
        r = r_scr[s * npair + hp]
        os_.append(o + _bdot(q, r) * cross[hp])
        upd = jnp.dot((k * kdec[hp]).T.astype(BF16), vb, preferred_element_type=F32)
        r_scr[s * npair + hp] = r * cdec[hp] + jnp.where(same_head, upd, 0.0)

    o_all = jnp.concatenate(os_, axis=0)
    xc = o_all - _seg_sum(o_all, e) * (1.0 / HEAD_DIM)
    yn = xc * lax.rsqrt(_seg_sum(xc * xc, e) * (1.0 / HEAD_DIM) + RET_GN_EPS)
    for i, (s, hp) in enumerate(chains):
        g = x_ref[s, :, (3 * npair + hp) * LANES:(3 * npair + hp + 1) * LANES]
        o_ref[s, :, hp * LANES:(hp + 1) * LANES] = (
            yn[i * lp:(i + 1) * lp] * gn_ref[:, hp * LANES:(hp + 1) * LANES] * (g * _sigmoid(g)))

    @pl.when(c == pl.num_programs(1) - 1)
    def _():
        rt_ref[...] = r_scr[...]


def _retention(proj, cos, sin, lg, lgh, gn, r0, bs):
    nb, t, _ = proj.shape
    l_in = RET_CHUNK
    nc = t // l_in
    npair = RET_H // HEADS_PER_TILE
    st = pl.BlockSpec((bs * npair, LANES, LANES), lambda b, c: (b, 0, 0))
    tab = pl.BlockSpec((l_in, LANES), lambda b, c: (c, 0))
    return pl.pallas_call(
        functools.partial(_ret_kernel, bs=bs),
        grid=(nb // bs, nc),
        in_specs=[pl.BlockSpec((bs, l_in, 4 * RET_W), lambda b, c: (b, c, 0)), tab, tab,
                  _resident(lg.shape), _resident(lgh.shape), _resident(gn.shape), st],
        out_specs=[pl.BlockSpec((bs, l_in, RET_W), lambda b, c: (b, c, 0)), st],
        out_shape=[jax.ShapeDtypeStruct((nb, t, RET_W), F32), jax.ShapeDtypeStruct(r0.shape, F32)],
        scratch_shapes=[pltpu.VMEM((bs * npair, LANES, LANES), F32)],
        compiler_params=_cparams(("parallel", "arbitrary")),
        name="retention",
    )(proj, cos, sin, lg, lgh, gn, r0)


def _rope_cols_kernel(inv_ref, cos_ref, sin_ref, *, pos0):
    for t in range(cos_ref.shape[0]):
        ang = float(pos0 + t) * inv_ref[...]
        cos_ref[t] = jnp.cos(ang)
        sin_ref[t] = jnp.sin(ang)


def _rope_cols(inv_cols, t, pos0):
    return pl.pallas_call(
        functools.partial(_rope_cols_kernel, pos0=pos0),
        out_shape=[jax.ShapeDtypeStruct((t, HEAD_DIM // 2, LANES), F32)] * 2,
        name="rope_cols",
    )(inv_cols)


def _ret_short_kernel(x_ref, cos_ref, sin_ref, lg_ref, gn_ref, r0_ref, o_ref, rt_ref, q_scr, k_scr):
    nt = x_ref.shape[0]
    half = HEAD_DIM // 2
    lg = lg_ref[0:1, :]
    gam = lambda n: jnp.exp(lg * float(n))

    def rope(x, t):
        x1, x2 = x[:half], x[half:]
        c, s = cos_ref[t], sin_ref[t]
        return jnp.concatenate([x1 * c - x2 * s, x2 * c + x1 * s], axis=0)

    q = [rope(x_ref[t, 0], t) for t in range(nt)]
    k = [rope(x_ref[t, 1], t) * (HEAD_DIM ** -0.5) for t in range(nt)]
    v = [x_ref[t, 2] for t in range(nt)]

    intra = []
    for t in range(nt):
        acc = None
        for t2 in range(t + 1):
            term = (jnp.sum(q[t] * k[t2], axis=0, keepdims=True) * gam(t - t2)) * v[t2]
            acc = term if acc is None else acc + term
        intra.append(acc)

    for t in range(nt):
        q_scr[t] = q[t]
        k_scr[t] = k[t] * gam(nt - 1 - t)
    decay_all = gam(nt)

    def key_rows(i, acc):
        base = pl.multiple_of(i * SUBLANES, SUBLANES)
        qb = [q_scr[t, pl.ds(base, SUBLANES), :] for t in range(nt)]
        kb = [k_scr[t, pl.ds(base, SUBLANES), :] for t in range(nt)]
        acc = list(acc)
        for j in range(SUBLANES):
            r = r0_ref[base + j]
            upd = r * decay_all
            for t in range(nt):
                acc[t] = acc[t] + qb[t][j:j + 1, :] * r
                upd = upd + kb[t][j:j + 1, :] * v[t]
            rt_ref[base + j] = upd
        return tuple(acc)

    cross = lax.fori_loop(0, HEAD_DIM // SUBLANES, key_rows,
                          tuple(jnp.zeros((HEAD_DIM, LANES), F32) for _ in range(nt)))

    for t in range(nt):
        o = intra[t] + cross[t] * gam(t + 1)
        mean = jnp.mean(o, axis=0, keepdims=True)
        xc = o - mean
        var = jnp.mean(xc * xc, axis=0, keepdims=True)
        g = x_ref[t, 3]
        o_ref[t] = xc * lax.rsqrt(var + RET_GN_EPS) * gn_ref[...] * (g * _sigmoid(g))


def _retention_short(xt, cos, sin, lg_rows, gn_cols, r0):
    nt, _, nh, _, nb = xt.shape
    assert nb == LANES and nt <= SUBLANES
    tab = _resident(cos.shape)
    return pl.pallas_call(
        _ret_short_kernel,
        grid=(nh,),
        in_specs=[pl.BlockSpec((nt, 4, None, HEAD_DIM, LANES), lambda h: (0, 0, h, 0, 0)), tab, tab,
                  pl.BlockSpec((None, SUBLANES, LANES), lambda h: (h, 0, 0)),
                  pl.BlockSpec((None, HEAD_DIM, LANES), lambda h: (h, 0, 0)),
                  pl.BlockSpec((None, HEAD_DIM, HEAD_DIM, LANES), lambda h: (h, 0, 0, 0))],
        out_specs=[pl.BlockSpec((nt, None, HEAD_DIM, LANES), lambda h: (0, h, 0, 0)),
                   pl.BlockSpec((None, HEAD_DIM, HEAD_DIM, LANES), lambda h: (h, 0, 0, 0))],
        out_shape=[jax.ShapeDtypeStruct((nt, nh, HEAD_DIM, LANES), F32), jax.ShapeDtypeStruct(r0.shape, F32)],
        scratch_shapes=[pltpu.VMEM((nt, HEAD_DIM, LANES), F32), pltpu.VMEM((nt, HEAD_DIM, LANES), F32)],
        compiler_params=_cparams(("parallel",)),
        name="retention_short",
    )(xt, cos, sin, lg_rows, gn_cols, r0)


def _outproj_kernel(x_ref, fo_ref, y_ref, r_ref, k_ref, v_ref, g_ref, eo_ref, lw_ref, lb_ref, rk_ref,
                    wf_ref, wr_ref, we_ref, o_ref):
    e = _seg_ones()
    y = y_ref[...]
    mean = _seg_sum(y, e) * (1.0 / HEAD_DIM)
    yc = y - mean
    var = _seg_sum(yc * yc, e) * (1.0 / HEAD_DIM)
    yn = yc * lax.rsqrt(var + RWKV_GN_EPS) * lw_ref[...] + lb_ref[...]
    v = v_ref[...]
    bonus = _seg_sum(r_ref[...] * k_ref[...] * rk_ref[...], e) * v
    ro = (yn + bonus) * g_ref[...]
    o_ref[...] = (x_ref[...] + _bdot(fo_ref[...], wf_ref[...]) + _bdot(ro, wr_ref[...])
                  + _bdot(eo_ref[...], we_ref[...]))


def _outproj(x, fo, y, r, k, v, g, eo, lw, lb, rk, wf, wr, we, tm):
    n = x.shape[0]
    row = lambda wd: pl.BlockSpec((tm, wd), lambda i: (i, 0))
    vec = _resident((1, RWKV_W))
    return pl.pallas_call(
        _outproj_kernel,
        grid=(n // tm,),
        in_specs=[row(D_MODEL), row(FOX_W)] + [row(RWKV_W)] * 6 + [vec, vec, vec,
                  _resident(wf.shape), _resident(wr.shape), _resident(we.shape)],
        out_specs=row(D_MODEL),
        out_shape=jax.ShapeDtypeStruct((n, D_MODEL), F32),
        compiler_params=_cparams(("parallel",)),
        name="outproj",
    )(x, fo, y, r, k, v, g, eo, lw, lb, rk, wf, wr, we)


def _ffn_kernel(x_ref, g_ref, wg_ref, wu_ref, wd_ref, o_ref):
    x = x_ref[...]
    h = (x * lax.rsqrt(jnp.mean(x * x, -1, keepdims=True) + NORM_EPS) * g_ref[...]).astype(BF16)
    gate = jnp.dot(h, wg_ref[...], preferred_element_type=F32)
    up = jnp.dot(h, wu_ref[...], preferred_element_type=F32)
    act = gate * _sigmoid(gate) * up
    o_ref[...] = x + _bdot(act, wd_ref[...])


def _ffn(x, g, wg, wu, wd, tm):
    n = x.shape[0]
    row = pl.BlockSpec((tm, D_MODEL), lambda i: (i, 0))
    return pl.pallas_call(
        _ffn_kernel,
        grid=(n // tm,),
        in_specs=[row, _resident((1, D_MODEL)), _resident(wg.shape), _resident(wu.shape), _resident(wd.shape)],
        out_specs=row,
        out_shape=jax.ShapeDtypeStruct((n, D_MODEL), F32),
        compiler_params=_cparams(("parallel",)),
        name="ffn",
    )(x, g, wg, wu, wd)


def _rwkv_state_out(s, nb):
    s = s.reshape(nb, RWKV_H // 2, HEAD_DIM, 2, HEAD_DIM).transpose(0, 1, 3, 2, 4)
    return s.reshape(nb, RWKV_H, HEAD_DIM, HEAD_DIM)


def _ret_state_out(r, nb):
    r = r.reshape(nb, RET_H // 2, LANES, LANES)
    return jnp.stack([r[:, :, :HEAD_DIM, :HEAD_DIM], r[:, :, HEAD_DIM:, HEAD_DIM:]], axis=2).reshape(
        nb, RET_H, HEAD_DIM, HEAD_DIM)


def _layer_weights(l, ln_mix_g, w_in, fox_qn_g, fox_kn_g, fox_f_b, rwkv_mu, rwkv_w0, rwkv_w2, rwkv_a0, rwkv_a2,
                   rwkv_g2, rwkv_k_k, rwkv_k_a, rwkv_r_k, rwkv_lnx_w, rwkv_lnx_b, ret_gn_w, w_out, ln_ffn_g,
                   w_gate, w_up, w_down):
    wi = w_in[l].astype(BF16)
    o_f = 3 * FOX_W
    o_r = o_f + FOX_H
    o_e = o_r + RWKV_PROJ
    pad_rows = lambda w, off: jnp.zeros((LANES, RWKV_W), BF16).at[off:off + w.shape[0]].set(w.astype(BF16))
    wo = w_out[l].astype(BF16)
    return dict(
        ln_mix_g=ln_mix_g[l][None],
        w_in=[wi[:, 0:FOX_W], wi[:, FOX_W:2 * FOX_W], wi[:, 2 * FOX_W:o_f],
              jnp.pad(wi[:, o_f:o_r], ((0, 0), (0, LANES - FOX_H))), wi[:, o_r:o_e], wi[:, o_e:]],
        qg=jnp.tile(fox_qn_g[l], FOX_H)[None], kg=jnp.tile(fox_kn_g[l], FOX_H)[None],
        fb=jnp.pad(fox_f_b[l], (0, LANES - FOX_H))[None],
        mu=rwkv_mu[l][None], w0=rwkv_w0[l][None], a0=rwkv_a0[l][None],
        w2=pad_rows(rwkv_w2[l], 0), a2=pad_rows(rwkv_a2[l], RWKV_LORA_W),
        g2=pad_rows(rwkv_g2[l], RWKV_LORA_W + RWKV_LORA_A),
        k_k=rwkv_k_k[l][None], k_a=rwkv_k_a[l][None], r_k=rwkv_r_k[l].reshape(1, RWKV_W),
        lnx_w=rwkv_lnx_w[l][None], lnx_b=rwkv_lnx_b[l][None], gn=ret_gn_w[l][None],
        wo_f=wo[:FOX_W], wo_r=wo[FOX_W:FOX_W + RWKV_W], wo_e=wo[FOX_W + RWKV_W:],
        ln_ffn_g=ln_ffn_g[l][None],
        w_gate=w_gate[l].astype(BF16), w_up=w_up[l].astype(BF16), w_down=w_down[l].astype(BF16),
    )


def _mix_and_ffn(x2, fo, p_rwkv, shift, s0, eo, lw, nb, t, tm, bblk, tblk):
    r, w, k, v, a, b, g = _rwkvprep(p_rwkv, shift, t, lw['mu'], lw['w0'], lw['w2'], lw['a0'], lw['a2'], lw['g2'],
                                    lw['k_k'], lw['k_a'], tm)
    if s0.ndim == 4:
        zt = jnp.stack([r, w, k, v, a, b], axis=1).reshape(nb, t, 6, RWKV_H, HEAD_DIM)
        y, s_t = _rwkv_short(jnp.transpose(zt, (1, 2, 3, 4, 0)), s0)
        y = jnp.transpose(y, (3, 0, 1, 2))
    else:
        seq = lambda z: z.reshape(nb, t, RWKV_W)
        y, s_t = _rwkvscan(seq(r), seq(w), seq(k), seq(v), seq(a), seq(b), s0, bblk, tblk)
    x2 = _outproj(x2, fo, y.reshape(nb * t, RWKV_W), r, k, v, g, eo, lw['lnx_w'], lw['lnx_b'], lw['r_k'],
                  lw['wo_f'], lw['wo_r'], lw['wo_e'], tm)
    x2 = _ffn(x2, lw['ln_ffn_g'], lw['w_gate'], lw['w_up'], lw['w_down'], tm)
    return x2, s_t


def kernel(x_prompt, x_sample, cache_fox_k, cache_fox_v, cache_fox_logf, state_rwkv, state_rwkv_shift, state_ret,
           page_table, ln_mix_g, w_in, fox_qn_g, fox_kn_g, fox_f_b, rwkv_mu, rwkv_w0, rwkv_w2, rwkv_a0, rwkv_a2,
           rwkv_g2, rwkv_k_k, rwkv_k_a, rwkv_r_k, rwkv_lnx_w, rwkv_lnx_b, ret_gn_w, w_out, ln_ffn_g, w_gate,
           w_up, w_down):
    nb, t, _ = x_prompt.shape
    db, ts, _ = x_sample.shape
    depth = w_in.shape[0]
    n_pages = page_table.shape[1]
    past_len = n_pages * PAGE_SIZE
    n_pool = cache_fox_k.shape[1]

    half = HEAD_DIM // 2
    inv = ROPE_BASE ** (-jnp.arange(half, dtype=F32) / half)
    inv_lanes = jnp.tile(inv, LANES // half)[None]
    log_gamma = jnp.log1p(-jnp.exp2(-5.0 - jnp.arange(RET_H, dtype=F32)))
    lg = jnp.repeat(log_gamma, HEAD_DIM).reshape(RET_H // 2, 1, LANES)
    lgh = jnp.broadcast_to(log_gamma.reshape(RET_H // 2, 2, 1), (RET_H // 2, 2, LANES))
    cos_p, sin_p = _rope_tables(inv_lanes, t, 0)
    cos_s, sin_s = _rope_cols(jnp.broadcast_to(inv[:, None], (half, LANES)), ts, past_len)
    lg_rows = jnp.broadcast_to(log_gamma[:, None, None], (RET_H, SUBLANES, LANES))

    ckt = jnp.transpose(cache_fox_k, (0, 1, 3, 4, 2))
    cvt = jnp.transpose(cache_fox_v, (0, 1, 3, 4, 2))
    clt = jnp.swapaxes(cache_fox_logf, 2, 3)

    pad8 = lambda z: jnp.pad(z.reshape(db, ts, -1), ((0, 0), (0, SUBLANES - ts), (0, 0)))

    yp = x_prompt.reshape(nb * t, D_MODEL)
    ys = x_sample.reshape(db * ts, D_MODEL)
    outs = [[] for _ in range(12)]
    for l in range(depth):
        lw = _layer_weights(l, ln_mix_g, w_in, fox_qn_g, fox_kn_g, fox_f_b, rwkv_mu, rwkv_w0, rwkv_w2, rwkv_a0,
                            rwkv_a2, rwkv_g2, rwkv_k_k, rwkv_k_a, rwkv_r_k, rwkv_lnx_w, rwkv_lnx_b, ret_gn_w,
                            w_out, ln_ffn_g, w_gate, w_up, w_down)
        fq, fk, fv, fl, p_rwkv, p_ret = _inproj(yp, lw['ln_mix_g'], lw['w_in'], ROWS_PROMPT)
        kt, vt, lft, qx, kx, vb = _foxprep_prompt(fq, fk, fl, fv, lw['qg'], lw['kg'], lw['fb'], nb, ROWS_FOXPREP)
        fo = _foxattn(qx, kx, vb, nb, ATTN_BLOCK)
        p3 = p_rwkv.reshape(nb, t, RWKV_PROJ)
        eo, r_t = _retention(p_ret.reshape(nb, t, RET_PROJ), cos_p, sin_p, lg, lgh, lw['gn'],
                             jnp.zeros((2 * nb, LANES, LANES), F32), RET_SEQS)
        yp, s_t = _mix_and_ffn(
            yp, fo, p_rwkv, jnp.zeros((nb, RWKV_PROJ), F32), jnp.zeros((2 * nb, HEAD_DIM, LANES), F32),
            eo.reshape(nb * t, RET_W), lw, nb, t, ROWS_PROMPT, nb, SCAN_STEPS)
        outs[0].append(kt); outs[1].append(vt); outs[2].append(lft)
        outs[3].append(_rwkv_state_out(s_t, nb)); outs[4].append(p3[:, -1]); outs[5].append(_ret_state_out(r_t, nb))
        fq, fk, fv, fl, p_rwkv, p_ret = _inproj(ys, lw['ln_mix_g'], lw['w_in'], ROWS_SAMPLE)
        qn, kn, lf = _foxprep_sample(fq, fk, fl, lw['qg'], lw['kg'], lw['fb'], ROWS_SAMPLE)
        fo = _foxsample(page_table, l, pad8(qn), pad8(kn), pad8(fv), pad8(lf), ckt, cvt, clt, ts)
        p3 = p_rwkv.reshape(db, ts, RWKV_PROJ)
        xt = jnp.transpose(p_ret.reshape(db, ts, 4, RET_H, HEAD_DIM), (1, 2, 3, 4, 0))
        gn_cols = jnp.broadcast_to(lw['gn'].reshape(RET_H, HEAD_DIM, 1), (RET_H, HEAD_DIM, LANES))
        eo, r_t = _retention_short(xt, cos_s, sin_s, lg_rows, gn_cols, jnp.transpose(state_ret[l], (1, 2, 3, 0)))
        eo = jnp.transpose(eo, (3, 0, 1, 2)).reshape(db * ts, RET_W)
        ys, s_t = _mix_and_ffn(
            ys, fo.reshape(db * ts, FOX_W), p_rwkv, state_rwkv_shift[l], jnp.transpose(state_rwkv[l], (1, 2, 3, 0)),
            eo, lw, db, ts, ROWS_SAMPLE, None, None)
        outs[6].append(kn); outs[7].append(fv); outs[8].append(lf[:, :FOX_H])
        outs[9].append(jnp.transpose(s_t, (3, 0, 1, 2))); outs[10].append(p3[:, -1])
        outs[11].append(jnp.transpose(r_t, (3, 0, 1, 2)))

    n_pp = nb * t // PAGE_SIZE
    st = lambda i: jnp.stack(outs[i])
    page_rows = lambda z: jnp.transpose(z.reshape(depth, n_pp, FOX_H, HEAD_DIM, PAGE_SIZE), (0, 1, 4, 2, 3))
    return (yp.reshape(nb, t, D_MODEL), ys.reshape(db, ts, D_MODEL),
            page_rows(st(0)), page_rows(st(1)), jnp.swapaxes(st(2), 2, 3),
            st(3), st(4), st(5),
            st(6).reshape(depth, db, ts, FOX_H, HEAD_DIM),
            st(7).reshape(depth, db, ts, FOX_H, HEAD_DIM),
            st(8).reshape(depth, db, ts, FOX_H),
            st(9), st(10), st(11))
```

```python
import functools

import jax
import jax.numpy as jnp
import numpy as np
from jax import lax
from jax.experimental import pallas as pl
from jax.experimental.pallas import tpu as pltpu

F32 = jnp.float32
BF16 = jnp.bfloat16

LANES = 128
SUBLANES = 8
VMEM_LIMIT = 56 * 1024 * 1024

D_MODEL = 1024
HEAD_DIM = 64
FOX_H = 8
RWKV_H = 4
RET_H = 4
FOX_W = FOX_H * HEAD_DIM
RWKV_W = RWKV_H * HEAD_DIM
RET_W = RET_H * HEAD_DIM
RWKV_LORA_W = 32
RWKV_LORA_A = 32
RWKV_LORA_G = 64
RWKV_PROJ = 3 * RWKV_W + RWKV_LORA_W + RWKV_LORA_A + RWKV_LORA_G
RET_PROJ = 4 * RET_W
D_FF = 2816
PAGE_SIZE = 128
RET_CHUNK = 128
ROPE_BASE = 10000.0
NORM_EPS = 1e-6
RWKV_GN_EPS = 64e-5
RET_GN_EPS = 1e-5
NEG_BIG = -1e30
HEADS_PER_TILE = LANES // HEAD_DIM

ROWS_PROMPT = 512
ROWS_SAMPLE = 256
ROWS_FOXPREP = 256
ATTN_BLOCK = 512
SCAN_STEPS = 128
SCAN_UNROLL = 32
RET_SEQS = 4


def _cparams(sem):
    return pltpu.CompilerParams(dimension_semantics=sem, vmem_limit_bytes=VMEM_LIMIT)


def _resident(shape):
    nd = len(shape)
    return pl.BlockSpec(shape, lambda *_: (0,) * nd, pipeline_mode=pl.Buffered(1))


def _bdot(a, b):
    return jnp.dot(a.astype(BF16), b.astype(BF16), preferred_element_type=F32)


def _bdot_nt(a, b):
    return lax.dot_general(a.astype(BF16), b.astype(BF16), (((1,), (1,)), ((), ())),
                           preferred_element_type=F32)


def _split2(x):
    hi = x.astype(BF16)
    lo = (x - hi.astype(F32)).astype(BF16)
    return hi, lo


def _split3(x):
    hi = x.astype(BF16)
    r = x - hi.astype(F32)
    mid = r.astype(BF16)
    lo = (r - mid.astype(F32)).astype(BF16)
    return hi, mid, lo


def _iota(shape, axis):
    return lax.broadcasted_iota(jnp.int32, shape, axis)


def _seg_ones():
    return (_iota((LANES, LANES), 0) // HEAD_DIM == _iota((LANES, LANES), 1) // HEAD_DIM).astype(BF16)


def _seg_sum(x, e):
    n, nblk = x.shape[0], x.shape[-1] // LANES
    stacked = x if nblk == 1 else jnp.concatenate([x[:, c * LANES:(c + 1) * LANES] for c in range(nblk)], axis=0)
    hi, lo = _split2(stacked)
    r = jnp.dot(jnp.concatenate([hi, lo], axis=0), e, preferred_element_type=F32)
    r = r[:n * nblk] + r[n * nblk:]
    return r if nblk == 1 else jnp.concatenate([r[c * n:(c + 1) * n] for c in range(nblk)], axis=-1)


def _exact_dot(x, m01, left):
    parts = _split3(x)
    n, w = x.shape
    if left:
        r = jnp.dot(m01, jnp.concatenate(parts, axis=1), preferred_element_type=F32)
        return r[:, :w] + r[:, w:2 * w] + r[:, 2 * w:]
    r = jnp.dot(jnp.concatenate(parts, axis=0), m01, preferred_element_type=F32)
    return r[:n] + r[n:2 * n] + r[2 * n:]


def _sigmoid(x):
    return 1.0 / (1.0 + jnp.exp(-x))


def _softplus(x):
    return jnp.maximum(x, 0.0) + jnp.log1p(jnp.exp(-jnp.abs(x)))


def _inproj_kernel(x_ref, g_ref, wq_ref, wk_ref, wv_ref, wl_ref, wr_ref, we_ref,
                   oq_ref, ok_ref, ov_ref, ol_ref, or_ref, oe_ref):
    x = x_ref[...]
    h = x * lax.rsqrt(jnp.mean(x * x, -1, keepdims=True) + NORM_EPS) * g_ref[...]
    hb = h.astype(BF16)
    for w_ref, o_ref in ((wq_ref, oq_ref), (wk_ref, ok_ref), (wv_ref, ov_ref), (wl_ref, ol_ref),
                         (wr_ref, or_ref), (we_ref, oe_ref)):
        o_ref[...] = jnp.dot(hb, w_ref[...], preferred_element_type=F32)


def _inproj(x, g, ws, tm):
    n = x.shape[0]
    widths = [w.shape[1] for w in ws]
    row = lambda wd: pl.BlockSpec((tm, wd), lambda i: (i, 0))
    return pl.pallas_call(
        _inproj_kernel,
        grid=(n // tm,),
        in_specs=[row(D_MODEL), _resident((1, D_MODEL))] + [_resident(w.shape) for w in ws],
        out_specs=[row(wd) for wd in widths],
        out_shape=[jax.ShapeDtypeStruct((n, wd), F32) for wd in widths],
        compiler_params=_cparams(("parallel",)),
        name="inproj",
    )(x, g, *ws)


def _fox_norms(q_ref, k_ref, fl_ref, qg_ref, kg_ref, fb_ref):
    e = _seg_ones()

    def hnorm(x, g):
        ms = _seg_sum(x * x, e) * (1.0 / HEAD_DIM)
        return x * lax.rsqrt(ms + NORM_EPS) * g

    qn = hnorm(q_ref[...], qg_ref[...]) * (HEAD_DIM ** -0.5)
    kn = hnorm(k_ref[...], kg_ref[...])
    z = fl_ref[...] + fb_ref[...]
    lf = jnp.minimum(z, 0.0) - jnp.log1p(jnp.exp(-jnp.abs(z)))
    return qn, kn, lf


def _foxprep_sample_kernel(q_ref, k_ref, fl_ref, qg_ref, kg_ref, fb_ref, qn_ref, kn_ref, lf_ref):
    qn_ref[...], kn_ref[...], lf_ref[...] = _fox_norms(q_ref, k_ref, fl_ref, qg_ref, kg_ref, fb_ref)


def _bias_placement():
    m = np.zeros((LANES, 2 * FOX_H * LANES), np.float32)
    one = 3 * FOX_H
    for h in range(FOX_H):
        spare = h * LANES + HEAD_DIM * (1 - h % HEADS_PER_TILE)
        kspare = FOX_H * LANES + spare
        for term in range(3):
            m[term * FOX_H + h, spare + term] = 1.0
            m[one, spare + 3 + term] = 1.0
            m[one, kspare + term] = 1.0
            m[term * FOX_H + h, kspare + 3 + term] = -1.0
    return jnp.asarray(m, BF16)


def _foxprep_prompt_kernel(q_ref, k_ref, fl_ref, qg_ref, kg_ref, fb_ref, v_ref, place_ref,
                           kt_ref, vt_ref, lft_ref, qx_ref, kx_ref, vb_ref, carry_ref, *, tm):
    qn, kn, lf = _fox_norms(q_ref, k_ref, fl_ref, qg_ref, kg_ref, fb_ref)
    v = v_ref[...]
    vb_ref[...] = v.astype(BF16)
    for pg in range(tm // PAGE_SIZE):
        rows = slice(pg * PAGE_SIZE, (pg + 1) * PAGE_SIZE)
        kt_ref[pg] = kn[rows, :].T
        vt_ref[pg] = v[rows, :].T
        lft_ref[pg] = lf[rows, :].T[:FOX_H, :]

    @pl.when(pl.program_id(1) == 0)
    def _():
        carry_ref[...] = jnp.zeros_like(carry_ref)

    tri = (_iota((tm, tm), 0) >= _iota((tm, tm), 1)).astype(BF16)
    c = _exact_dot(lf, tri, left=True) + carry_ref[...]
    carry_ref[...] = c[tm - 1:tm, :]

    lane = _iota((tm, LANES), 1)
    c_hi, c_mid, c_lo = (p.astype(F32) for p in _split3(c))
    packed = jnp.where(lane < FOX_H, c_hi,
                       jnp.where(lane < 2 * FOX_H, pltpu.roll(c_mid, FOX_H, 1),
                                 jnp.where(lane < 3 * FOX_H, pltpu.roll(c_lo, 2 * FOX_H, 1),
                                           jnp.where(lane == 3 * FOX_H, 1.0, 0.0))))
    ext = jnp.dot(packed.astype(BF16), place_ref[...], preferred_element_type=F32)
    for h in range(FOX_H):
        hp, hh = divmod(h, HEADS_PER_TILE)
        own = lane // HEAD_DIM == hh
        pair = slice(hp * LANES, (hp + 1) * LANES)
        tile = slice(h * LANES, (h + 1) * LANES)
        ktile = slice((FOX_H + h) * LANES, (FOX_H + h + 1) * LANES)
        qx_ref[:, tile] = jnp.where(own, qn[:, pair], ext[:, tile]).astype(BF16)
        kx_ref[:, tile] = jnp.where(own, kn[:, pair], ext[:, ktile]).astype(BF16)


def _foxprep_sample(fq, fk, fl, qg, kg, fb, tm):
    n = fq.shape[0]
    row = lambda wd: pl.BlockSpec((tm, wd), lambda i: (i, 0))
    return pl.pallas_call(
        _foxprep_sample_kernel,
        grid=(n // tm,),
        in_specs=[row(FOX_W), row(FOX_W), row(LANES), _resident((1, FOX_W)), _resident((1, FOX_W)),
                  _resident((1, LANES))],
        out_specs=[row(FOX_W), row(FOX_W), row(LANES)],
        out_shape=[jax.ShapeDtypeStruct((n, FOX_W), F32), jax.ShapeDtypeStruct((n, FOX_W), F32),
                   jax.ShapeDtypeStruct((n, LANES), F32)],
        compiler_params=_cparams(("parallel",)),
        name="foxprep_sample",
    )(fq, fk, fl, qg, kg, fb)


def _foxprep_prompt(fq, fk, fl, fv, qg, kg, fb, nseq, tm):
    n = fq.shape[0]
    nt = n // nseq // tm
    ppt = tm // PAGE_SIZE
    n_pp = n // PAGE_SIZE
    place = _bias_placement()
    row = lambda wd: pl.BlockSpec((tm, wd), lambda b, j: (b * nt + j, 0))
    pages = lambda r: pl.BlockSpec((ppt, r, PAGE_SIZE), lambda b, j: (b * nt + j, 0, 0))
    return pl.pallas_call(
        functools.partial(_foxprep_prompt_kernel, tm=tm),
        grid=(nseq, nt),
        in_specs=[row(FOX_W), row(FOX_W), row(LANES), _resident((1, FOX_W)), _resident((1, FOX_W)),
                  _resident((1, LANES)), row(FOX_W), _resident(place.shape)],
        out_specs=[pages(FOX_W), pages(FOX_W), pages(FOX_H), row(FOX_H * LANES), row(FOX_H * LANES), row(FOX_W)],
        out_shape=[jax.ShapeDtypeStruct((n_pp, FOX_W, PAGE_SIZE), F32), jax.ShapeDtypeStruct((n_pp, FOX_W, PAGE_SIZE), F32),
                   jax.ShapeDtypeStruct((n_pp, FOX_H, PAGE_SIZE), F32),
                   jax.ShapeDtypeStruct((n, FOX_H * LANES), BF16), jax.ShapeDtypeStruct((n, FOX_H * LANES), BF16),
                   jax.ShapeDtypeStruct((n, FOX_W), BF16)],
        scratch_shapes=[pltpu.VMEM((1, LANES), F32)],
        compiler_params=_cparams(("parallel", "arbitrary")),
        name="foxprep_prompt",
    )(fq, fk, fl, qg, kg, fb, fv, place)


def _foxattn_kernel(qx_ref, kx_ref, vb_ref, o_ref, *, tq):
    qi = pl.program_id(2)
    q = [qx_ref[:, hh * LANES:(hh + 1) * LANES] for hh in range(HEADS_PER_TILE)]

    def block(off, carry, diagonal):
        vb = vb_ref[pl.ds(off, tq), :]
        stats, ps = [], []
        for hh in range(HEADS_PER_TILE):
            m, l, _ = carry[hh]
            kb = kx_ref[pl.ds(off, tq), hh * LANES:(hh + 1) * LANES]
            s = lax.dot_general(q[hh], kb, (((1,), (1,)), ((), ())), preferred_element_type=F32)
            if diagonal:
                s = jnp.where(_iota((tq, tq), 1) <= _iota((tq, tq), 0), s, NEG_BIG)
            m_new = jnp.maximum(m, jnp.max(s, axis=-1, keepdims=True))
            alpha = jnp.exp(m - m_new)
            p = jnp.exp(s - m_new)
            stats.append((m_new, alpha, alpha * l + jnp.sum(p, axis=-1, keepdims=True)))
            ps.append(p.astype(BF16))
        pv = jnp.dot(jnp.concatenate(ps, axis=0), vb, preferred_element_type=F32)
        return tuple((m_new, l, alpha * carry[hh][2] + pv[hh * tq:(hh + 1) * tq])
                     for hh, (m_new, alpha, l) in enumerate(stats))

    init = tuple((jnp.full((tq, 1), NEG_BIG, F32), jnp.zeros((tq, 1), F32), jnp.zeros((tq, LANES), F32))
                 for _ in range(HEADS_PER_TILE))
    carry = lax.fori_loop(0, qi, lambda j, c: block(pl.multiple_of(j * tq, tq), c, False), init)
    carry = block(pl.multiple_of(qi * tq, tq), carry, True)
    outs = [acc / l for _, l, acc in carry]
    o_ref[...] = jnp.where(_iota((tq, LANES), 1) // HEAD_DIM == 0, outs[0], outs[1]).astype(BF16)


def _foxattn(qx, kx, vb, nseq, tq):
    n = qx.shape[0]
    t = n // nseq
    nq = t // tq
    npair = FOX_H // HEADS_PER_TILE
    pair_w = HEADS_PER_TILE * LANES
    return pl.pallas_call(
        functools.partial(_foxattn_kernel, tq=tq),
        grid=(nseq, npair, nq),
        in_specs=[
            pl.BlockSpec((tq, pair_w), lambda b, h, i: (b * nq + i, h)),
            pl.BlockSpec((t, pair_w), lambda b, h, i: (b, h)),
            pl.BlockSpec((t, LANES), lambda b, h, i: (b, h)),
        ],
        out_specs=pl.BlockSpec((tq, LANES), lambda b, h, i: (b * nq + i, h)),
        out_shape=jax.ShapeDtypeStruct((n, FOX_W), BF16),
        compiler_params=_cparams(("parallel", "parallel", "arbitrary")),
        name="foxattn",
    )(qx, kx, vb)


def _foxsample_kernel(pt_ref, q_ref, kn_ref, vn_ref, lfn_ref, *rest, n_pages, t_new):
    del pt_ref
    kp = rest[:n_pages]
    vp = rest[n_pages:2 * n_pages]
    lp = rest[2 * n_pages:3 * n_pages]
    o_ref = rest[3 * n_pages]
    nrow = t_new * FOX_H

    rep = lambda x: jnp.concatenate([x] * t_new, axis=0)
    q = q_ref[...]
    hmask = _iota((FOX_H, FOX_W), 1) // HEAD_DIM == _iota((FOX_H, FOX_W), 0)
    qbd = jnp.concatenate(
        [jnp.where(hmask, jnp.broadcast_to(q[t:t + 1, :], (FOX_H, FOX_W)), 0.0) for t in range(t_new)],
        axis=0).astype(BF16)

    lfn = lfn_ref[...]
    diag8 = _iota((FOX_H, LANES), 1) == _iota((FOX_H, LANES), 0)
    cn_cols = []
    run = jnp.zeros((1, LANES), F32)
    for t in range(t_new):
        run = run + lfn[t:t + 1, :]
        cn_cols.append(jnp.sum(jnp.where(diag8, jnp.broadcast_to(run, (FOX_H, LANES)), 0.0), axis=-1, keepdims=True))
    cn = jnp.concatenate(cn_cols, axis=0)

    zeros_tail = jnp.zeros((PAGE_SIZE - SUBLANES, FOX_W), F32)
    key = _iota((nrow, PAGE_SIZE), 1)
    trow = _iota((nrow, PAGE_SIZE), 0) // FOX_H
    ckey = jnp.zeros((nrow, PAGE_SIZE), F32)
    for j in range(t_new):
        ckey = jnp.where(key == j, rep(cn_cols[j]), ckey)
    s_new = _bdot_nt(qbd, jnp.concatenate([kn_ref[...], zeros_tail], axis=0)) + (cn - ckey)
    s_new = jnp.where(key <= trow, s_new, NEG_BIG)

    lf_all = jnp.concatenate([lp[i][...] for i in range(n_pages)], axis=0)
    upper = (_iota((PAGE_SIZE, PAGE_SIZE), 0) <= _iota((PAGE_SIZE, PAGE_SIZE), 1)).astype(BF16)
    cp_all = _exact_dot(lf_all, upper, left=False)
    before = [jnp.zeros((FOX_H, 1), F32)]
    for i in range(n_pages):
        before.append(before[-1] + cp_all[i * FOX_H:(i + 1) * FOX_H, PAGE_SIZE - 1:PAGE_SIZE])
    cq_abs = rep(before[n_pages]) + cn
    s_past = []
    for i in range(n_pages):
        cp = rep(cp_all[i * FOX_H:(i + 1) * FOX_H, :] + before[i])
        s_past.append(_bdot(qbd, kp[i][...].reshape(FOX_W, PAGE_SIZE)) + (cq_abs - cp))

    m = jnp.max(s_new, axis=-1, keepdims=True)
    for s in s_past:
        m = jnp.maximum(m, jnp.max(s, axis=-1, keepdims=True))
    p = jnp.exp(s_new - m)
    l = jnp.sum(p, axis=-1, keepdims=True)
    o = _bdot(p, jnp.concatenate([vn_ref[...], zeros_tail], axis=0))
    for i in range(n_pages):
        p = jnp.exp(s_past[i] - m)
        l = l + jnp.sum(p, axis=-1, keepdims=True)
        o = o + _bdot_nt(p, vp[i][...].reshape(FOX_W, PAGE_SIZE))
    o = o / l
    omask = _iota((nrow, FOX_W), 1) // HEAD_DIM == _iota((nrow, FOX_W), 0) % FOX_H
    o_ref[...] = jnp.sum(jnp.where(omask, o, 0.0).reshape(t_new, FOX_H, FOX_W), axis=1)


def _foxsample(page_table, layer, qn, kn, vn, lfn, cache_kt, cache_vt, cache_lft, t_new):
    db, n_pages = page_table.shape
    pt = page_table.reshape(-1)
    new = lambda wd: pl.BlockSpec((None, SUBLANES, wd), lambda b, pt: (b, 0, 0))

    def page(shape, i):
        nd = len(shape)
        return pl.BlockSpec((None, None) + shape, lambda b, pt: (layer, pt[b * n_pages + i]) + (0,) * nd)

    in_specs = [new(FOX_W), new(FOX_W), new(FOX_W), new(LANES)]
    in_specs += [page((FOX_H, HEAD_DIM, PAGE_SIZE), i) for i in range(n_pages)]
    in_specs += [page((FOX_H, HEAD_DIM, PAGE_SIZE), i) for i in range(n_pages)]
    in_specs += [page((FOX_H, PAGE_SIZE), i) for i in range(n_pages)]
    return pl.pallas_call(
        functools.partial(_foxsample_kernel, n_pages=n_pages, t_new=t_new),
        grid_spec=pltpu.PrefetchScalarGridSpec(
            num_scalar_prefetch=1,
            grid=(db,),
            in_specs=in_specs,
            out_specs=pl.BlockSpec((None, t_new, FOX_W), lambda b, pt: (b, 0, 0)),
        ),
        out_shape=jax.ShapeDtypeStruct((db, t_new, FOX_W), F32),
        compiler_params=_cparams(("arbitrary",)),
        name="foxsample",
    )(pt, qn, kn, vn, lfn, *([cache_kt] * n_pages), *([cache_vt] * n_pages), *([cache_lft] * n_pages))


def _rwkvprep_kernel(p_ref, init_ref, mu_ref, w0_ref, w2_ref, a0_ref, a2_ref, g2_ref, kk_ref, ka_ref,
                     r_out, w_out, k_out, v_out, a_out, b_out, g_out, *carry, tm, period):
    e = _seg_ones()
    p = p_ref[...]
    rolled = pltpu.roll(p, 1, 0)
    row = _iota((tm, RWKV_PROJ), 0)
    if period >= tm:
        carry_ref, = carry

        @pl.when(pl.program_id(1) == 0)
        def _():
            carry_ref[...] = init_ref[...]

        pp = jnp.where(row == 0, carry_ref[...], rolled)
        carry_ref[...] = p[tm - 1:tm, :]
    else:
        pp = jnp.where(row % period == 0, init_ref[...], rolled)
    xs = p + (pp - p) * mu_ref[...]
    r = xs[:, 0:RWKV_W]
    k = xs[:, RWKV_W:2 * RWKV_W]
    v = xs[:, 2 * RWKV_W:3 * RWKV_W]
    lo = xs[:, 3 * RWKV_W:]
    w = -_softplus(-(w0_ref[...] + _bdot(jnp.tanh(lo), w2_ref[...]))) - 0.5
    a = _sigmoid(a0_ref[...] + _bdot(lo, a2_ref[...]))
    kk = k * kk_ref[...]
    nrm = jnp.sqrt(_seg_sum(kk * kk, e))
    kk = kk / jnp.maximum(nrm, 1e-12)
    r_out[...] = r
    w_out[...] = jnp.exp(-jnp.exp(w))
    k_out[...] = k * (1.0 + (a - 1.0) * ka_ref[...])
    v_out[...] = v
    a_out[...] = -kk
    b_out[...] = kk * a
    g_out[...] = _bdot(_sigmoid(lo), g2_ref[...])


def _rwkvprep(p, shift, period, mu, w0, w2p, a0, a2p, g2p, k_k, k_a, tm):
    n = p.shape[0]
    nseq = n // period
    vec = _resident((1, RWKV_W))
    lora = _resident((LANES, RWKV_W))
    if period >= tm:
        nt = period // tm
        grid = (nseq, nt)
        row = lambda wd: pl.BlockSpec((tm, wd), lambda b, j: (b * nt + j, 0))
        init = shift[:, None, :]
        init_spec = pl.BlockSpec((None, 1, RWKV_PROJ), lambda b, j: (b, 0, 0))
        scratch = [pltpu.VMEM((1, RWKV_PROJ), F32)]
        sem = ("parallel", "arbitrary")
    else:
        grid = (n // tm,)
        row = lambda wd: pl.BlockSpec((tm, wd), lambda i: (i, 0))
        init = jnp.repeat(shift, period, axis=0)
        init_spec = row(RWKV_PROJ)
        scratch = []
        sem = ("parallel",)
    return pl.pallas_call(
        functools.partial(_rwkvprep_kernel, tm=tm, period=period),
        grid=grid,
        in_specs=[row(RWKV_PROJ), init_spec, _resident((1, RWKV_PROJ)), vec, lora, vec, lora, lora, vec, vec],
        out_specs=[row(RWKV_W)] * 7,
        out_shape=[jax.ShapeDtypeStruct((n, RWKV_W), F32)] * 7,
        scratch_shapes=scratch,
        compiler_params=_cparams(sem),
        name="rwkvprep",
    )(p, init, mu, w0, w2p, a0, a2p, g2p, k_k, k_a)


def _rwkvscan_kernel(r_ref, w_ref, k_ref, v_ref, a_ref, b_ref, s0_ref, y_ref, st_ref, s_scr, *, bblk, tblk):
    tb = pl.program_id(1)
    ng = HEADS_PER_TILE * bblk

    @pl.when(tb == 0)
    def _():
        s_scr[...] = s0_ref[...]

    e = _seg_ones()
    diag = (_iota((HEAD_DIM, LANES), 1) % HEAD_DIM == _iota((HEAD_DIM, LANES), 0)).astype(BF16)[None]
    half = (_iota((SUBLANES, LANES), 1) // HEAD_DIM == _iota((SUBLANES, LANES), 0)).astype(BF16)
    first = _iota((1, LANES), 1) < HEAD_DIM

    sub = min(SCAN_UNROLL, tblk)

    def chunk(c, carry):
        off = pl.multiple_of(c * sub, sub)
        tiles = [[ref[b, pl.ds(off, sub), :] for b in range(bblk)] for ref in (r_ref, w_ref, k_ref, v_ref, a_ref, b_ref)]

        def rows(kind, i):
            return jnp.concatenate(
                [tiles[kind][b][i:i + 1, hp * LANES:(hp + 1) * LANES][None] for b in range(bblk) for hp in range(2)],
                axis=0)

        vdiag = jnp.concatenate([rows(3, i).astype(BF16) * diag for i in range(sub)], axis=0)
        vcol = jnp.dot(vdiag.reshape(sub * ng * HEAD_DIM, LANES), e,
                       preferred_element_type=F32).reshape(sub * ng, HEAD_DIM, LANES)

        s = s_scr[...]
        srs = []
        for i in range(sub):
            sa = jnp.dot((s * rows(4, i)).reshape(ng * HEAD_DIM, LANES).astype(BF16), e,
                         preferred_element_type=F32).reshape(ng, HEAD_DIM, LANES)
            s = s * rows(1, i) + sa * rows(5, i) + vcol[i * ng:(i + 1) * ng] * rows(2, i)
            srs.append((s * rows(0, i)).astype(BF16))
        s_scr[...] = s

        sr = jnp.concatenate(srs, axis=0).reshape(sub * ng * HEAD_DIM, LANES)
        out = lax.dot_general(half, sr, (((1,), (1,)), ((), ())), preferred_element_type=F32)
        for b in range(bblk):
            ytile = []
            for i in range(sub):
                lanes = slice((i * bblk + b) * 2 * HEAD_DIM, (i * bblk + b + 1) * 2 * HEAD_DIM)
                h0, h1 = out[0:1, lanes], out[1:2, lanes]
                ytile.append(jnp.concatenate(
                    [jnp.where(first, h0, pltpu.roll(h1, HEAD_DIM, 1)),
                     jnp.where(first, pltpu.roll(h0, HEAD_DIM, 1), h1)], axis=1))
            y_ref[b, pl.ds(off, sub), :] = jnp.concatenate(ytile, axis=0)
        return carry

    lax.fori_loop(0, tblk // sub, chunk, 0)

    @pl.when(tb == pl.num_programs(1) - 1)
    def _():
        st_ref[...] = s_scr[...]


def _rwkvscan(r, w, k, v, a, b, s0, bblk, tblk):
    nb, t, _ = r.shape
    ng = HEADS_PER_TILE * bblk
    seq = pl.BlockSpec((bblk, tblk, RWKV_W), lambda i, j: (i, j, 0))
    st = pl.BlockSpec((ng, HEAD_DIM, LANES), lambda i, j: (i, 0, 0))
    return pl.pallas_call(
        functools.partial(_rwkvscan_kernel, bblk=bblk, tblk=tblk),
        grid=(nb // bblk, t // tblk),
        in_specs=[seq] * 6 + [st],
        out_specs=[seq, st],
        out_shape=[jax.ShapeDtypeStruct((nb, t, RWKV_W), F32), jax.ShapeDtypeStruct(s0.shape, F32)],
        scratch_shapes=[pltpu.VMEM((ng, HEAD_DIM, LANES), F32)],
        compiler_params=_cparams(("parallel", "arbitrary")),
        name="rwkvscan",
    )(r, w, k, v, a, b, s0)


def _rwkv_short_kernel(z_ref, s0_ref, y_ref, st_ref):
    nt = z_ref.shape[0]
    R, W, K, V, A, B = range(6)

    def value_rows(i, carry):
        base = pl.multiple_of(i * SUBLANES, SUBLANES)
        vt = [z_ref[t, V, pl.ds(base, SUBLANES), :] for t in range(nt)]
        yrows = [[] for _ in range(nt)]
        for j in range(SUBLANES):
            s = s0_ref[base + j]
            for t in range(nt):
                sa = jnp.sum(s * z_ref[t, A], axis=0, keepdims=True)
                s = s * z_ref[t, W] + sa * z_ref[t, B] + vt[t][j:j + 1, :] * z_ref[t, K]
                yrows[t].append(jnp.sum(s * z_ref[t, R], axis=0, keepdims=True))
            st_ref[base + j] = s
        for t in range(nt):
            y_ref[t, pl.ds(base, SUBLANES), :] = jnp.concatenate(yrows[t], axis=0)
        return carry

    lax.fori_loop(0, HEAD_DIM // SUBLANES, value_rows, 0)


def _rwkv_short(zt, s0):
    nt, _, nh, _, nb = zt.shape
    assert nb == LANES and nt <= SUBLANES
    st = pl.BlockSpec((None, HEAD_DIM, HEAD_DIM, LANES), lambda h: (h, 0, 0, 0))
    return pl.pallas_call(
        _rwkv_short_kernel,
        grid=(nh,),
        in_specs=[pl.BlockSpec((nt, 6, None, HEAD_DIM, LANES), lambda h: (0, 0, h, 0, 0)), st],
        out_specs=[pl.BlockSpec((nt, None, HEAD_DIM, LANES), lambda h: (0, h, 0, 0)), st],
        out_shape=[jax.ShapeDtypeStruct((nt, nh, HEAD_DIM, LANES), F32), jax.ShapeDtypeStruct(s0.shape, F32)],
        compiler_params=_cparams(("parallel",)),
        name="rwkv_short",
    )(zt, s0)


def _rope_kernel(inv_ref, cos_ref, sin_ref, *, pos0):
    t = cos_ref.shape[0]
    pos = (pos0 + _iota((t, LANES), 0)).astype(F32)
    ang = pos * inv_ref[...]
    first_half = _iota((t, LANES), 1) % HEAD_DIM < HEAD_DIM // 2
    cos_ref[...] = jnp.cos(ang)
    sin_ref[...] = jnp.where(first_half, -jnp.sin(ang), jnp.sin(ang))


def _rope_tables(inv_lanes, t, pos0):
    return pl.pallas_call(
        functools.partial(_rope_kernel, pos0=pos0),
        out_shape=[jax.ShapeDtypeStruct((t, LANES), F32)] * 2,
        name="rope",
    )(inv_lanes)


def _ret_kernel(x_ref, cos_ref, sin_ref, lg_ref, lgh_ref, gn_ref, r0_ref, o_ref, rt_ref, r_scr,
                *, bs):
    lp = RET_CHUNK
    npair = RET_H // HEADS_PER_TILE
    c = pl.program_id(1)

    @pl.when(c == 0)
    def _():
        r_scr[...] = r0_ref[...]

    lane = _iota((lp, LANES), 1)
    first_half = lane % HEAD_DIM < HEAD_DIM // 2
    cos = cos_ref[...]
    sin = sin_ref[...]

    def rope(x):
        swapped = jnp.where(first_half, pltpu.roll(x, LANES - HEAD_DIM // 2, 1), pltpu.roll(x, HEAD_DIM // 2, 1))
        return x * cos + swapped * sin

    idx = _iota((lp, 1), 0).astype(F32)
    diff = (_iota((lp, lp), 0) - _iota((lp, lp), 1)).astype(F32)
    same_head = _iota((LANES, LANES), 0) // HEAD_DIM == _iota((LANES, LANES), 1) // HEAD_DIM
    e = _seg_ones()
    dmask, cross, kdec, cdec = [], [], [], []
    for hp in range(npair):
        lg = lg_ref[hp]
        dmask.append(jnp.concatenate(
            [jnp.where(diff >= 0.0, jnp.exp(lgh_ref[hp, hh:hh + 1, :] * jnp.maximum(diff, 0.0)), 0.0)
             for hh in range(HEADS_PER_TILE)], axis=0))
        cross.append(jnp.exp(lg * (idx + 1.0)))
        kdec.append(jnp.exp(lg * (lp - 1.0 - idx)))
        cdec.append(jnp.exp(lg * float(lp)))

    chains = [(s, hp) for s in range(bs) for hp in range(npair)]
    os_ = []
    for s, hp in chains:
        col = lambda j: x_ref[s, :, (j * npair + hp) * LANES:(j * npair + hp + 1) * LANES]
        q = rope(col(0))
        k = rope(col(1)) * (HEAD_DIM ** -0.5)
        vb = col(2).astype(BF16)
        qh = jnp.concatenate([jnp.where(lane // HEAD_DIM == hh, q, 0.0) for hh in range(HEADS_PER_TILE)], axis=0)
        sc = _bdot_nt(qh, k) * dmask[hp]
        ov = jnp.dot(sc.astype(BF16), vb, preferred_element_type=F32)
        o = jnp.where(lane // HEAD_DIM == 0, ov[:lp], ov[lp:])
        r = r_scr[s * npair + hp]
        os_.append(o + _bdot(q, r) * cross[hp])
        upd = jnp.dot((k * kdec[hp]).T.astype(BF16), vb, preferred_element_type=F32)
        r_scr[s * npair + hp] = r * cdec[hp] + jnp.where(same_head, upd, 0.0)

    o_all = jnp.concatenate(os_, axis=0)
    xc = o_all - _seg_sum(o_all, e) * (1.0 / HEAD_DIM)
    yn = xc * lax.rsqrt(_seg_sum(xc * xc, e) * (1.0 / HEAD_DIM) + RET_GN_EPS)
    for i, (s, hp) in enumerate(chains):
        g = x_ref[s, :, (3 * npair + hp) * LANES:(3 * npair + hp + 1) * LANES]
        o_ref[s, :, hp * LANES:(hp + 1) * LANES] = (
            yn[i * lp:(i + 1) * lp] * gn_ref[:, hp * LANES:(hp + 1) * LANES] * (g * _sigmoid(g))).astype(BF16)

    @pl.when(c == pl.num_programs(1) - 1)
    def _():
        rt_ref[...] = r_scr[...]


def _retention(proj, cos, sin, lg, lgh, gn, r0, bs):
    nb, t, _ = proj.shape
    l_in = RET_CHUNK
    nc = t // l_in
    npair = RET_H // HEADS_PER_TILE
    st = pl.BlockSpec((bs * npair, LANES, LANES), lambda b, c: (b, 0, 0))
    tab = pl.BlockSpec((l_in, LANES), lambda b, c: (c, 0))
    return pl.pallas_call(
        functools.partial(_ret_kernel, bs=bs),
        grid=(nb // bs, nc),
        in_specs=[pl.BlockSpec((bs, l_in, 4 * RET_W), lambda b, c: (b, c, 0)), tab, tab,
                  _resident(lg.shape), _resident(lgh.shape), _resident(gn.shape), st],
        out_specs=[pl.BlockSpec((bs, l_in, RET_W), lambda b, c: (b, c, 0)), st],
        out_shape=[jax.ShapeDtypeStruct((nb, t, RET_W), BF16), jax.ShapeDtypeStruct(r0.shape, F32)],
        scratch_shapes=[pltpu.VMEM((bs * npair, LANES, LANES), F32)],
        compiler_params=_cparams(("parallel", "arbitrary")),
        name="retention",
    )(proj, cos, sin, lg, lgh, gn, r0)


def _rope_cols_kernel(inv_ref, cos_ref, sin_ref, *, pos0):
    for t in range(cos_ref.shape[0]):
        ang = float(pos0 + t) * inv_ref[...]
        cos_ref[t] = jnp.cos(ang)
        sin_ref[t] = jnp.sin(ang)


def _rope_cols(inv_cols, t, pos0):
    return pl.pallas_call(
        functools.partial(_rope_cols_kernel, pos0=pos0),
        out_shape=[jax.ShapeDtypeStruct((t, HEAD_DIM // 2, LANES), F32)] * 2,
        name="rope_cols",
    )(inv_cols)


def _ret_short_kernel(x_ref, cos_ref, sin_ref, lg_ref, gn_ref, r0_ref, o_ref, rt_ref, q_scr, k_scr):
    nt = x_ref.shape[0]
    half = HEAD_DIM // 2
    lg = lg_ref[0:1, :]
    gam = lambda n: jnp.exp(lg * float(n))

    def rope(x, t):
        x1, x2 = x[:half], x[half:]
        c, s = cos_ref[t], sin_ref[t]
        return jnp.concatenate([x1 * c - x2 * s, x2 * c + x1 * s], axis=0)

    q = [rope(x_ref[t, 0], t) for t in range(nt)]
    k = [rope(x_ref[t, 1], t) * (HEAD_DIM ** -0.5) for t in range(nt)]
    v = [x_ref[t, 2] for t in range(nt)]

    intra = []
    for t in range(nt):
        acc = None
        for t2 in range(t + 1):
            term = (jnp.sum(q[t] * k[t2], axis=0, keepdims=True) * gam(t - t2)) * v[t2]
            acc = term if acc is None else acc + term
        intra.append(acc)

    for t in range(nt):
        q_scr[t] = q[t]
        k_scr[t] = k[t] * gam(nt - 1 - t)
    decay_all = gam(nt)

    def key_rows(i, acc):
        base = pl.multiple_of(i * SUBLANES, SUBLANES)
        qb = [q_scr[t, pl.ds(base, SUBLANES), :] for t in range(nt)]
        kb = [k_scr[t, pl.ds(base, SUBLANES), :] for t in range(nt)]
        acc = list(acc)
        for j in range(SUBLANES):
            r = r0_ref[base + j]
            upd = r * decay_all
            for t in range(nt):
                acc[t] = acc[t] + qb[t][j:j + 1, :] * r
                upd = upd + kb[t][j:j + 1, :] * v[t]
            rt_ref[base + j] = upd
        return tuple(acc)

    cross = lax.fori_loop(0, HEAD_DIM // SUBLANES, key_rows,
                          tuple(jnp.zeros((HEAD_DIM, LANES), F32) for _ in range(nt)))

    for t in range(nt):
        o = intra[t] + cross[t] * gam(t + 1)
        mean = jnp.mean(o, axis=0, keepdims=True)
        xc = o - mean
        var = jnp.mean(xc * xc, axis=0, keepdims=True)
        g = x_ref[t, 3]
        o_ref[t] = xc * lax.rsqrt(var + RET_GN_EPS) * gn_ref[...] * (g * _sigmoid(g))


def _retention_short(xt, cos, sin, lg_rows, gn_cols, r0):
    nt, _, nh, _, nb = xt.shape
    assert nb == LANES and nt <= SUBLANES
    tab = _resident(cos.shape)
    return pl.pallas_call(
        _ret_short_kernel,
        grid=(nh,),
        in_specs=[pl.BlockSpec((nt, 4, None, HEAD_DIM, LANES), lambda h: (0, 0, h, 0, 0)), tab, tab,
                  pl.BlockSpec((None, SUBLANES, LANES), lambda h: (h, 0, 0)),
                  pl.BlockSpec((None, HEAD_DIM, LANES), lambda h: (h, 0, 0)),
                  pl.BlockSpec((None, HEAD_DIM, HEAD_DIM, LANES), lambda h: (h, 0, 0, 0))],
        out_specs=[pl.BlockSpec((nt, None, HEAD_DIM, LANES), lambda h: (0, h, 0, 0)),
                   pl.BlockSpec((None, HEAD_DIM, HEAD_DIM, LANES), lambda h: (h, 0, 0, 0))],
        out_shape=[jax.ShapeDtypeStruct((nt, nh, HEAD_DIM, LANES), F32), jax.ShapeDtypeStruct(r0.shape, F32)],
        scratch_shapes=[pltpu.VMEM((nt, HEAD_DIM, LANES), F32), pltpu.VMEM((nt, HEAD_DIM, LANES), F32)],
        compiler_params=_cparams(("parallel",)),
        name="retention_short",
    )(xt, cos, sin, lg_rows, gn_cols, r0)


def _outproj_kernel(x_ref, fo_ref, y_ref, r_ref, k_ref, v_ref, g_ref, eo_ref, lw_ref, lb_ref, rk_ref,
                    wf_ref, wr_ref, we_ref, o_ref):
    e = _seg_ones()
    y = y_ref[...]
    mean = _seg_sum(y, e) * (1.0 / HEAD_DIM)
    yc = y - mean
    var = _seg_sum(yc * yc, e) * (1.0 / HEAD_DIM)
    yn = yc * lax.rsqrt(var + RWKV_GN_EPS) * lw_ref[...] + lb_ref[...]
    v = v_ref[...]
    bonus = _seg_sum(r_ref[...] * k_ref[...] * rk_ref[...], e) * v
    ro = (yn + bonus) * g_ref[...]
    o_ref[...] = (x_ref[...] + _bdot(fo_ref[...], wf_ref[...]) + _bdot(ro, wr_ref[...])
                  + _bdot(eo_ref[...], we_ref[...]))


def _outproj(x, fo, y, r, k, v, g, eo, lw, lb, rk, wf, wr, we, tm):
    n = x.shape[0]
    row = lambda wd: pl.BlockSpec((tm, wd), lambda i: (i, 0))
    vec = _resident((1, RWKV_W))
    return pl.pallas_call(
        _outproj_kernel,
        grid=(n // tm,),
        in_specs=[row(D_MODEL), row(FOX_W)] + [row(RWKV_W)] * 6 + [vec, vec, vec,
                  _resident(wf.shape), _resident(wr.shape), _resident(we.shape)],
        out_specs=row(D_MODEL),
        out_shape=jax.ShapeDtypeStruct((n, D_MODEL), F32),
        compiler_params=_cparams(("parallel",)),
        name="outproj",
    )(x, fo, y, r, k, v, g, eo, lw, lb, rk, wf, wr, we)


def _ffn_kernel(x_ref, g_ref, wg_ref, wu_ref, wd_ref, o_ref):
    x = x_ref[...]
    h = (x * lax.rsqrt(jnp.mean(x * x, -1, keepdims=True) + NORM_EPS) * g_ref[...]).astype(BF16)
    gate = jnp.dot(h, wg_ref[...], preferred_element_type=F32)
    up = jnp.dot(h, wu_ref[...], preferred_element_type=F32)
    act = gate * _sigmoid(gate) * up
    o_ref[...] = x + _bdot(act, wd_ref[...])


def _ffn(x, g, wg, wu, wd, tm):
    n = x.shape[0]
    row = pl.BlockSpec((tm, D_MODEL), lambda i: (i, 0))
    return pl.pallas_call(
        _ffn_kernel,
        grid=(n // tm,),
        in_specs=[row, _resident((1, D_MODEL)), _resident(wg.shape), _resident(wu.shape), _resident(wd.shape)],
        out_specs=row,
        out_shape=jax.ShapeDtypeStruct((n, D_MODEL), F32),
        compiler_params=_cparams(("parallel",)),
        name="ffn",
    )(x, g, wg, wu, wd)


def _rwkv_state_out(s, nb):
    s = s.reshape(nb, RWKV_H // 2, HEAD_DIM, 2, HEAD_DIM).transpose(0, 1, 3, 2, 4)
    return s.reshape(nb, RWKV_H, HEAD_DIM, HEAD_DIM)


def _ret_state_out(r, nb):
    r = r.reshape(nb, RET_H // 2, LANES, LANES)
    return jnp.stack([r[:, :, :HEAD_DIM, :HEAD_DIM], r[:, :, HEAD_DIM:, HEAD_DIM:]], axis=2).reshape(
        nb, RET_H, HEAD_DIM, HEAD_DIM)


def _layer_weights(l, ln_mix_g, w_in, fox_qn_g, fox_kn_g, fox_f_b, rwkv_mu, rwkv_w0, rwkv_w2, rwkv_a0, rwkv_a2,
                   rwkv_g2, rwkv_k_k, rwkv_k_a, rwkv_r_k, rwkv_lnx_w, rwkv_lnx_b, ret_gn_w, w_out, ln_ffn_g,
                   w_gate, w_up, w_down):
    wi = w_in[l].astype(BF16)
    o_f = 3 * FOX_W
    o_r = o_f + FOX_H
    o_e = o_r + RWKV_PROJ
    pad_rows = lambda w, off: jnp.zeros((LANES, RWKV_W), BF16).at[off:off + w.shape[0]].set(w.astype(BF16))
    wo = w_out[l].astype(BF16)
    return dict(
        ln_mix_g=ln_mix_g[l][None],
        w_in=[wi[:, 0:FOX_W], wi[:, FOX_W:2 * FOX_W], wi[:, 2 * FOX_W:o_f],
              jnp.pad(wi[:, o_f:o_r], ((0, 0), (0, LANES - FOX_H))), wi[:, o_r:o_e], wi[:, o_e:]],
        qg=jnp.tile(fox_qn_g[l], FOX_H)[None], kg=jnp.tile(fox_kn_g[l], FOX_H)[None],
        fb=jnp.pad(fox_f_b[l], (0, LANES - FOX_H))[None],
        mu=rwkv_mu[l][None], w0=rwkv_w0[l][None], a0=rwkv_a0[l][None],
        w2=pad_rows(rwkv_w2[l], 0), a2=pad_rows(rwkv_a2[l], RWKV_LORA_W),
        g2=pad_rows(rwkv_g2[l], RWKV_LORA_W + RWKV_LORA_A),
        k_k=rwkv_k_k[l][None], k_a=rwkv_k_a[l][None], r_k=rwkv_r_k[l].reshape(1, RWKV_W),
        lnx_w=rwkv_lnx_w[l][None], lnx_b=rwkv_lnx_b[l][None], gn=ret_gn_w[l][None],
        wo_f=wo[:FOX_W], wo_r=wo[FOX_W:FOX_W + RWKV_W], wo_e=wo[FOX_W + RWKV_W:],
        ln_ffn_g=ln_ffn_g[l][None],
        w_gate=w_gate[l].astype(BF16), w_up=w_up[l].astype(BF16), w_down=w_down[l].astype(BF16),
    )


def _mix_and_ffn(x2, fo, p_rwkv, shift, s0, eo, lw, nb, t, tm, bblk, tblk):
    r, w, k, v, a, b, g = _rwkvprep(p_rwkv, shift, t, lw['mu'], lw['w0'], lw['w2'], lw['a0'], lw['a2'], lw['g2'],
                                    lw['k_k'], lw['k_a'], tm)
    if s0.ndim == 4:
        zt = jnp.stack([r, w, k, v, a, b], axis=1).reshape(nb, t, 6, RWKV_H, HEAD_DIM)
        y, s_t = _rwkv_short(jnp.transpose(zt, (1, 2, 3, 4, 0)), s0)
        y = jnp.transpose(y, (3, 0, 1, 2))
    else:
        seq = lambda z: z.reshape(nb, t, RWKV_W)
        y, s_t = _rwkvscan(seq(r), seq(w), seq(k), seq(v), seq(a), seq(b), s0, bblk, tblk)
    x2 = _outproj(x2, fo, y.reshape(nb * t, RWKV_W), r, k, v, g, eo, lw['lnx_w'], lw['lnx_b'], lw['r_k'],
                  lw['wo_f'], lw['wo_r'], lw['wo_e'], tm)
    x2 = _ffn(x2, lw['ln_ffn_g'], lw['w_gate'], lw['w_up'], lw['w_down'], tm)
    return x2, s_t


def kernel(x_prompt, x_sample, cache_fox_k, cache_fox_v, cache_fox_logf, state_rwkv, state_rwkv_shift, state_ret,
           page_table, ln_mix_g, w_in, fox_qn_g, fox_kn_g, fox_f_b, rwkv_mu, rwkv_w0, rwkv_w2, rwkv_a0, rwkv_a2,
           rwkv_g2, rwkv_k_k, rwkv_k_a, rwkv_r_k, rwkv_lnx_w, rwkv_lnx_b, ret_gn_w, w_out, ln_ffn_g, w_gate,
           w_up, w_down):
    nb, t, _ = x_prompt.shape
    db, ts, _ = x_sample.shape
    depth = w_in.shape[0]
    n_pages = page_table.shape[1]
    past_len = n_pages * PAGE_SIZE
    n_pool = cache_fox_k.shape[1]

    half = HEAD_DIM // 2
    inv = ROPE_BASE ** (-jnp.arange(half, dtype=F32) / half)
    inv_lanes = jnp.tile(inv, LANES // half)[None]
    log_gamma = jnp.log1p(-jnp.exp2(-5.0 - jnp.arange(RET_H, dtype=F32)))
    lg = jnp.repeat(log_gamma, HEAD_DIM).reshape(RET_H // 2, 1, LANES)
    lgh = jnp.broadcast_to(log_gamma.reshape(RET_H // 2, 2, 1), (RET_H // 2, 2, LANES))
    cos_p, sin_p = _rope_tables(inv_lanes, t, 0)
    cos_s, sin_s = _rope_cols(jnp.broadcast_to(inv[:, None], (half, LANES)), ts, past_len)
    lg_rows = jnp.broadcast_to(log_gamma[:, None, None], (RET_H, SUBLANES, LANES))

    ckt = jnp.transpose(cache_fox_k, (0, 1, 3, 4, 2))
    cvt = jnp.transpose(cache_fox_v, (0, 1, 3, 4, 2))
    clt = jnp.swapaxes(cache_fox_logf, 2, 3)

    pad8 = lambda z: jnp.pad(z.reshape(db, ts, -1), ((0, 0), (0, SUBLANES - ts), (0, 0)))

    yp = x_prompt.reshape(nb * t, D_MODEL)
    ys = x_sample.reshape(db * ts, D_MODEL)
    outs = [[] for _ in range(12)]
    for l in range(depth):
        lw = _layer_weights(l, ln_mix_g, w_in, fox_qn_g, fox_kn_g, fox_f_b, rwkv_mu, rwkv_w0, rwkv_w2, rwkv_a0,
                            rwkv_a2, rwkv_g2, rwkv_k_k, rwkv_k_a, rwkv_r_k, rwkv_lnx_w, rwkv_lnx_b, ret_gn_w,
                            w_out, ln_ffn_g, w_gate, w_up, w_down)
        fq, fk, fv, fl, p_rwkv, p_ret = _inproj(yp, lw['ln_mix_g'], lw['w_in'], ROWS_PROMPT)
        kt, vt, lft, qx, kx, vb = _foxprep_prompt(fq, fk, fl, fv, lw['qg'], lw['kg'], lw['fb'], nb, ROWS_FOXPREP)
        fo = _foxattn(qx, kx, vb, nb, ATTN_BLOCK)
        p3 = p_rwkv.reshape(nb, t, RWKV_PROJ)
        eo, r_t = _retention(p_ret.reshape(nb, t, RET_PROJ), cos_p, sin_p, lg, lgh, lw['gn'],
                             jnp.zeros((2 * nb, LANES, LANES), F32), RET_SEQS)
        yp, s_t = _mix_and_ffn(
            yp, fo, p_rwkv, jnp.zeros((nb, RWKV_PROJ), F32), jnp.zeros((2 * nb, HEAD_DIM, LANES), F32),
            eo.reshape(nb * t, RET_W), lw, nb, t, ROWS_PROMPT, nb, SCAN_STEPS)
        outs[0].append(kt); outs[1].append(vt); outs[2].append(lft)
        outs[3].append(_rwkv_state_out(s_t, nb)); outs[4].append(p3[:, -1]); outs[5].append(_ret_state_out(r_t, nb))
        fq, fk, fv, fl, p_rwkv, p_ret = _inproj(ys, lw['ln_mix_g'], lw['w_in'], ROWS_SAMPLE)
        qn, kn, lf = _foxprep_sample(fq, fk, fl, lw['qg'], lw['kg'], lw['fb'], ROWS_SAMPLE)
        fo = _foxsample(page_table, l, pad8(qn), pad8(kn), pad8(fv), pad8(lf), ckt, cvt, clt, ts)
        p3 = p_rwkv.reshape(db, ts, RWKV_PROJ)
        xt = jnp.transpose(p_ret.reshape(db, ts, 4, RET_H, HEAD_DIM), (1, 2, 3, 4, 0))
        gn_cols = jnp.broadcast_to(lw['gn'].reshape(RET_H, HEAD_DIM, 1), (RET_H, HEAD_DIM, LANES))
        eo, r_t = _retention_short(xt, cos_s, sin_s, lg_rows, gn_cols, jnp.transpose(state_ret[l], (1, 2, 3, 0)))
        eo = jnp.transpose(eo, (3, 0, 1, 2)).reshape(db * ts, RET_W)
        ys, s_t = _mix_and_ffn(
            ys, fo.reshape(db * ts, FOX_W), p_rwkv, state_rwkv_shift[l], jnp.transpose(state_rwkv[l], (1, 2, 3, 0)),
            eo, lw, db, ts, ROWS_SAMPLE, None, None)
        outs[6].append(kn); outs[7].append(fv); outs[8].append(lf[:, :FOX_H])
        outs[9].append(jnp.transpose(s_t, (3, 0, 1, 2))); outs[10].append(p3[:, -1])
        outs[11].append(jnp.transpose(r_t, (3, 0, 1, 2)))

    n_pp = nb * t // PAGE_SIZE
    st = lambda i: jnp.stack(outs[i])
    page_rows = lambda z: jnp.transpose(z.reshape(depth, n_pp, FOX_H, HEAD_DIM, PAGE_SIZE), (0, 1, 4, 2, 3))
    return (yp.reshape(nb, t, D_MODEL), ys.reshape(db, ts, D_MODEL),
            page_rows(st(0)), page_rows(st(1)), jnp.swapaxes(st(2), 2, 3),
            st(3), st(4), st(5),
            st(6).reshape(depth, db, ts, FOX_H, HEAD_DIM),
            st(7).reshape(depth, db, ts, FOX_H, HEAD_DIM),
            st(8).reshape(depth, db, ts, FOX_H),
            st(9), st(10), st(11))
```

```python
import functools

import jax
import jax.numpy as jnp
import numpy as np
from jax import lax
from jax.experimental import pallas as pl
from jax.experimental.pallas import tpu as pltpu

F32 = jnp.float32
BF16 = jnp.bfloat16

LANES = 128
SUBLANES = 8
VMEM_LIMIT = 56 * 1024 * 1024

D_MODEL = 1024
HEAD_DIM = 64
FOX_H = 8
RWKV_H = 4
RET_H = 4
FOX_W = FOX_H * HEAD_DIM
RWKV_W = RWKV_H * HEAD_DIM
RET_W = RET_H * HEAD_DIM
RWKV_LORA_W = 32
RWKV_LORA_A = 32
RWKV_LORA_G = 64
RWKV_PROJ = 3 * RWKV_W + RWKV_LORA_W + RWKV_LORA_A + RWKV_LORA_G
RET_PROJ = 4 * RET_W
D_FF = 2816
PAGE_SIZE = 128
RET_CHUNK = 128
ROPE_BASE = 10000.0
NORM_EPS = 1e-6
RWKV_GN_EPS = 64e-5
RET_GN_EPS = 1e-5
NEG_BIG = -1e30
HEADS_PER_TILE = LANES // HEAD_DIM

ROWS_PROMPT = 512
ROWS_SAMPLE = 256
ROWS_FOXPREP = 256
ATTN_BLOCK = 512
SCAN_STEPS = 128
SCAN_UNROLL = 32
RET_SEQS = 8


def _cparams(sem):
    return pltpu.CompilerParams(dimension_semantics=sem, vmem_limit_bytes=VMEM_LIMIT)


def _resident(shape):
    nd = len(shape)
    return pl.BlockSpec(shape, lambda *_: (0,) * nd, pipeline_mode=pl.Buffered(1))


def _bdot(a, b):
    return jnp.dot(a.astype(BF16), b.astype(BF16), preferred_element_type=F32)


def _bdot_nt(a, b):
    return lax.dot_general(a.astype(BF16), b.astype(BF16), (((1,), (1,)), ((), ())),
                           preferred_element_type=F32)


def _split2(x):
    hi = x.astype(BF16)
    lo = (x - hi.astype(F32)).astype(BF16)
    return hi, lo


def _split3(x):
    hi = x.astype(BF16)
    r = x - hi.astype(F32)
    mid = r.astype(BF16)
    lo = (r - mid.astype(F32)).astype(BF16)
    return hi, mid, lo


def _iota(shape, axis):
    return lax.broadcasted_iota(jnp.int32, shape, axis)


def _seg_ones():
    return (_iota((LANES, LANES), 0) // HEAD_DIM == _iota((LANES, LANES), 1) // HEAD_DIM).astype(BF16)


def _seg_sum(x, e):
    n, nblk = x.shape[0], x.shape[-1] // LANES
    stacked = x if nblk == 1 else jnp.concatenate([x[:, c * LANES:(c + 1) * LANES] for c in range(nblk)], axis=0)
    hi, lo = _split2(stacked)
    r = jnp.dot(jnp.concatenate([hi, lo], axis=0), e, preferred_element_type=F32)
    r = r[:n * nblk] + r[n * nblk:]
    return r if nblk == 1 else jnp.concatenate([r[c * n:(c + 1) * n] for c in range(nblk)], axis=-1)


def _exact_dot(x, m01, left):
    parts = _split3(x)
    n, w = x.shape
    if left:
        r = jnp.dot(m01, jnp.concatenate(parts, axis=1), preferred_element_type=F32)
        return r[:, :w] + r[:, w:2 * w] + r[:, 2 * w:]
    r = jnp.dot(jnp.concatenate(parts, axis=0), m01, preferred_element_type=F32)
    return r[:n] + r[n:2 * n] + r[2 * n:]


def _sigmoid(x):
    return 1.0 / (1.0 + jnp.exp(-x))


def _softplus(x):
    return jnp.maximum(x, 0.0) + jnp.log1p(jnp.exp(-jnp.abs(x)))


def _inproj_kernel(x_ref, g_ref, wq_ref, wk_ref, wv_ref, wl_ref, wr_ref, we_ref,
                   oq_ref, ok_ref, ov_ref, ol_ref, or_ref, oe_ref):
    x = x_ref[...]
    h = x * lax.rsqrt(jnp.mean(x * x, -1, keepdims=True) + NORM_EPS) * g_ref[...]
    hb = h.astype(BF16)
    for w_ref, o_ref in ((wq_ref, oq_ref), (wk_ref, ok_ref), (wv_ref, ov_ref), (wl_ref, ol_ref),
                         (wr_ref, or_ref), (we_ref, oe_ref)):
        o_ref[...] = jnp.dot(hb, w_ref[...], preferred_element_type=F32)


def _inproj(x, g, ws, tm):
    n = x.shape[0]
    widths = [w.shape[1] for w in ws]
    row = lambda wd: pl.BlockSpec((tm, wd), lambda i: (i, 0))
    return pl.pallas_call(
        _inproj_kernel,
        grid=(n // tm,),
        in_specs=[row(D_MODEL), _resident((1, D_MODEL))] + [_resident(w.shape) for w in ws],
        out_specs=[row(wd) for wd in widths],
        out_shape=[jax.ShapeDtypeStruct((n, wd), F32) for wd in widths],
        compiler_params=_cparams(("parallel",)),
        name="inproj",
    )(x, g, *ws)


def _fox_norms(q_ref, k_ref, fl_ref, qg_ref, kg_ref, fb_ref):
    e = _seg_ones()

    def hnorm(x, g):
        ms = _seg_sum(x * x, e) * (1.0 / HEAD_DIM)
        return x * lax.rsqrt(ms + NORM_EPS) * g

    qn = hnorm(q_ref[...], qg_ref[...]) * (HEAD_DIM ** -0.5)
    kn = hnorm(k_ref[...], kg_ref[...])
    z = fl_ref[...] + fb_ref[...]
    lf = jnp.minimum(z, 0.0) - jnp.log1p(jnp.exp(-jnp.abs(z)))
    return qn, kn, lf


def _foxprep_sample_kernel(q_ref, k_ref, fl_ref, qg_ref, kg_ref, fb_ref, qn_ref, kn_ref, lf_ref):
    qn_ref[...], kn_ref[...], lf_ref[...] = _fox_norms(q_ref, k_ref, fl_ref, qg_ref, kg_ref, fb_ref)


def _bias_placement():
    m = np.zeros((LANES, 2 * FOX_H * LANES), np.float32)
    one = 3 * FOX_H
    for h in range(FOX_H):
        spare = h * LANES + HEAD_DIM * (1 - h % HEADS_PER_TILE)
        kspare = FOX_H * LANES + spare
        for term in range(3):
            m[term * FOX_H + h, spare + term] = 1.0
            m[one, spare + 3 + term] = 1.0
            m[one, kspare + term] = 1.0
            m[term * FOX_H + h, kspare + 3 + term] = -1.0
    return jnp.asarray(m, BF16)


def _foxprep_prompt_kernel(q_ref, k_ref, fl_ref, qg_ref, kg_ref, fb_ref, v_ref, place_ref,
                           kt_ref, vt_ref, lft_ref, qx_ref, kx_ref, vb_ref, carry_ref, *, tm):
    qn, kn, lf = _fox_norms(q_ref, k_ref, fl_ref, qg_ref, kg_ref, fb_ref)
    v = v_ref[...]
    vb_ref[...] = v.astype(BF16)
    for pg in range(tm // PAGE_SIZE):
        rows = slice(pg * PAGE_SIZE, (pg + 1) * PAGE_SIZE)
        kt_ref[pg] = kn[rows, :].T
        vt_ref[pg] = v[rows, :].T
        lft_ref[pg] = lf[rows, :].T[:FOX_H, :]

    @pl.when(pl.program_id(1) == 0)
    def _():
        carry_ref[...] = jnp.zeros_like(carry_ref)

    tri = (_iota((tm, tm), 0) >= _iota((tm, tm), 1)).astype(BF16)
    c = _exact_dot(lf, tri, left=True) + carry_ref[...]
    carry_ref[...] = c[tm - 1:tm, :]

    lane = _iota((tm, LANES), 1)
    c_hi, c_mid, c_lo = (p.astype(F32) for p in _split3(c))
    packed = jnp.where(lane < FOX_H, c_hi,
                       jnp.where(lane < 2 * FOX_H, pltpu.roll(c_mid, FOX_H, 1),
                                 jnp.where(lane < 3 * FOX_H, pltpu.roll(c_lo, 2 * FOX_H, 1),
                                           jnp.where(lane == 3 * FOX_H, 1.0, 0.0))))
    ext = jnp.dot(packed.astype(BF16), place_ref[...], preferred_element_type=F32)
    for h in range(FOX_H):
        hp, hh = divmod(h, HEADS_PER_TILE)
        own = lane // HEAD_DIM == hh
        pair = slice(hp * LANES, (hp + 1) * LANES)
        tile = slice(h * LANES, (h + 1) * LANES)
        ktile = slice((FOX_H + h) * LANES, (FOX_H + h + 1) * LANES)
        qx_ref[:, tile] = jnp.where(own, qn[:, pair], ext[:, tile]).astype(BF16)
        kx_ref[:, tile] = jnp.where(own, kn[:, pair], ext[:, ktile]).astype(BF16)


def _foxprep_sample(fq, fk, fl, qg, kg, fb, tm):
    n = fq.shape[0]
    row = lambda wd: pl.BlockSpec((tm, wd), lambda i: (i, 0))
    return pl.pallas_call(
        _foxprep_sample_kernel,
        grid=(n // tm,),
        in_specs=[row(FOX_W), row(FOX_W), row(LANES), _resident((1, FOX_W)), _resident((1, FOX_W)),
                  _resident((1, LANES))],
        out_specs=[row(FOX_W), row(FOX_W), row(LANES)],
        out_shape=[jax.ShapeDtypeStruct((n, FOX_W), F32), jax.ShapeDtypeStruct((n, FOX_W), F32),
                   jax.ShapeDtypeStruct((n, LANES), F32)],
        compiler_params=_cparams(("parallel",)),
        name="foxprep_sample",
    )(fq, fk, fl, qg, kg, fb)


def _foxprep_prompt(fq, fk, fl, fv, qg, kg, fb, nseq, tm):
    n = fq.shape[0]
    nt = n // nseq // tm
    ppt = tm // PAGE_SIZE
    n_pp = n // PAGE_SIZE
    place = _bias_placement()
    row = lambda wd: pl.BlockSpec((tm, wd), lambda b, j: (b * nt + j, 0))
    pages = lambda r: pl.BlockSpec((ppt, r, PAGE_SIZE), lambda b, j: (b * nt + j, 0, 0))
    return pl.pallas_call(
        functools.partial(_foxprep_prompt_kernel, tm=tm),
        grid=(nseq, nt),
        in_specs=[row(FOX_W), row(FOX_W), row(LANES), _resident((1, FOX_W)), _resident((1, FOX_W)),
                  _resident((1, LANES)), row(FOX_W), _resident(place.shape)],
        out_specs=[pages(FOX_W), pages(FOX_W), pages(FOX_H), row(FOX_H * LANES), row(FOX_H * LANES), row(FOX_W)],
        out_shape=[jax.ShapeDtypeStruct((n_pp, FOX_W, PAGE_SIZE), F32), jax.ShapeDtypeStruct((n_pp, FOX_W, PAGE_SIZE), F32),
                   jax.ShapeDtypeStruct((n_pp, FOX_H, PAGE_SIZE), F32),
                   jax.ShapeDtypeStruct((n, FOX_H * LANES), BF16), jax.ShapeDtypeStruct((n, FOX_H * LANES), BF16),
                   jax.ShapeDtypeStruct((n, FOX_W), BF16)],
        scratch_shapes=[pltpu.VMEM((1, LANES), F32)],
        compiler_params=_cparams(("parallel", "arbitrary")),
        name="foxprep_prompt",
    )(fq, fk, fl, qg, kg, fb, fv, place)


def _foxattn_kernel(qx_ref, kx_ref, vb_ref, o_ref, *, tq):
    qi = pl.program_id(2)
    q = [qx_ref[:, hh * LANES:(hh + 1) * LANES] for hh in range(HEADS_PER_TILE)]

    def block(off, carry, diagonal):
        vb = vb_ref[pl.ds(off, tq), :]
        stats, ps = [], []
        for hh in range(HEADS_PER_TILE):
            m, l, _ = carry[hh]
            kb = kx_ref[pl.ds(off, tq), hh * LANES:(hh + 1) * LANES]
            s = lax.dot_general(q[hh], kb, (((1,), (1,)), ((), ())), preferred_element_type=F32)
            if diagonal:
                s = jnp.where(_iota((tq, tq), 1) <= _iota((tq, tq), 0), s, NEG_BIG)
            m_new = jnp.maximum(m, jnp.max(s, axis=-1, keepdims=True))
            alpha = jnp.exp(m - m_new)
            p = jnp.exp(s - m_new)
            stats.append((m_new, alpha, alpha * l + jnp.sum(p, axis=-1, keepdims=True)))
            ps.append(p.astype(BF16))
        pv = jnp.dot(jnp.concatenate(ps, axis=0), vb, preferred_element_type=F32)
        return tuple((m_new, l, alpha * carry[hh][2] + pv[hh * tq:(hh + 1) * tq])
                     for hh, (m_new, alpha, l) in enumerate(stats))

    init = tuple((jnp.full((tq, 1), NEG_BIG, F32), jnp.zeros((tq, 1), F32), jnp.zeros((tq, LANES), F32))
                 for _ in range(HEADS_PER_TILE))
    carry = lax.fori_loop(0, qi, lambda j, c: block(pl.multiple_of(j * tq, tq), c, False), init)
    carry = block(pl.multiple_of(qi * tq, tq), carry, True)
    outs = [acc / l for _, l, acc in carry]
    o_ref[...] = jnp.where(_iota((tq, LANES), 1) // HEAD_DIM == 0, outs[0], outs[1]).astype(BF16)


def _foxattn(qx, kx, vb, nseq, tq):
    n = qx.shape[0]
    t = n // nseq
    nq = t // tq
    npair = FOX_H // HEADS_PER_TILE
    pair_w = HEADS_PER_TILE * LANES
    return pl.pallas_call(
        functools.partial(_foxattn_kernel, tq=tq),
        grid=(nseq, npair, nq),
        in_specs=[
            pl.BlockSpec((tq, pair_w), lambda b, h, i: (b * nq + i, h)),
            pl.BlockSpec((t, pair_w), lambda b, h, i: (b, h)),
            pl.BlockSpec((t, LANES), lambda b, h, i: (b, h)),
        ],
        out_specs=pl.BlockSpec((tq, LANES), lambda b, h, i: (b * nq + i, h)),
        out_shape=jax.ShapeDtypeStruct((n, FOX_W), BF16),
        compiler_params=_cparams(("parallel", "parallel", "arbitrary")),
        name="foxattn",
    )(qx, kx, vb)


def _foxsample_kernel(pt_ref, q_ref, kn_ref, vn_ref, lfn_ref, *rest, n_pages, t_new):
    del pt_ref
    kp = rest[:n_pages]
    vp = rest[n_pages:2 * n_pages]
    lp = rest[2 * n_pages:3 * n_pages]
    o_ref = rest[3 * n_pages]
    nrow = t_new * FOX_H

    rep = lambda x: jnp.concatenate([x] * t_new, axis=0)
    q = q_ref[...]
    hmask = _iota((FOX_H, FOX_W), 1) // HEAD_DIM == _iota((FOX_H, FOX_W), 0)
    qbd = jnp.concatenate(
        [jnp.where(hmask, jnp.broadcast_to(q[t:t + 1, :], (FOX_H, FOX_W)), 0.0) for t in range(t_new)],
        axis=0).astype(BF16)

    lfn = lfn_ref[...]
    diag8 = _iota((FOX_H, LANES), 1) == _iota((FOX_H, LANES), 0)
    cn_cols = []
    run = jnp.zeros((1, LANES), F32)
    for t in range(t_new):
        run = run + lfn[t:t + 1, :]
        cn_cols.append(jnp.sum(jnp.where(diag8, jnp.broadcast_to(run, (FOX_H, LANES)), 0.0), axis=-1, keepdims=True))
    cn = jnp.concatenate(cn_cols, axis=0)

    zeros_tail = jnp.zeros((PAGE_SIZE - SUBLANES, FOX_W), F32)
    key = _iota((nrow, PAGE_SIZE), 1)
    trow = _iota((nrow, PAGE_SIZE), 0) // FOX_H
    ckey = jnp.zeros((nrow, PAGE_SIZE), F32)
    for j in range(t_new):
        ckey = jnp.where(key == j, rep(cn_cols[j]), ckey)
    s_new = _bdot_nt(qbd, jnp.concatenate([kn_ref[...], zeros_tail], axis=0)) + (cn - ckey)
    s_new = jnp.where(key <= trow, s_new, NEG_BIG)

    lf_all = jnp.concatenate([lp[i][...] for i in range(n_pages)], axis=0)
    upper = (_iota((PAGE_SIZE, PAGE_SIZE), 0) <= _iota((PAGE_SIZE, PAGE_SIZE), 1)).astype(BF16)
    cp_all = _exact_dot(lf_all, upper, left=False)
    before = [jnp.zeros((FOX_H, 1), F32)]
    for i in range(n_pages):
        before.append(before[-1] + cp_all[i * FOX_H:(i + 1) * FOX_H, PAGE_SIZE - 1:PAGE_SIZE])
    cq_abs = rep(before[n_pages]) + cn
    s_past = []
    for i in range(n_pages):
        cp = rep(cp_all[i * FOX_H:(i + 1) * FOX_H, :] + before[i])
        s_past.append(_bdot(qbd, kp[i][...].reshape(FOX_W, PAGE_SIZE)) + (cq_abs - cp))

    m = jnp.max(s_new, axis=-1, keepdims=True)
    for s in s_past:
        m = jnp.maximum(m, jnp.max(s, axis=-1, keepdims=True))
    p = jnp.exp(s_new - m)
    l = jnp.sum(p, axis=-1, keepdims=True)
    o = _bdot(p, jnp.concatenate([vn_ref[...], zeros_tail], axis=0))
    for i in range(n_pages):
        p = jnp.exp(s_past[i] - m)
        l = l + jnp.sum(p, axis=-1, keepdims=True)
        o = o + _bdot_nt(p, vp[i][...].reshape(FOX_W, PAGE_SIZE))
    o = o / l
    omask = _iota((nrow, FOX_W), 1) // HEAD_DIM == _iota((nrow, FOX_W), 0) % FOX_H
    o_ref[...] = jnp.sum(jnp.where(omask, o, 0.0).reshape(t_new, FOX_H, FOX_W), axis=1)


def _foxsample(page_table, layer, qn, kn, vn, lfn, cache_kt, cache_vt, cache_lft, t_new):
    db, n_pages = page_table.shape
    pt = page_table.reshape(-1)
    new = lambda wd: pl.BlockSpec((None, SUBLANES, wd), lambda b, pt: (b, 0, 0))

    def page(shape, i):
        nd = len(shape)
        return pl.BlockSpec((None, None) + shape, lambda b, pt: (layer, pt[b * n_pages + i]) + (0,) * nd)

    in_specs = [new(FOX_W), new(FOX_W), new(FOX_W), new(LANES)]
    in_specs += [page((FOX_H, HEAD_DIM, PAGE_SIZE), i) for i in range(n_pages)]
    in_specs += [page((FOX_H, HEAD_DIM, PAGE_SIZE), i) for i in range(n_pages)]
    in_specs += [page((FOX_H, PAGE_SIZE), i) for i in range(n_pages)]
    return pl.pallas_call(
        functools.partial(_foxsample_kernel, n_pages=n_pages, t_new=t_new),
        grid_spec=pltpu.PrefetchScalarGridSpec(
            num_scalar_prefetch=1,
            grid=(db,),
            in_specs=in_specs,
            out_specs=pl.BlockSpec((None, t_new, FOX_W), lambda b, pt: (b, 0, 0)),
        ),
        out_shape=jax.ShapeDtypeStruct((db, t_new, FOX_W), F32),
        compiler_params=_cparams(("arbitrary",)),
        name="foxsample",
    )(pt, qn, kn, vn, lfn, *([cache_kt] * n_pages), *([cache_vt] * n_pages), *([cache_lft] * n_pages))


def _rwkvprep_kernel(p_ref, init_ref, mu_ref, w0_ref, w2_ref, a0_ref, a2_ref, g2_ref, kk_ref, ka_ref,
                     r_out, w_out, k_out, v_out, a_out, b_out, g_out, *carry, tm, period):
    e = _seg_ones()
    p = p_ref[...]
    rolled = pltpu.roll(p, 1, 0)
    row = _iota((tm, RWKV_PROJ), 0)
    if period >= tm:
        carry_ref, = carry

        @pl.when(pl.program_id(1) == 0)
        def _():
            carry_ref[...] = init_ref[...]

        pp = jnp.where(row == 0, carry_ref[...], rolled)
        carry_ref[...] = p[tm - 1:tm, :]
    else:
        pp = jnp.where(row % period == 0, init_ref[...], rolled)
    xs = p + (pp - p) * mu_ref[...]
    r = xs[:, 0:RWKV_W]
    k = xs[:, RWKV_W:2 * RWKV_W]
    v = xs[:, 2 * RWKV_W:3 * RWKV_W]
    lo = xs[:, 3 * RWKV_W:]
    w = -_softplus(-(w0_ref[...] + _bdot(jnp.tanh(lo), w2_ref[...]))) - 0.5
    a = _sigmoid(a0_ref[...] + _bdot(lo, a2_ref[...]))
    kk = k * kk_ref[...]
    nrm = jnp.sqrt(_seg_sum(kk * kk, e))
    kk = kk / jnp.maximum(nrm, 1e-12)
    r_out[...] = r
    w_out[...] = jnp.exp(-jnp.exp(w))
    k_out[...] = k * (1.0 + (a - 1.0) * ka_ref[...])
    v_out[...] = v
    a_out[...] = -kk
    b_out[...] = kk * a
    g_out[...] = _bdot(_sigmoid(lo), g2_ref[...])


def _rwkvprep(p, shift, period, mu, w0, w2p, a0, a2p, g2p, k_k, k_a, tm):
    n = p.shape[0]
    nseq = n // period
    vec = _resident((1, RWKV_W))
    lora = _resident((LANES, RWKV_W))
    if period >= tm:
        nt = period // tm
        grid = (nseq, nt)
        row = lambda wd: pl.BlockSpec((tm, wd), lambda b, j: (b * nt + j, 0))
        init = shift[:, None, :]
        init_spec = pl.BlockSpec((None, 1, RWKV_PROJ), lambda b, j: (b, 0, 0))
        scratch = [pltpu.VMEM((1, RWKV_PROJ), F32)]
        sem = ("parallel", "arbitrary")
    else:
        grid = (n // tm,)
        row = lambda wd: pl.BlockSpec((tm, wd), lambda i: (i, 0))
        init = jnp.repeat(shift, period, axis=0)
        init_spec = row(RWKV_PROJ)
        scratch = []
        sem = ("parallel",)
    return pl.pallas_call(
        functools.partial(_rwkvprep_kernel, tm=tm, period=period),
        grid=grid,
        in_specs=[row(RWKV_PROJ), init_spec, _resident((1, RWKV_PROJ)), vec, lora, vec, lora, lora, vec, vec],
        out_specs=[row(RWKV_W)] * 7,
        out_shape=[jax.ShapeDtypeStruct((n, RWKV_W), F32)] * 7,
        scratch_shapes=scratch,
        compiler_params=_cparams(sem),
        name="rwkvprep",
    )(p, init, mu, w0, w2p, a0, a2p, g2p, k_k, k_a)


def _rwkvscan_kernel(r_ref, w_ref, k_ref, v_ref, a_ref, b_ref, s0_ref, y_ref, st_ref, s_scr, *, bblk, tblk):
    tb = pl.program_id(1)
    ng = HEADS_PER_TILE * bblk

    @pl.when(tb == 0)
    def _():
        s_scr[...] = s0_ref[...]

    e = _seg_ones()
    diag = (_iota((HEAD_DIM, LANES), 1) % HEAD_DIM == _iota((HEAD_DIM, LANES), 0)).astype(BF16)[None]
    half = (_iota((SUBLANES, LANES), 1) // HEAD_DIM == _iota((SUBLANES, LANES), 0)).astype(BF16)
    first = _iota((1, LANES), 1) < HEAD_DIM

    sub = min(SCAN_UNROLL, tblk)

    def chunk(c, carry):
        off = pl.multiple_of(c * sub, sub)
        tiles = [[ref[b, pl.ds(off, sub), :] for b in range(bblk)] for ref in (r_ref, w_ref, k_ref, v_ref, a_ref, b_ref)]

        def rows(kind, i):
            return jnp.concatenate(
                [tiles[kind][b][i:i + 1, hp * LANES:(hp + 1) * LANES][None] for b in range(bblk) for hp in range(2)],
                axis=0)

        vdiag = jnp.concatenate([rows(3, i).astype(BF16) * diag for i in range(sub)], axis=0)
        vcol = jnp.dot(vdiag.reshape(sub * ng * HEAD_DIM, LANES), e,
                       preferred_element_type=F32).reshape(sub * ng, HEAD_DIM, LANES)

        s = s_scr[...]
        srs = []
        for i in range(sub):
            sa = jnp.dot((s * rows(4, i)).reshape(ng * HEAD_DIM, LANES).astype(BF16), e,
                         preferred_element_type=F32).reshape(ng, HEAD_DIM, LANES)
            s = s * rows(1, i) + sa * rows(5, i) + vcol[i * ng:(i + 1) * ng] * rows(2, i)
            srs.append((s * rows(0, i)).astype(BF16))
        s_scr[...] = s

        sr = jnp.concatenate(srs, axis=0).reshape(sub * ng * HEAD_DIM, LANES)
        out = lax.dot_general(half, sr, (((1,), (1,)), ((), ())), preferred_element_type=F32)
        for b in range(bblk):
            ytile = []
            for i in range(sub):
                lanes = slice((i * bblk + b) * 2 * HEAD_DIM, (i * bblk + b + 1) * 2 * HEAD_DIM)
                h0, h1 = out[0:1, lanes], out[1:2, lanes]
                ytile.append(jnp.concatenate(
                    [jnp.where(first, h0, pltpu.roll(h1, HEAD_DIM, 1)),
                     jnp.where(first, pltpu.roll(h0, HEAD_DIM, 1), h1)], axis=1))
            y_ref[b, pl.ds(off, sub), :] = jnp.concatenate(ytile, axis=0)
        return carry

    lax.fori_loop(0, tblk // sub, chunk, 0)

    @pl.when(tb == pl.num_programs(1) - 1)
    def _():
        st_ref[...] = s_scr[...]


def _rwkvscan(r, w, k, v, a, b, s0, bblk, tblk):
    nb, t, _ = r.shape
    ng = HEADS_PER_TILE * bblk
    seq = pl.BlockSpec((bblk, tblk, RWKV_W), lambda i, j: (i, j, 0))
    st = pl.BlockSpec((ng, HEAD_DIM, LANES), lambda i, j: (i, 0, 0))
    return pl.pallas_call(
        functools.partial(_rwkvscan_kernel, bblk=bblk, tblk=tblk),
        grid=(nb // bblk, t // tblk),
        in_specs=[seq] * 6 + [st],
        out_specs=[seq, st],
        out_shape=[jax.ShapeDtypeStruct((nb, t, RWKV_W), F32), jax.ShapeDtypeStruct(s0.shape, F32)],
        scratch_shapes=[pltpu.VMEM((ng, HEAD_DIM, LANES), F32)],
        compiler_params=_cparams(("parallel", "arbitrary")),
        name="rwkvscan",
    )(r, w, k, v, a, b, s0)


def _rwkv_short_kernel(z_ref, s0_ref, y_ref, st_ref):
    nt = z_ref.shape[0]
    R, W, K, V, A, B = range(6)

    def value_rows(i, carry):
        base = pl.multiple_of(i * SUBLANES, SUBLANES)
        vt = [z_ref[t, V, pl.ds(base, SUBLANES), :] for t in range(nt)]
        yrows = [[] for _ in range(nt)]
        for j in range(SUBLANES):
            s = s0_ref[base + j]
            for t in range(nt):
                sa = jnp.sum(s * z_ref[t, A], axis=0, keepdims=True)
                s = s * z_ref[t, W] + sa * z_ref[t, B] + vt[t][j:j + 1, :] * z_ref[t, K]
                yrows[t].append(jnp.sum(s * z_ref[t, R], axis=0, keepdims=True))
            st_ref[base + j] = s
        for t in range(nt):
            y_ref[t, pl.ds(base, SUBLANES), :] = jnp.concatenate(yrows[t], axis=0)
        return carry

    lax.fori_loop(0, HEAD_DIM // SUBLANES, value_rows, 0)


def _rwkv_short(zt, s0):
    nt, _, nh, _, nb = zt.shape
    assert nb == LANES and nt <= SUBLANES
    st = pl.BlockSpec((None, HEAD_DIM, HEAD_DIM, LANES), lambda h: (h, 0, 0, 0))
    return pl.pallas_call(
        _rwkv_short_kernel,
        grid=(nh,),
        in_specs=[pl.BlockSpec((nt, 6, None, HEAD_DIM, LANES), lambda h: (0, 0, h, 0, 0)), st],
        out_specs=[pl.BlockSpec((nt, None, HEAD_DIM, LANES), lambda h: (0, h, 0, 0)), st],
        out_shape=[jax.ShapeDtypeStruct((nt, nh, HEAD_DIM, LANES), F32), jax.ShapeDtypeStruct(s0.shape, F32)],
        compiler_params=_cparams(("parallel",)),
        name="rwkv_short",
    )(zt, s0)


def _rope_kernel(inv_ref, cos_ref, sin_ref, *, pos0):
    t = cos_ref.shape[0]
    pos = (pos0 + _iota((t, LANES), 0)).astype(F32)
    ang = pos * inv_ref[...]
    first_half = _iota((t, LANES), 1) % HEAD_DIM < HEAD_DIM // 2
    cos_ref[...] = jnp.cos(ang)
    sin_ref[...] = jnp.where(first_half, -jnp.sin(ang), jnp.sin(ang))


def _rope_tables(inv_lanes, t, pos0):
    return pl.pallas_call(
        functools.partial(_rope_kernel, pos0=pos0),
        out_shape=[jax.ShapeDtypeStruct((t, LANES), F32)] * 2,
        name="rope",
    )(inv_lanes)


def _ret_kernel(x_ref, cos_ref, sin_ref, lg_ref, lgh_ref, gn_ref, r0_ref, o_ref, rt_ref, r_scr,
                *, bs):
    lp = RET_CHUNK
    npair = RET_H // HEADS_PER_TILE
    c = pl.program_id(1)

    @pl.when(c == 0)
    def _():
        r_scr[...] = r0_ref[...]

    lane = _iota((lp, LANES), 1)
    first_half = lane % HEAD_DIM < HEAD_DIM // 2
    cos = cos_ref[...]
    sin = sin_ref[...]

    def rope(x):
        swapped = jnp.where(first_half, pltpu.roll(x, LANES - HEAD_DIM // 2, 1), pltpu.roll(x, HEAD_DIM // 2, 1))
        return x * cos + swapped * sin

    idx = _iota((lp, 1), 0).astype(F32)
    diff = (_iota((lp, lp), 0) - _iota((lp, lp), 1)).astype(F32)
    same_head = _iota((LANES, LANES), 0) // HEAD_DIM == _iota((LANES, LANES), 1) // HEAD_DIM
    e = _seg_ones()
    dmask, cross, kdec, cdec = [], [], [], []
    for hp in range(npair):
        lg = lg_ref[hp]
        dmask.append(jnp.concatenate(
            [jnp.where(diff >= 0.0, jnp.exp(lgh_ref[hp, hh:hh + 1, :] * jnp.maximum(diff, 0.0)), 0.0)
             for hh in range(HEADS_PER_TILE)], axis=0))
        cross.append(jnp.exp(lg * (idx + 1.0)))
        kdec.append(jnp.exp(lg * (lp - 1.0 - idx)))
        cdec.append(jnp.exp(lg * float(lp)))

    chains = [(s, hp) for s in range(bs) for hp in range(npair)]
    os_ = []
    for s, hp in chains:
        col = lambda j: x_ref[s, :, (j * npair + hp) * LANES:(j * npair + hp + 1) * LANES]
        q = rope(col(0))
        k = rope(col(1)) * (HEAD_DIM ** -0.5)
        vb = col(2).astype(BF16)
        qh = jnp.concatenate([jnp.where(lane // HEAD_DIM == hh, q, 0.0) for hh in range(HEADS_PER_TILE)], axis=0)
        sc = _bdot_nt(qh, k) * dmask[hp]
        ov = jnp.dot(sc.astype(BF16), vb, preferred_element_type=F32)
        o = jnp.where(lane // HEAD_DIM == 0, ov[:lp], ov[lp:])
        r = r_scr[s * npair + hp]
        os_.append(o + _bdot(q, r) * cross[hp])
        upd = jnp.dot((k * kdec[hp]).T.astype(BF16), vb, preferred_element_type=F32)
        r_scr[s * npair + hp] = r * cdec[hp] + jnp.where(same_head, upd, 0.0)

    o_all = jnp.concatenate(os_, axis=0)
    xc = o_all - _seg_sum(o_all, e) * (1.0 / HEAD_DIM)
    yn = xc * lax.rsqrt(_seg_sum(xc * xc, e) * (1.0 / HEAD_DIM) + RET_GN_EPS)
    for i, (s, hp) in enumerate(chains):
        g = x_ref[s, :, (3 * npair + hp) * LANES:(3 * npair + hp + 1) * LANES]
        o_ref[s, :, hp * LANES:(hp + 1) * LANES] = (
            yn[i * lp:(i + 1) * lp] * gn_ref[:, hp * LANES:(hp + 1) * LANES] * (g * _sigmoid(g))).astype(BF16)

    @pl.when(c == pl.num_programs(1) - 1)
    def _():
        rt_ref[...] = r_scr[...]


def _retention(proj, cos, sin, lg, lgh, gn, r0, bs):
    nb, t, _ = proj.shape
    l_in = RET_CHUNK
    nc = t // l_in
    npair = RET_H // HEADS_PER_TILE
    st = pl.BlockSpec((bs * npair, LANES, LANES), lambda b, c: (b, 0, 0))
    tab = pl.BlockSpec((l_in, LANES), lambda b, c: (c, 0))
    return pl.pallas_call(
        functools.partial(_ret_kernel, bs=bs),
        grid=(nb // bs, nc),
        in_specs=[pl.BlockSpec((bs, l_in, 4 * RET_W), lambda b, c: (b, c, 0)), tab, tab,
                  _resident(lg.shape), _resident(lgh.shape), _resident(gn.shape), st],
        out_specs=[pl.BlockSpec((bs, l_in, RET_W), lambda b, c: (b, c, 0)), st],
        out_shape=[jax.ShapeDtypeStruct((nb, t, RET_W), BF16), jax.ShapeDtypeStruct(r0.shape, F32)],
        scratch_shapes=[pltpu.VMEM((bs * npair, LANES, LANES), F32)],
        compiler_params=_cparams(("parallel", "arbitrary")),
        name="retention",
    )(proj, cos, sin, lg, lgh, gn, r0)


def _rope_cols_kernel(inv_ref, cos_ref, sin_ref, *, pos0):
    for t in range(cos_ref.shape[0]):
        ang = float(pos0 + t) * inv_ref[...]
        cos_ref[t] = jnp.cos(ang)
        sin_ref[t] = jnp.sin(ang)


def _rope_cols(inv_cols, t, pos0):
    return pl.pallas_call(
        functools.partial(_rope_cols_kernel, pos0=pos0),
        out_shape=[jax.ShapeDtypeStruct((t, HEAD_DIM // 2, LANES), F32)] * 2,
        name="rope_cols",
    )(inv_cols)


def _ret_short_kernel(x_ref, cos_ref, sin_ref, lg_ref, gn_ref, r0_ref, o_ref, rt_ref, q_scr, k_scr):
    nt = x_ref.shape[0]
    half = HEAD_DIM // 2
    lg = lg_ref[0:1, :]
    gam = lambda n: jnp.exp(lg * float(n))

    def rope(x, t):
        x1, x2 = x[:half], x[half:]
        c, s = cos_ref[t], sin_ref[t]
        return jnp.concatenate([x1 * c - x2 * s, x2 * c + x1 * s], axis=0)

    q = [rope(x_ref[t, 0], t) for t in range(nt)]
    k = [rope(x_ref[t, 1], t) * (HEAD_DIM ** -0.5) for t in range(nt)]
    v = [x_ref[t, 2] for t in range(nt)]

    intra = []
    for t in range(nt):
        acc = None
        for t2 in range(t + 1):
            term = (jnp.sum(q[t] * k[t2], axis=0, keepdims=True) * gam(t - t2)) * v[t2]
            acc = term if acc is None else acc + term
        intra.append(acc)

    for t in range(nt):
        q_scr[t] = q[t]
        k_scr[t] = k[t] * gam(nt - 1 - t)
    decay_all = gam(nt)

    def key_rows(i, acc):
        base = pl.multiple_of(i * SUBLANES, SUBLANES)
        qb = [q_scr[t, pl.ds(base, SUBLANES), :] for t in range(nt)]
        kb = [k_scr[t, pl.ds(base, SUBLANES), :] for t in range(nt)]
        acc = list(acc)
        for j in range(SUBLANES):
            r = r0_ref[base + j]
            upd = r * decay_all
            for t in range(nt):
                acc[t] = acc[t] + qb[t][j:j + 1, :] * r
                upd = upd + kb[t][j:j + 1, :] * v[t]
            rt_ref[base + j] = upd
        return tuple(acc)

    cross = lax.fori_loop(0, HEAD_DIM // SUBLANES, key_rows,
                          tuple(jnp.zeros((HEAD_DIM, LANES), F32) for _ in range(nt)))

    for t in range(nt):
        o = intra[t] + cross[t] * gam(t + 1)
        mean = jnp.mean(o, axis=0, keepdims=True)
        xc = o - mean
        var = jnp.mean(xc * xc, axis=0, keepdims=True)
        g = x_ref[t, 3]
        o_ref[t] = xc * lax.rsqrt(var + RET_GN_EPS) * gn_ref[...] * (g * _sigmoid(g))


def _retention_short(xt, cos, sin, lg_rows, gn_cols, r0):
    nt, _, nh, _, nb = xt.shape
    assert nb == LANES and nt <= SUBLANES
    tab = _resident(cos.shape)
    return pl.pallas_call(
        _ret_short_kernel,
        grid=(nh,),
        in_specs=[pl.BlockSpec((nt, 4, None, HEAD_DIM, LANES), lambda h: (0, 0, h, 0, 0)), tab, tab,
                  pl.BlockSpec((None, SUBLANES, LANES), lambda h: (h, 0, 0)),
                  pl.BlockSpec((None, HEAD_DIM, LANES), lambda h: (h, 0, 0)),
                  pl.BlockSpec((None, HEAD_DIM, HEAD_DIM, LANES), lambda h: (h, 0, 0, 0))],
        out_specs=[pl.BlockSpec((nt, None, HEAD_DIM, LANES), lambda h: (0, h, 0, 0)),
                   pl.BlockSpec((None, HEAD_DIM, HEAD_DIM, LANES), lambda h: (h, 0, 0, 0))],
        out_shape=[jax.ShapeDtypeStruct((nt, nh, HEAD_DIM, LANES), F32), jax.ShapeDtypeStruct(r0.shape, F32)],
        scratch_shapes=[pltpu.VMEM((nt, HEAD_DIM, LANES), F32), pltpu.VMEM((nt, HEAD_DIM, LANES), F32)],
        compiler_params=_cparams(("parallel",)),
        name="retention_short",
    )(xt, cos, sin, lg_rows, gn_cols, r0)


def _outproj_kernel(x_ref, fo_ref, y_ref, r_ref, k_ref, v_ref, g_ref, eo_ref, lw_ref, lb_ref, rk_ref,
                    wf_ref, wr_ref, we_ref, o_ref):
    e = _seg_ones()
    y = y_ref[...]
    mean = _seg_sum(y, e) * (1.0 / HEAD_DIM)
    yc = y - mean
    var = _seg_sum(yc * yc, e) * (1.0 / HEAD_DIM)
    yn = yc * lax.rsqrt(var + RWKV_GN_EPS) * lw_ref[...] + lb_ref[...]
    v = v_ref[...]
    bonus = _seg_sum(r_ref[...] * k_ref[...] * rk_ref[...], e) * v
    ro = (yn + bonus) * g_ref[...]
    o_ref[...] = (x_ref[...] + _bdot(fo_ref[...], wf_ref[...]) + _bdot(ro, wr_ref[...])
                  + _bdot(eo_ref[...], we_ref[...]))


def _outproj(x, fo, y, r, k, v, g, eo, lw, lb, rk, wf, wr, we, tm):
    n = x.shape[0]
    row = lambda wd: pl.BlockSpec((tm, wd), lambda i: (i, 0))
    vec = _resident((1, RWKV_W))
    return pl.pallas_call(
        _outproj_kernel,
        grid=(n // tm,),
        in_specs=[row(D_MODEL), row(FOX_W)] + [row(RWKV_W)] * 6 + [vec, vec, vec,
                  _resident(wf.shape), _resident(wr.shape), _resident(we.shape)],
        out_specs=row(D_MODEL),
        out_shape=jax.ShapeDtypeStruct((n, D_MODEL), F32),
        compiler_params=_cparams(("parallel",)),
        name="outproj",
    )(x, fo, y, r, k, v, g, eo, lw, lb, rk, wf, wr, we)


def _ffn_kernel(x_ref, g_ref, wg_ref, wu_ref, wd_ref, o_ref):
    x = x_ref[...]
    h = (x * lax.rsqrt(jnp.mean(x * x, -1, keepdims=True) + NORM_EPS) * g_ref[...]).astype(BF16)
    gate = jnp.dot(h, wg_ref[...], preferred_element_type=F32)
    up = jnp.dot(h, wu_ref[...], preferred_element_type=F32)
    act = gate * _sigmoid(gate) * up
    o_ref[...] = x + _bdot(act, wd_ref[...])


def _ffn(x, g, wg, wu, wd, tm):
    n = x.shape[0]
    row = pl.BlockSpec((tm, D_MODEL), lambda i: (i, 0))
    return pl.pallas_call(
        _ffn_kernel,
        grid=(n // tm,),
        in_specs=[row, _resident((1, D_MODEL)), _resident(wg.shape), _resident(wu.shape), _resident(wd.shape)],
        out_specs=row,
        out_shape=jax.ShapeDtypeStruct((n, D_MODEL), F32),
        compiler_params=_cparams(("parallel",)),
        name="ffn",
    )(x, g, wg, wu, wd)


def _rwkv_state_out(s, nb):
    s = s.reshape(nb, RWKV_H // 2, HEAD_DIM, 2, HEAD_DIM).transpose(0, 1, 3, 2, 4)
    return s.reshape(nb, RWKV_H, HEAD_DIM, HEAD_DIM)


def _ret_state_out(r, nb):
    r = r.reshape(nb, RET_H // 2, LANES, LANES)
    return jnp.stack([r[:, :, :HEAD_DIM, :HEAD_DIM], r[:, :, HEAD_DIM:, HEAD_DIM:]], axis=2).reshape(
        nb, RET_H, HEAD_DIM, HEAD_DIM)


def _layer_weights(l, ln_mix_g, w_in, fox_qn_g, fox_kn_g, fox_f_b, rwkv_mu, rwkv_w0, rwkv_w2, rwkv_a0, rwkv_a2,
                   rwkv_g2, rwkv_k_k, rwkv_k_a, rwkv_r_k, rwkv_lnx_w, rwkv_lnx_b, ret_gn_w, w_out, ln_ffn_g,
                   w_gate, w_up, w_down):
    wi = w_in[l].astype(BF16)
    o_f = 3 * FOX_W
    o_r = o_f + FOX_H
    o_e = o_r + RWKV_PROJ
    pad_rows = lambda w, off: jnp.zeros((LANES, RWKV_W), BF16).at[off:off + w.shape[0]].set(w.astype(BF16))
    wo = w_out[l].astype(BF16)
    return dict(
        ln_mix_g=ln_mix_g[l][None],
        w_in=[wi[:, 0:FOX_W], wi[:, FOX_W:2 * FOX_W], wi[:, 2 * FOX_W:o_f],
              jnp.pad(wi[:, o_f:o_r], ((0, 0), (0, LANES - FOX_H))), wi[:, o_r:o_e], wi[:, o_e:]],
        qg=jnp.tile(fox_qn_g[l], FOX_H)[None], kg=jnp.tile(fox_kn_g[l], FOX_H)[None],
        fb=jnp.pad(fox_f_b[l], (0, LANES - FOX_H))[None],
        mu=rwkv_mu[l][None], w0=rwkv_w0[l][None], a0=rwkv_a0[l][None],
        w2=pad_rows(rwkv_w2[l], 0), a2=pad_rows(rwkv_a2[l], RWKV_LORA_W),
        g2=pad_rows(rwkv_g2[l], RWKV_LORA_W + RWKV_LORA_A),
        k_k=rwkv_k_k[l][None], k_a=rwkv_k_a[l][None], r_k=rwkv_r_k[l].reshape(1, RWKV_W),
        lnx_w=rwkv_lnx_w[l][None], lnx_b=rwkv_lnx_b[l][None], gn=ret_gn_w[l][None],
        wo_f=wo[:FOX_W], wo_r=wo[FOX_W:FOX_W + RWKV_W], wo_e=wo[FOX_W + RWKV_W:],
        ln_ffn_g=ln_ffn_g[l][None],
        w_gate=w_gate[l].astype(BF16), w_up=w_up[l].astype(BF16), w_down=w_down[l].astype(BF16),
    )


def _mix_and_ffn(x2, fo, p_rwkv, shift, s0, eo, lw, nb, t, tm, bblk, tblk):
    r, w, k, v, a, b, g = _rwkvprep(p_rwkv, shift, t, lw['mu'], lw['w0'], lw['w2'], lw['a0'], lw['a2'], lw['g2'],
                                    lw['k_k'], lw['k_a'], tm)
    if s0.ndim == 4:
        zt = jnp.stack([r, w, k, v, a, b], axis=1).reshape(nb, t, 6, RWKV_H, HEAD_DIM)
        y, s_t = _rwkv_short(jnp.transpose(zt, (1, 2, 3, 4, 0)), s0)
        y = jnp.transpose(y, (3, 0, 1, 2))
    else:
        seq = lambda z: z.reshape(nb, t, RWKV_W)
        y, s_t = _rwkvscan(seq(r), seq(w), seq(k), seq(v), seq(a), seq(b), s0, bblk, tblk)
    x2 = _outproj(x2, fo, y.reshape(nb * t, RWKV_W), r, k, v, g, eo, lw['lnx_w'], lw['lnx_b'], lw['r_k'],
                  lw['wo_f'], lw['wo_r'], lw['wo_e'], tm)
    x2 = _ffn(x2, lw['ln_ffn_g'], lw['w_gate'], lw['w_up'], lw['w_down'], tm)
    return x2, s_t


def kernel(x_prompt, x_sample, cache_fox_k, cache_fox_v, cache_fox_logf, state_rwkv, state_rwkv_shift, state_ret,
           page_table, ln_mix_g, w_in, fox_qn_g, fox_kn_g, fox_f_b, rwkv_mu, rwkv_w0, rwkv_w2, rwkv_a0, rwkv_a2,
           rwkv_g2, rwkv_k_k, rwkv_k_a, rwkv_r_k, rwkv_lnx_w, rwkv_lnx_b, ret_gn_w, w_out, ln_ffn_g, w_gate,
           w_up, w_down):
    nb, t, _ = x_prompt.shape
    db, ts, _ = x_sample.shape
    depth = w_in.shape[0]
    n_pages = page_table.shape[1]
    past_len = n_pages * PAGE_SIZE
    n_pool = cache_fox_k.shape[1]

    half = HEAD_DIM // 2
    inv = ROPE_BASE ** (-jnp.arange(half, dtype=F32) / half)
    inv_lanes = jnp.tile(inv, LANES // half)[None]
    log_gamma = jnp.log1p(-jnp.exp2(-5.0 - jnp.arange(RET_H, dtype=F32)))
    lg = jnp.repeat(log_gamma, HEAD_DIM).reshape(RET_H // 2, 1, LANES)
    lgh = jnp.broadcast_to(log_gamma.reshape(RET_H // 2, 2, 1), (RET_H // 2, 2, LANES))
    cos_p, sin_p = _rope_tables(inv_lanes, t, 0)
    cos_s, sin_s = _rope_cols(jnp.broadcast_to(inv[:, None], (half, LANES)), ts, past_len)
    lg_rows = jnp.broadcast_to(log_gamma[:, None, None], (RET_H, SUBLANES, LANES))

    ckt = jnp.transpose(cache_fox_k, (0, 1, 3, 4, 2))
    cvt = jnp.transpose(cache_fox_v, (0, 1, 3, 4, 2))
    clt = jnp.swapaxes(cache_fox_logf, 2, 3)

    pad8 = lambda z: jnp.pad(z.reshape(db, ts, -1), ((0, 0), (0, SUBLANES - ts), (0, 0)))

    yp = x_prompt.reshape(nb * t, D_MODEL)
    ys = x_sample.reshape(db * ts, D_MODEL)
    outs = [[] for _ in range(12)]
    for l in range(depth):
        lw = _layer_weights(l, ln_mix_g, w_in, fox_qn_g, fox_kn_g, fox_f_b, rwkv_mu, rwkv_w0, rwkv_w2, rwkv_a0,
                            rwkv_a2, rwkv_g2, rwkv_k_k, rwkv_k_a, rwkv_r_k, rwkv_lnx_w, rwkv_lnx_b, ret_gn_w,
                            w_out, ln_ffn_g, w_gate, w_up, w_down)
        fq, fk, fv, fl, p_rwkv, p_ret = _inproj(yp, lw['ln_mix_g'], lw['w_in'], ROWS_PROMPT)
        kt, vt, lft, qx, kx, vb = _foxprep_prompt(fq, fk, fl, fv, lw['qg'], lw['kg'], lw['fb'], nb, ROWS_FOXPREP)
        fo = _foxattn(qx, kx, vb, nb, ATTN_BLOCK)
        p3 = p_rwkv.reshape(nb, t, RWKV_PROJ)
        eo, r_t = _retention(p_ret.reshape(nb, t, RET_PROJ), cos_p, sin_p, lg, lgh, lw['gn'],
                             jnp.zeros((2 * nb, LANES, LANES), F32), RET_SEQS)
        yp, s_t = _mix_and_ffn(
            yp, fo, p_rwkv, jnp.zeros((nb, RWKV_PROJ), F32), jnp.zeros((2 * nb, HEAD_DIM, LANES), F32),
            eo.reshape(nb * t, RET_W), lw, nb, t, ROWS_PROMPT, nb, SCAN_STEPS)
        outs[0].append(kt); outs[1].append(vt); outs[2].append(lft)
        outs[3].append(_rwkv_state_out(s_t, nb)); outs[4].append(p3[:, -1]); outs[5].append(_ret_state_out(r_t, nb))
        fq, fk, fv, fl, p_rwkv, p_ret = _inproj(ys, lw['ln_mix_g'], lw['w_in'], ROWS_SAMPLE)
        qn, kn, lf = _foxprep_sample(fq, fk, fl, lw['qg'], lw['kg'], lw['fb'], ROWS_SAMPLE)
        fo = _foxsample(page_table, l, pad8(qn), pad8(kn), pad8(fv), pad8(lf), ckt, cvt, clt, ts)
        p3 = p_rwkv.reshape(db, ts, RWKV_PROJ)
        xt = jnp.transpose(p_ret.reshape(db, ts, 4, RET_H, HEAD_DIM), (1, 2, 3, 4, 0))
        gn_cols = jnp.broadcast_to(lw['gn'].reshape(RET_H, HEAD_DIM, 1), (RET_H, HEAD_DIM, LANES))
        eo, r_t = _retention_short(xt, cos_s, sin_s, lg_rows, gn_cols, jnp.transpose(state_ret[l], (1, 2, 3, 0)))
        eo = jnp.transpose(eo, (3, 0, 1, 2)).reshape(db * ts, RET_W)
        ys, s_t = _mix_and_ffn(
            ys, fo.reshape(db * ts, FOX_W), p_rwkv, state_rwkv_shift[l], jnp.transpose(state_rwkv[l], (1, 2, 3, 0)),
            eo, lw, db, ts, ROWS_SAMPLE, None, None)
        outs[6].append(kn); outs[7].append(fv); outs[8].append(lf[:, :FOX_H])
        outs[9].append(jnp.transpose(s_t, (3, 0, 1, 2))); outs[10].append(p3[:, -1])
        outs[11].append(jnp.transpose(r_t, (3, 0, 1, 2)))

    n_pp = nb * t // PAGE_SIZE
    st = lambda i: jnp.stack(outs[i])
    page_rows = lambda z: jnp.transpose(z.reshape(depth, n_pp, FOX_H, HEAD_DIM, PAGE_SIZE), (0, 1, 4, 2, 3))
    return (yp.reshape(nb, t, D_MODEL), ys.reshape(db, ts, D_MODEL),
            page_rows(st(0)), page_rows(st(1)), jnp.swapaxes(st(2), 2, 3),
            st(3), st(4), st(5),
            st(6).reshape(depth, db, ts, FOX_H, HEAD_DIM),
            st(7).reshape(depth, db, ts, FOX_H, HEAD_DIM),
            st(8).reshape(depth, db, ts, FOX_H),
            st(9), st(10), st(11))
```
